```python
import math
import jax, jax.numpy as jnp
from jax import lax
import numpy as np

D_MODEL = 2048
BATCH = 2
SEQ = 4096
DEPTH = 2

GRID_W = 64
CTX_LEN = 256
EPS = 1e-6
NEG = -1e30

HEAD_DIM = 128
W_A = D_MODEL // 4
W_B = D_MODEL // 2
W_C = D_MODEL // 4
MIX_W = W_A + W_B + W_C

CHUNK = 128
A_GROUPS = W_A // HEAD_DIM

B_HEADS = W_B // HEAD_DIM
Q_LORA = D_MODEL // 4
KV_LORA = 512
NOPE_DIM = 128
ROPE_DIM = 64
V_DIM = 128
MLA_SCALE = (NOPE_DIM + ROPE_DIM) ** -0.5
ROPE_THETA = 10000.0

C_HEADS = W_C // HEAD_DIM
MAX_KH = 8
KW = 16
C_SCALE = HEAD_DIM ** -0.5

Q_BLOCK = 128
SPLIT_WIDTHS = (W_A, W_A, W_A, Q_LORA, KV_LORA, ROPE_DIM, W_B, W_C, W_C, W_C, W_C)
IN_W = 3 * W_A + Q_LORA + KV_LORA + ROPE_DIM + W_B + 4 * W_C

kernel_name = "hybrid_gmlp_mla_natten_prefix_block"


def rmsnorm(x, g):
    xf = x.astype(jnp.float32)
    y = xf * lax.rsqrt(jnp.mean(xf * xf, axis=-1, keepdims=True) + EPS)
    return (y * g.astype(jnp.float32)).astype(x.dtype)


def split_in(z):
    offs, acc = [], 0
    for w in SPLIT_WIDTHS[:-1]:
        acc += w
        offs.append(acc)
    return jnp.split(z, offs, axis=-1)


def axial_angles(n_tokens):
    t = jnp.arange(n_tokens)
    row = (t // GRID_W).astype(jnp.float32)
    col = (t % GRID_W).astype(jnp.float32)
    n_freq = ROPE_DIM // 4
    inv = ROPE_THETA ** (-jnp.arange(n_freq, dtype=jnp.float32) / n_freq)
    return row[:, None] * inv, col[:, None] * inv


def _rotate(x, ang):
    m = x.shape[-1] // 2
    x1, x2 = x[..., :m], x[..., m:]
    cos = jnp.cos(ang).astype(x.dtype)
    sin = jnp.sin(ang).astype(x.dtype)
    return jnp.concatenate([x1 * cos - x2 * sin, x2 * cos + x1 * sin], axis=-1)


def axial_rope(x, ang_row, ang_col):
    half = ROPE_DIM // 2
    return jnp.concatenate([_rotate(x[..., :half], ang_row), _rotate(x[..., half:], ang_col)], axis=-1)


def chunk_gmlp(u, v, sgu_g, w_s, b_s):
    B, N, _ = u.shape
    u = jax.nn.gelu(u)
    v = rmsnorm(jax.nn.gelu(v), sgu_g).reshape(B, N // CHUNK, CHUNK, A_GROUPS, HEAD_DIM)
    s = jnp.einsum('gpq,bnqgc->bnpgc', w_s, v) + b_s.T[None, None, :, :, None]
    return u * s.reshape(B, N, W_A)


def mla_q(cq, qa_g, w_uq):
    B, N, _ = cq.shape
    q = (rmsnorm(cq, qa_g) @ w_uq).reshape(B, N, B_HEADS, NOPE_DIM + ROPE_DIM)
    return q[..., :NOPE_DIM], q[..., NOPE_DIM:]


def mla_kv(ckv, kva_g, w_ukv):
    B, N, _ = ckv.shape
    kv = (rmsnorm(ckv, kva_g) @ w_ukv).reshape(B, N, B_HEADS, NOPE_DIM + V_DIM)
    return kv[..., :NOPE_DIM], kv[..., NOPE_DIM:]


def mla_latent_attention(qn, qr, kn, kr, v, kn_c, kr_c, v_c):
    B, S, H, _ = qn.shape
    n_blocks = S // Q_BLOCK

    def block(i):
        qbn = lax.dynamic_slice_in_dim(qn, i * Q_BLOCK, Q_BLOCK, axis=1)
        qbr = lax.dynamic_slice_in_dim(qr, i * Q_BLOCK, Q_BLOCK, axis=1)
        s_lat = jnp.einsum('bqhd,bkhd->bhqk', qbn, kn) + jnp.einsum('bqhr,bkr->bhqk', qbr, kr)
        s_ctx = jnp.einsum('bqhd,bkhd->bhqk', qbn, kn_c) + jnp.einsum('bqhr,bkr->bhqk', qbr, kr_c)
        s = jnp.concatenate([s_lat, s_ctx], axis=-1).astype(jnp.float32) * MLA_SCALE
        p = jax.nn.softmax(s, axis=-1).astype(v.dtype)
        return (jnp.einsum('bhqk,bkhd->bqhd', p[..., :S], v)
                + jnp.einsum('bhqk,bkhd->bqhd', p[..., S:], v_c))

    out = lax.map(block, jnp.arange(n_blocks))
    return out.transpose(1, 0, 2, 3, 4).reshape(B, S, H * V_DIM)


def mla_context_attention(qn, qr, kn, kr, v):
    B, L, H, _ = qn.shape
    s = jnp.einsum('bqhd,bkhd->bhqk', qn, kn) + jnp.einsum('bqhr,bkr->bhqk', qr, kr)
    p = jax.nn.softmax(s.astype(jnp.float32) * MLA_SCALE, axis=-1).astype(v.dtype)
    return jnp.einsum('bhqk,bkhd->bqhd', p, v).reshape(B, L, H * V_DIM)


def dense_attention(q, k, v, scale):
    B, L, H, D = q.shape
    s = jnp.einsum('bqhd,bkhd->bhqk', q, k).astype(jnp.float32) * scale
    p = jax.nn.softmax(s, axis=-1).astype(v.dtype)
    return jnp.einsum('bhqk,bkhd->bqhd', p, v).reshape(B, L, H * D)


def natten_latent(q, k, v, k_c, v_c, rpb):
    B, S, H, D = q.shape
    rows = S // GRID_W
    kh = min(MAX_KH, rows)
    r = jnp.arange(rows)
    col = jnp.arange(GRID_W)
    r0 = jnp.clip(r - kh // 2, 0, rows - kh)
    row_idx = r0[:, None] + jnp.arange(kh)[None, :]
    c0 = jnp.clip(col - KW // 2, 0, GRID_W - KW)
    col_ok = (col[None, :] >= c0[:, None]) & (col[None, :] < c0[:, None] + KW)
    dr = row_idx - r[:, None] + (MAX_KH - 1)
    dc = jnp.clip(col[None, :] - col[:, None], -(KW - 1), KW - 1) + (KW - 1)
    bias = rpb[:, dr[:, None, :, None], dc[None, :, None, :]].astype(jnp.float32)
    bias = jnp.where(col_ok[None, None, :, None, :], bias, NEG)
    bias = bias.reshape(H, rows, GRID_W, kh * GRID_W).transpose(1, 0, 2, 3)

    qg = q.reshape(B, rows, GRID_W, H, D)
    kg = k.reshape(B, rows, GRID_W, H, D)
    vg = v.reshape(B, rows, GRID_W, H, D)
    n_keys = kh * GRID_W
    ks = kg[:, row_idx].reshape(B, rows, n_keys, H, D)
    vs = vg[:, row_idx].reshape(B, rows, n_keys, H, D)
    s_nb = jnp.einsum('brqhd,brkhd->brhqk', qg, ks).astype(jnp.float32) * C_SCALE + bias[None]
    s_cx = jnp.einsum('brqhd,blhd->brhql', qg, k_c).astype(jnp.float32) * C_SCALE
    p = jax.nn.softmax(jnp.concatenate([s_nb, s_cx], axis=-1), axis=-1).astype(v.dtype)
    o = (jnp.einsum('brhqk,brkhd->brqhd', p[..., :n_keys], vs)
         + jnp.einsum('brhql,blhd->brqhd', p[..., n_keys:], v_c))
    return o.reshape(B, S, H * D)


def hybrid_layer(x, xc, mod, mod_c, norm_g, w_in, qa_g, kva_g, w_uq, w_ukv,
                 sgu_g, w_s, b_s, rpb, w_out, ang_r, ang_c, ctx_out):
    B, S, _ = x.shape
    L = xc.shape[1]
    shift, scale, gate = jnp.split(mod, 3, axis=-1)
    shift_c, scale_c, gate_c = jnp.split(mod_c, 3, axis=-1)
    h = rmsnorm(x, norm_g) * (1.0 + scale[:, None, :]) + shift[:, None, :]
    hc = rmsnorm(xc, norm_g) * (1.0 + scale_c) + shift_c

    u, v_a, g_a, cq, ckv, kr, g_b, q_c, k_c, v_c, g_c = split_in(h @ w_in)
    (u_x, v_ax, g_ax, cq_x, ckv_x, kr_x, g_bx,
     q_cx, k_cx, v_cx, g_cx) = split_in(hc @ w_in)

    a_out = chunk_gmlp(u, v_a, sgu_g, w_s, b_s)

    qn, qr = mla_q(cq, qa_g, w_uq)
    qr = axial_rope(qr, ang_r[:, None, :], ang_c[:, None, :])
    kn, v_b = mla_kv(ckv, kva_g, w_ukv)
    kr = axial_rope(kr, ang_r, ang_c)
    kn_x, v_bx = mla_kv(ckv_x, kva_g, w_ukv)
    b_out = mla_latent_attention(qn, qr, kn, kr, v_b, kn_x, kr_x, v_bx)

    hd = lambda t, n: t.reshape(B, n, C_HEADS, HEAD_DIM)
    k_cx_h, v_cx_h = hd(k_cx, L), hd(v_cx, L)
    c_out = natten_latent(hd(q_c, S), hd(k_c, S), hd(v_c, S), k_cx_h, v_cx_h, rpb)

    y = jnp.concatenate([a_out * jax.nn.silu(g_a), b_out * jax.nn.silu(g_b),
                         c_out * jax.nn.silu(g_c)], axis=-1) @ w_out
    x_new = x + gate[:, None, :] * y

    if ctx_out:
        a_x = chunk_gmlp(u_x, v_ax, sgu_g, w_s, b_s)
        qn_x, qr_x = mla_q(cq_x, qa_g, w_uq)
        b_x = mla_context_attention(qn_x, qr_x, kn_x, kr_x, v_bx)
        c_x = dense_attention(hd(q_cx, L), k_cx_h, v_cx_h, C_SCALE)
        yc = jnp.concatenate([a_x * jax.nn.silu(g_ax), b_x * jax.nn.silu(g_bx),
                              c_x * jax.nn.silu(g_cx)], axis=-1) @ w_out
        xc = xc + gate_c * yc
    return x_new, xc


def setup_inputs(seed: int = 0) -> dict:
    key = jax.random.key(seed)
    ks = jax.random.split(key, 20)
    nrm = lambda k, shape, s: jax.random.normal(k, shape, jnp.float32) * s
    return {
        "x": nrm(ks[0], (BATCH, SEQ, D_MODEL), 1.0),
        "c": nrm(ks[1], (BATCH, D_MODEL), 1.0),
        "ctx": nrm(ks[2], (BATCH, CTX_LEN, D_MODEL), 1.0),
        "c_ctx": nrm(ks[3], (D_MODEL,), 1.0),
        "w_ada": nrm(ks[4], (DEPTH, D_MODEL, 3 * D_MODEL), 0.5 * D_MODEL ** -0.5),
        "b_ada": nrm(ks[5], (DEPTH, 3 * D_MODEL), 0.02),
        "norm_g": 1.0 + nrm(ks[6], (DEPTH, D_MODEL), 0.02),
        "w_in": nrm(ks[7], (DEPTH, D_MODEL, IN_W), D_MODEL ** -0.5),
        "qa_g": 1.0 + nrm(ks[8], (DEPTH, Q_LORA), 0.02),
        "kva_g": 1.0 + nrm(ks[9], (DEPTH, KV_LORA), 0.02),
        "w_uq": nrm(ks[10], (DEPTH, Q_LORA, B_HEADS * (NOPE_DIM + ROPE_DIM)), Q_LORA ** -0.5),
        "w_ukv": nrm(ks[11], (DEPTH, KV_LORA, B_HEADS * (NOPE_DIM + V_DIM)), KV_LORA ** -0.5),
        "sgu_g": 1.0 + nrm(ks[12], (DEPTH, W_A), 0.02),
        "w_s": nrm(ks[13], (DEPTH, A_GROUPS, CHUNK, CHUNK), CHUNK ** -0.5),
        "b_s": 1.0 + nrm(ks[14], (DEPTH, A_GROUPS, CHUNK), 0.02),
        "rpb": nrm(ks[15], (DEPTH, C_HEADS, 2 * MAX_KH - 1, 2 * KW - 1), 0.1),
        "w_out": nrm(ks[16], (DEPTH, MIX_W, D_MODEL), MIX_W ** -0.5),
        "final_g": 1.0 + nrm(ks[17], (D_MODEL,), 0.02),
    }


def reference(x, c, ctx, c_ctx, w_ada, b_ada, norm_g, w_in, qa_g, kva_g, w_uq, w_ukv,
              sgu_g, w_s, b_s, rpb, w_out, final_g):
    silu_c = jax.nn.silu(c)
    silu_cc = jax.nn.silu(c_ctx)
    ang_r, ang_c = axial_angles(x.shape[1])
    xc = ctx
    for l in range(DEPTH):
        mod = silu_c @ w_ada[l] + b_ada[l]
        mod_c = silu_cc @ w_ada[l] + b_ada[l]
        x, xc = hybrid_layer(x, xc, mod, mod_c, norm_g[l], w_in[l], qa_g[l], kva_g[l],
                             w_uq[l], w_ukv[l], sgu_g[l], w_s[l], b_s[l], rpb[l], w_out[l],
                             ang_r, ang_c, l < DEPTH - 1)
    return rmsnorm(x, final_g)
```

```python
import functools
import math

import numpy as np
import jax
import jax.numpy as jnp
from jax import lax
from jax.experimental import pallas as pl
from jax.experimental.pallas import tpu as pltpu

D_MODEL = 2048
BATCH = 2
SEQ = 4096
DEPTH = 2
GRID_W = 64
CTX_LEN = 256
EPS = 1e-6
NEG = -1e30
HEAD_DIM = 128
W_A = D_MODEL // 4
W_B = D_MODEL // 2
W_C = D_MODEL // 4
CHUNK = 128
A_GROUPS = W_A // HEAD_DIM
B_HEADS = W_B // HEAD_DIM
Q_LORA = D_MODEL // 4
KV_LORA = 512
NOPE_DIM = 128
ROPE_DIM = 64
V_DIM = 128
MLA_SCALE = (NOPE_DIM + ROPE_DIM) ** -0.5
ROPE_THETA = 10000.0
C_HEADS = W_C // HEAD_DIM
MAX_KH = 8
KW = 16
C_SCALE = HEAD_DIM ** -0.5
ROWS = SEQ // GRID_W

LANES = 128
MXU_DIM = 256
VMEM_BYTES_V7X = 64 * 1024 * 1024
VMEM_LIMIT_CAP = 56 * 1024 * 1024

F32 = jnp.float32
BF16 = jnp.bfloat16

N_LAT = BATCH * SEQ
N_CTX = BATCH * CTX_LEN
N_TOK = N_LAT + N_CTX
ROW_TILE = N_CTX
LAT_TILES = N_LAT // ROW_TILE
ROW_TILES = N_TOK // ROW_TILE
TILES_PER_SAMPLE = SEQ // ROW_TILE
Q_TILE = CTX_LEN
Q_TILES_PER_SAMPLE = SEQ // Q_TILE
KV_CHUNK = 256
assert ROW_TILE % CHUNK == 0 and SEQ % ROW_TILE == 0 and SEQ % Q_TILE == 0

OFF_GB = 0
OFF_U = OFF_GB + W_B
OFF_VA = OFF_U + W_A
OFF_GA = OFF_VA + W_A
OFF_CQ = OFF_GA + W_A
OFF_CKV = OFF_CQ + Q_LORA
OFF_QC = OFF_CKV + KV_LORA
OFF_KC = OFF_QC + W_C
OFF_VC = OFF_KC + W_C
OFF_GC = OFF_VC + W_C
OFF_KR = OFF_GC + W_C
Z_W = OFF_KR + 4 * ROPE_DIM
Z_CHUNK = 512
assert Z_W % MXU_DIM == 0 and OFF_KR % (4 * ROPE_DIM) == 0

_SRC = {}
_acc = 0
for _name, _w in (("u", W_A), ("va", W_A), ("ga", W_A), ("cq", Q_LORA), ("ckv", KV_LORA), ("kr", ROPE_DIM),
                  ("gb", W_B), ("qc", W_C), ("kc", W_C), ("vc", W_C), ("gc", W_C)):
    _SRC[_name] = (_acc, _acc + _w)
    _acc += _w
IN_W = _acc

_Q4 = ROPE_DIM // 4
_SWAP = np.concatenate([np.arange(_Q4, 2 * _Q4), np.arange(0, _Q4),
                        np.arange(3 * _Q4, 4 * _Q4), np.arange(2 * _Q4, 3 * _Q4)])


def _vmem_limit(nbytes):
    return int(min(VMEM_LIMIT_CAP, max(16 * 1024 * 1024, nbytes * 5 // 4)))


def _silu(x):
    return x * jax.nn.sigmoid(x)


def _rms(x, g):
    return x * lax.rsqrt(jnp.mean(x * x, axis=-1, keepdims=True) + EPS) * g


MOD_ROWS = 8
MOD_TN = 768


def _mod_kernel(c_ref, w_ref, b_ref, o_ref):
    a = _silu(c_ref[...])
    o_ref[0] = jnp.dot(a, w_ref[0], preferred_element_type=F32, precision=lax.Precision.HIGHEST) + b_ref[0]


def _modulation(cc, w_ada, b_ada):
    n = 3 * D_MODEL
    est = 2 * (MOD_ROWS * D_MODEL * 4 + D_MODEL * MOD_TN * 4 + 2 * MOD_ROWS * MOD_TN * 4)
    return pl.pallas_call(
        _mod_kernel,
        grid=(DEPTH, n // MOD_TN),
        in_specs=[pl.BlockSpec((MOD_ROWS, D_MODEL), lambda l, j: (0, 0)),
                  pl.BlockSpec((1, D_MODEL, MOD_TN), lambda l, j: (l, 0, j)),
                  pl.BlockSpec((1, 1, MOD_TN), lambda l, j: (l, 0, j))],
        out_specs=pl.BlockSpec((1, MOD_ROWS, MOD_TN), lambda l, j: (l, 0, j)),
        out_shape=jax.ShapeDtypeStruct((DEPTH, MOD_ROWS, n), F32),
        compiler_params=pltpu.CompilerParams(dimension_semantics=("parallel", "parallel"),
                                             vmem_limit_bytes=_vmem_limit(est)),
    )(cc, w_ada, b_ada.reshape(DEPTH, 1, n))


def _mod_row(t):
    return jnp.where(t < LAT_TILES, t // TILES_PER_SAMPLE, BATCH)


def _inproj_kernel(x_ref, mod_ref, g_ref, w_ref, o_ref, h_ref):
    x = x_ref[...]
    shift = mod_ref[0, :, 0:D_MODEL]
    scale = mod_ref[0, :, D_MODEL:2 * D_MODEL]
    h_ref[...] = (_rms(x, g_ref[...]) * (1.0 + scale) + shift).astype(BF16)
    for c0 in range(0, Z_W, Z_CHUNK):
        c1 = min(c0 + Z_CHUNK, Z_W)
        o_ref[:, c0:c1] = jnp.dot(h_ref[...], w_ref[:, c0:c1], preferred_element_type=F32).astype(BF16)


def _inproj(xf, mods_l, norm_g, w_in_p):
    est = (2 * ROW_TILE * D_MODEL * 4 + D_MODEL * Z_W * 2 + 2 * ROW_TILE * Z_W * 2 + ROW_TILE * D_MODEL * 2
           + 2 * ROW_TILE * Z_CHUNK * 4)
    return pl.pallas_call(
        _inproj_kernel,
        grid=(ROW_TILES,),
        in_specs=[pl.BlockSpec((ROW_TILE, D_MODEL), lambda t: (t, 0)),
                  pl.BlockSpec((1, 1, 3 * D_MODEL), lambda t: (_mod_row(t), 0, 0)),
                  pl.BlockSpec((1, D_MODEL), lambda t: (0, 0)),
                  pl.BlockSpec((D_MODEL, Z_W), lambda t: (0, 0), pipeline_mode=pl.Buffered(1))],
        out_specs=pl.BlockSpec((ROW_TILE, Z_W), lambda t: (t, 0)),
        out_shape=jax.ShapeDtypeStruct((N_TOK, Z_W), BF16),
        scratch_shapes=[pltpu.VMEM((ROW_TILE, D_MODEL), BF16)],
        compiler_params=pltpu.CompilerParams(dimension_semantics=("parallel",),
                                             vmem_limit_bytes=_vmem_limit(est)),
    )(xf, mods_l.reshape(MOD_ROWS, 1, 3 * D_MODEL), norm_g.reshape(1, D_MODEL), w_in_p)


QK_W = 2 * HEAD_DIM
UQ_W = B_HEADS * (NOPE_DIM + 2 * ROPE_DIM)
UKV_W = B_HEADS * (NOPE_DIM + V_DIM)
_Q_PRESCALE = MLA_SCALE * math.log2(math.e)
assert B_HEADS % 2 == 0 and 2 * ROPE_DIM == LANES


def _upproj_kernel(cq_ref, ckv_ref, kr_ref, cos_ref, sin_ref, qg_ref, kg_ref, wq_ref, wkv_ref,
                   q_ref, k_ref, vt_ref):
    cos2 = cos_ref[...]
    sin2 = sin_ref[...]
    n_rope = B_HEADS * ROPE_DIM
    cqn = _rms(cq_ref[...].astype(F32), qg_ref[...]).astype(BF16)
    q_all = jnp.dot(cqn, wq_ref[...], preferred_element_type=F32)
    rope0 = B_HEADS * NOPE_DIM
    for h in range(B_HEADS):
        q_ref[h, :, 0:NOPE_DIM] = (q_all[:, h * NOPE_DIM:(h + 1) * NOPE_DIM] * _Q_PRESCALE).astype(BF16)
    for j in range(B_HEADS // 2):
        a = q_all[:, rope0 + j * LANES: rope0 + (j + 1) * LANES]
        a_sw = q_all[:, rope0 + n_rope + j * LANES: rope0 + n_rope + (j + 1) * LANES]
        rot = ((a * cos2 + a_sw * sin2) * _Q_PRESCALE).astype(BF16)
        q_ref[2 * j, :, NOPE_DIM:QK_W] = rot
        q_ref[2 * j + 1, :, NOPE_DIM:QK_W] = rot

    ckvn = _rms(ckv_ref[...].astype(F32), kg_ref[...]).astype(BF16)
    kv_all = jnp.dot(ckvn, wkv_ref[...], preferred_element_type=F32)
    kr = kr_ref[...].astype(F32)
    krot = kr[:, 0:LANES] * cos2 + kr[:, LANES:2 * LANES] * sin2
    lane = lax.broadcasted_iota(jnp.int32, krot.shape, 1)
    k_lo = jnp.where(lane < ROPE_DIM, krot, 0.0).astype(BF16)
    k_hi = jnp.where(lane >= ROPE_DIM, krot, 0.0).astype(BF16)
    v0 = B_HEADS * NOPE_DIM
    for h in range(B_HEADS):
        k_ref[h, :, 0:NOPE_DIM] = kv_all[:, h * NOPE_DIM:(h + 1) * NOPE_DIM].astype(BF16)
        k_ref[h, :, NOPE_DIM:QK_W] = k_lo if h % 2 == 0 else k_hi
        v_h = kv_all[:, v0 + h * V_DIM: v0 + (h + 1) * V_DIM]
        for c in range(ROW_TILE // KV_CHUNK):
            vt_ref[h, c] = v_h[c * KV_CHUNK:(c + 1) * KV_CHUNK, :].T.astype(BF16)


def _upproj(z, cos2, sin2, qa_g, kva_g, w_uq_p, w_ukv_p):
    tm = ROW_TILE
    est = 2 * (2 * tm * Q_LORA * 2 + tm * 4 * ROPE_DIM * 2 + 2 * tm * LANES * 4 + Q_LORA * UQ_W * 2
               + KV_LORA * UKV_W * 2 + B_HEADS * tm * (2 * QK_W + V_DIM) * 2) + 4 * tm * UQ_W * 4
    return pl.pallas_call(
        _upproj_kernel,
        grid=(ROW_TILES,),
        in_specs=[pl.BlockSpec((tm, Q_LORA), lambda t: (t, OFF_CQ // Q_LORA)),
                  pl.BlockSpec((tm, KV_LORA), lambda t: (t, OFF_CKV // KV_LORA)),
                  pl.BlockSpec((tm, 4 * ROPE_DIM), lambda t: (t, OFF_KR // (4 * ROPE_DIM))),
                  pl.BlockSpec((tm, LANES), lambda t: (t, 0)),
                  pl.BlockSpec((tm, LANES), lambda t: (t, 0)),
                  pl.BlockSpec((1, Q_LORA), lambda t: (0, 0)),
                  pl.BlockSpec((1, KV_LORA), lambda t: (0, 0)),
                  pl.BlockSpec((Q_LORA, UQ_W), lambda t: (0, 0)),
                  pl.BlockSpec((KV_LORA, UKV_W), lambda t: (0, 0))],
        out_specs=[pl.BlockSpec((B_HEADS, tm, QK_W), lambda t: (0, t, 0)),
                   pl.BlockSpec((B_HEADS, tm, QK_W), lambda t: (0, t, 0)),
                   pl.BlockSpec((B_HEADS, tm // KV_CHUNK, V_DIM, KV_CHUNK), lambda t: (0, t, 0, 0))],
        out_shape=[jax.ShapeDtypeStruct((B_HEADS, N_TOK, QK_W), BF16),
                   jax.ShapeDtypeStruct((B_HEADS, N_TOK, QK_W), BF16),
                   jax.ShapeDtypeStruct((B_HEADS, N_TOK // KV_CHUNK, V_DIM, KV_CHUNK), BF16)],
        compiler_params=pltpu.CompilerParams(dimension_semantics=("parallel",),
                                             vmem_limit_bytes=_vmem_limit(est)),
    )(z, z, z, cos2, sin2, qa_g.reshape(1, Q_LORA), kva_g.reshape(1, KV_LORA), w_uq_p, w_ukv_p)


def _gmlp_kernel(u_ref, v_ref, g_ref, sg_ref, ws_ref, bs_ref, o_ref):
    v = jax.nn.gelu(v_ref[...].astype(F32))
    vn = _rms(v, sg_ref[...]).astype(BF16)
    front = jax.nn.gelu(u_ref[...].astype(F32))
    gate = _silu(g_ref[...].astype(F32))
    for c in range(ROW_TILE // CHUNK):
        r0, r1 = c * CHUNK, (c + 1) * CHUNK
        for g in range(A_GROUPS):
            c0, c1 = g * HEAD_DIM, (g + 1) * HEAD_DIM
            s = jnp.dot(ws_ref[g], vn[r0:r1, c0:c1], preferred_element_type=F32) + bs_ref[:, c0:c1]
            o_ref[r0:r1, c0:c1] = (front[r0:r1, c0:c1] * s * gate[r0:r1, c0:c1]).astype(BF16)


def _gmlp(z, sgu_g, w_s_b, b_s_exp, n_tiles):
    tm = ROW_TILE
    est = 2 * (4 * tm * W_A * 2 + A_GROUPS * CHUNK * CHUNK * 2 + CHUNK * W_A * 4) + 5 * tm * W_A * 4
    return pl.pallas_call(
        _gmlp_kernel,
        grid=(n_tiles,),
        in_specs=[pl.BlockSpec((tm, W_A), lambda t: (t, OFF_U // W_A)),
                  pl.BlockSpec((tm, W_A), lambda t: (t, OFF_VA // W_A)),
                  pl.BlockSpec((tm, W_A), lambda t: (t, OFF_GA // W_A)),
                  pl.BlockSpec((1, W_A), lambda t: (0, 0)),
                  pl.BlockSpec((A_GROUPS, CHUNK, CHUNK), lambda t: (0, 0, 0)),
                  pl.BlockSpec((CHUNK, W_A), lambda t: (0, 0))],
        out_specs=pl.BlockSpec((tm, W_A), lambda t: (t, 0)),
        out_shape=jax.ShapeDtypeStruct((n_tiles * tm, W_A), BF16),
        compiler_params=pltpu.CompilerParams(dimension_semantics=("parallel",),
                                             vmem_limit_bytes=_vmem_limit(est)),
    )(z, z, z, sgu_g.reshape(1, W_A), w_s_b, b_s_exp)


MLA_TK = 512
assert SEQ % MLA_TK == 0 and MLA_TK % KV_CHUNK == 0


def _q_row_tile(b, i):
    return jnp.where(i < Q_TILES_PER_SAMPLE, b * Q_TILES_PER_SAMPLE + i, N_LAT // Q_TILE + b)


def _mla_kernel(q_ref, kl_ref, vtl_ref, kc_ref, vtc_ref, gb_ref, o_ref, m_ref, l_ref, acc_ref):
    i = pl.program_id(2)
    q = q_ref[0]
    m_ref[...] = jnp.full(m_ref.shape, -jnp.inf, F32)
    l_ref[...] = jnp.zeros(l_ref.shape, F32)
    acc_ref[...] = jnp.zeros(acc_ref.shape, F32)

    def step(k, vts):
        s = lax.dot_general(k, q, (((1,), (1,)), ((), ())), preferred_element_type=F32)
        m_prev = m_ref[...]
        m_new = jnp.maximum(m_prev, jnp.max(s, axis=0, keepdims=True))
        alpha = jnp.exp2(m_prev - m_new)
        p = jnp.exp2(s - m_new)
        l_ref[...] = alpha * l_ref[...] + jnp.sum(p, axis=0, keepdims=True)
        pb = p.astype(BF16)
        pv = jnp.dot(vts[0], pb[0:KV_CHUNK], preferred_element_type=F32)
        for n in range(1, len(vts)):
            pv += jnp.dot(vts[n], pb[n * KV_CHUNK:(n + 1) * KV_CHUNK], preferred_element_type=F32)
        acc_ref[...] = alpha * acc_ref[...] + pv
        m_ref[...] = m_new

    @pl.when(i < Q_TILES_PER_SAMPLE)
    def _():
        def body(j, carry):
            off = pl.multiple_of(j * MLA_TK, MLA_TK)
            n0 = j * (MLA_TK // KV_CHUNK)
            step(kl_ref[0, pl.ds(off, MLA_TK), :], [vtl_ref[0, n0 + n] for n in range(MLA_TK // KV_CHUNK)])
            return carry
        lax.fori_loop(0, SEQ // MLA_TK, body, 0)

    step(kc_ref[0], [vtc_ref[0, 0]])
    o = (acc_ref[...] / l_ref[...]).T
    o_ref[...] = (o * _silu(gb_ref[...].astype(F32))).astype(BF16)


def _mla(q, k, vt, z, with_ctx_queries):
    nq = Q_TILES_PER_SAMPLE + (1 if with_ctx_queries else 0)
    n_rows = N_TOK if with_ctx_queries else N_LAT
    ctx_tile = N_LAT // Q_TILE
    est = 2 * (Q_TILE * QK_W * 2 + SEQ * QK_W * 2 + SEQ * V_DIM * 2 + CTX_LEN * (QK_W + V_DIM) * 2
               + 2 * Q_TILE * HEAD_DIM * 2) + 6 * MLA_TK * Q_TILE * 4
    return pl.pallas_call(
        _mla_kernel,
        grid=(BATCH, B_HEADS, nq),
        in_specs=[pl.BlockSpec((1, Q_TILE, QK_W), lambda b, h, i: (h, _q_row_tile(b, i), 0)),
                  pl.BlockSpec((1, SEQ, QK_W), lambda b, h, i: (h, b, 0)),
                  pl.BlockSpec((1, SEQ // KV_CHUNK, V_DIM, KV_CHUNK), lambda b, h, i: (h, b, 0, 0)),
                  pl.BlockSpec((1, CTX_LEN, QK_W), lambda b, h, i: (h, ctx_tile + b, 0)),
                  pl.BlockSpec((1, CTX_LEN // KV_CHUNK, V_DIM, KV_CHUNK), lambda b, h, i: (h, ctx_tile + b, 0, 0)),
                  pl.BlockSpec((Q_TILE, HEAD_DIM), lambda b, h, i: (_q_row_tile(b, i), OFF_GB // HEAD_DIM + h))],
        out_specs=pl.BlockSpec((Q_TILE, HEAD_DIM), lambda b, h, i: (_q_row_tile(b, i), h)),
        out_shape=jax.ShapeDtypeStruct((n_rows, W_B), BF16),
        scratch_shapes=[pltpu.VMEM((1, Q_TILE), F32), pltpu.VMEM((1, Q_TILE), F32),
                        pltpu.VMEM((V_DIM, Q_TILE), F32)],
        compiler_params=pltpu.CompilerParams(dimension_semantics=("parallel", "parallel", "arbitrary"),
                                             vmem_limit_bytes=_vmem_limit(est)),
    )(q, k, vt, k, vt, z)


NA_QROWS = Q_TILE // GRID_W
NA_KROWS = 12
NA_KEYS = NA_KROWS * GRID_W
assert NA_QROWS + MAX_KH <= NA_KROWS + 1 and ROWS >= NA_KROWS


def _na_strip_row(j):
    return int(np.clip(j * NA_QROWS - MAX_KH // 2, 0, ROWS - NA_KROWS))


def _na_tables():
    def one(j):
        s = _na_strip_row(j)
        valid = np.zeros((NA_QROWS, NA_KROWS), bool)
        d = np.zeros((NA_QROWS, NA_KROWS), np.int32)
        for a in range(NA_QROWS):
            r = j * NA_QROWS + a
            r0 = int(np.clip(r - MAX_KH // 2, 0, ROWS - MAX_KH))
            for i in range(NA_KROWS):
                valid[a, i] = 0 <= s + i - r0 < MAX_KH
                d[a, i] = s + i - r + (MAX_KH - 1) if valid[a, i] else 0
        return valid, d
    n_tiles = ROWS // NA_QROWS
    first, mid, last = one(0), one(1), one(n_tiles - 1)
    for j in range(1, n_tiles - 1):
        v, d = one(j)
        assert (v == mid[0]).all() and (d == mid[1]).all()
    return np.stack([first[0], mid[0], last[0]]), np.stack([first[1], mid[1], last[1]])


_NA_VALID, _NA_DROW = _na_tables()


def _na_class(j):
    n_tiles = ROWS // NA_QROWS
    return jnp.where(j == 0, 0, jnp.where(j == n_tiles - 1, 2, 1))


def _na_bias_tables(rpb_l):
    col = np.arange(GRID_W)
    c0 = np.clip(col - KW // 2, 0, GRID_W - KW)
    col_ok = (col[None, :] >= c0[:, None]) & (col[None, :] < c0[:, None] + KW)
    dc = np.clip(col[None, :] - col[:, None], -(KW - 1), KW - 1) + (KW - 1)
    t1 = jnp.take(rpb_l, jnp.asarray(dc.reshape(-1)), axis=2).reshape(C_HEADS, 2 * MAX_KH - 1, GRID_W, GRID_W)
    t1 = jnp.where(jnp.asarray(col_ok)[None, None], t1, NEG)
    t2 = jnp.take(t1, jnp.asarray(_NA_DROW.reshape(-1)), axis=1)
    t2 = t2.reshape(C_HEADS, 3, NA_QROWS, NA_KROWS, GRID_W, GRID_W)
    t2 = jnp.where(jnp.asarray(_NA_VALID)[None, :, :, :, None, None], t2, NEG)
    return t2.transpose(1, 0, 2, 4, 3, 5).reshape(3, C_HEADS, Q_TILE, NA_KEYS)


def _natten_kernel(q_ref, kl_ref, vl_ref, kc_ref, vc_ref, bias_ref, g_ref, o_ref, m_ref, l_ref, acc_ref):
    j = pl.program_id(2)
    q = q_ref[...]
    m_ref[...] = jnp.full(m_ref.shape, -jnp.inf, F32)
    l_ref[...] = jnp.zeros(l_ref.shape, F32)
    acc_ref[...] = jnp.zeros(acc_ref.shape, F32)

    def step(k, v, bias):
        s = lax.dot_general(q, k, (((1,), (1,)), ((), ())), preferred_element_type=F32) * C_SCALE
        if bias is not None:
            s = s + bias
        m_prev = m_ref[...]
        m_new = jnp.maximum(m_prev, jnp.max(s, axis=-1, keepdims=True))
        alpha = jnp.exp(m_prev - m_new)
        p = jnp.exp(s - m_new)
        l_ref[...] = alpha * l_ref[...] + jnp.sum(p, axis=-1, keepdims=True)
        acc_ref[...] = alpha * acc_ref[...] + jnp.dot(p.astype(BF16), v, preferred_element_type=F32)
        m_ref[...] = m_new

    step(kc_ref[...], vc_ref[...], None)

    @pl.when(j < Q_TILES_PER_SAMPLE)
    def _():
        row = jnp.clip(j * NA_QROWS - MAX_KH // 2, 0, ROWS - NA_KROWS)
        off = pl.multiple_of(row * GRID_W, GRID_W)
        step(kl_ref[pl.ds(off, NA_KEYS), :], vl_ref[pl.ds(off, NA_KEYS), :], bias_ref[0, 0])

    o = acc_ref[...] / l_ref[...]
    o_ref[...] = (o * _silu(g_ref[...].astype(F32))).astype(BF16)


def _natten(z, bias_tab, with_ctx_queries):
    nq = Q_TILES_PER_SAMPLE + (1 if with_ctx_queries else 0)
    n_rows = N_TOK if with_ctx_queries else N_LAT
    ctx_tile = N_LAT // CTX_LEN
    hd = HEAD_DIM
    est = 2 * (2 * Q_TILE * hd * 2 + 2 * SEQ * hd * 2 + 2 * CTX_LEN * hd * 2 + Q_TILE * NA_KEYS * 4
               + Q_TILE * hd * 2) + 6 * Q_TILE * NA_KEYS * 4
    return pl.pallas_call(
        _natten_kernel,
        grid=(BATCH, C_HEADS, nq),
        in_specs=[pl.BlockSpec((Q_TILE, hd), lambda b, h, j: (_q_row_tile(b, j), OFF_QC // hd + h)),
                  pl.BlockSpec((SEQ, hd), lambda b, h, j: (b, OFF_KC // hd + h)),
                  pl.BlockSpec((SEQ, hd), lambda b, h, j: (b, OFF_VC // hd + h)),
                  pl.BlockSpec((CTX_LEN, hd), lambda b, h, j: (ctx_tile + b, OFF_KC // hd + h)),
                  pl.BlockSpec((CTX_LEN, hd), lambda b, h, j: (ctx_tile + b, OFF_VC // hd + h)),
                  pl.BlockSpec((1, 1, Q_TILE, NA_KEYS), lambda b, h, j: (_na_class(j), h, 0, 0)),
                  pl.BlockSpec((Q_TILE, hd), lambda b, h, j: (_q_row_tile(b, j), OFF_GC // hd + h))],
        out_specs=pl.BlockSpec((Q_TILE, hd), lambda b, h, j: (_q_row_tile(b, j), h)),
        out_shape=jax.ShapeDtypeStruct((n_rows, W_C), BF16),
        scratch_shapes=[pltpu.VMEM((Q_TILE, 1), F32), pltpu.VMEM((Q_TILE, 1), F32),
                        pltpu.VMEM((Q_TILE, hd), F32)],
        compiler_params=pltpu.CompilerParams(dimension_semantics=("parallel", "parallel", "arbitrary"),
                                             vmem_limit_bytes=_vmem_limit(est)),
    )(z, z, z, z, z, bias_tab, z)


OUT_CHUNK = 512


def _outproj_kernel(ma_ref, mb_ref, mc_ref, w_ref, x_ref, mod_ref, fg_ref, o_ref, *, final):
    gate = mod_ref[0, :, 2 * D_MODEL:3 * D_MODEL]
    for c0 in range(0, D_MODEL, OUT_CHUNK):
        c1 = c0 + OUT_CHUNK
        y = jnp.dot(ma_ref[...], w_ref[0:W_A, c0:c1], preferred_element_type=F32)
        y += jnp.dot(mb_ref[...], w_ref[W_A:W_A + W_B, c0:c1], preferred_element_type=F32)
        y += jnp.dot(mc_ref[...], w_ref[W_A + W_B:D_MODEL, c0:c1], preferred_element_type=F32)
        o_ref[:, c0:c1] = x_ref[:, c0:c1] + gate[:, c0:c1] * y
    if final:
        o_ref[...] = _rms(o_ref[...], fg_ref[...])


def _outproj(ma, mb, mc, w_out_b, xf, mods_l, final_g, final):
    tm = ROW_TILE
    n_tiles = LAT_TILES if final else ROW_TILES
    est = (2 * tm * D_MODEL * 2 + D_MODEL * D_MODEL * 2 + 4 * tm * D_MODEL * 4 + 3 * tm * OUT_CHUNK * 4)
    return pl.pallas_call(
        functools.partial(_outproj_kernel, final=final),
        grid=(n_tiles,),
        in_specs=[pl.BlockSpec((tm, W_A), lambda t: (t, 0)),
                  pl.BlockSpec((tm, W_B), lambda t: (t, 0)),
                  pl.BlockSpec((tm, W_C), lambda t: (t, 0)),
                  pl.BlockSpec((D_MODEL, D_MODEL), lambda t: (0, 0), pipeline_mode=pl.Buffered(1)),
                  pl.BlockSpec((tm, D_MODEL), lambda t: (t, 0)),
                  pl.BlockSpec((1, 1, 3 * D_MODEL), lambda t: (_mod_row(t), 0, 0)),
                  pl.BlockSpec((1, D_MODEL), lambda t: (0, 0))],
        out_specs=pl.BlockSpec((tm, D_MODEL), lambda t: (t, 0)),
        out_shape=jax.ShapeDtypeStruct((n_tiles * tm, D_MODEL), F32),
        compiler_params=pltpu.CompilerParams(dimension_semantics=("parallel",),
                                             vmem_limit_bytes=_vmem_limit(est)),
    )(ma, mb, mc, w_out_b, xf, mods_l.reshape(MOD_ROWS, 1, 3 * D_MODEL), final_g.reshape(1, D_MODEL))


def _cols(w, name):
    a, b = _SRC[name]
    return w[:, a:b]


def _prep_w_in(w):
    kr = _cols(w, "kr")
    kr_sw = jnp.concatenate([kr[:, _Q4:2 * _Q4], kr[:, 0:_Q4], kr[:, 3 * _Q4:4 * _Q4], kr[:, 2 * _Q4:3 * _Q4]], axis=1)
    parts = [_cols(w, n) for n in ("gb", "u", "va", "ga", "cq", "ckv", "qc", "kc", "vc", "gc")]
    return jnp.concatenate(parts + [kr, kr, kr_sw, kr_sw], axis=1).astype(BF16)


def _prep_w_uq(w):
    w3 = w.reshape(Q_LORA, B_HEADS, NOPE_DIM + ROPE_DIM)
    nope = w3[:, :, :NOPE_DIM].reshape(Q_LORA, B_HEADS * NOPE_DIM)
    rope = w3[:, :, NOPE_DIM:]
    rope_sw = jnp.concatenate([rope[..., _Q4:2 * _Q4], rope[..., 0:_Q4],
                               rope[..., 3 * _Q4:4 * _Q4], rope[..., 2 * _Q4:3 * _Q4]], axis=-1)
    return jnp.concatenate([nope, rope.reshape(Q_LORA, -1), rope_sw.reshape(Q_LORA, -1)], axis=1).astype(BF16)


def _prep_w_ukv(w):
    w3 = w.reshape(KV_LORA, B_HEADS, NOPE_DIM + V_DIM)
    return jnp.concatenate([w3[:, :, :NOPE_DIM].reshape(KV_LORA, -1),
                            w3[:, :, NOPE_DIM:].reshape(KV_LORA, -1)], axis=1).astype(BF16)


def _rope_tables():
    t = jnp.arange(SEQ)
    row = (t // GRID_W).astype(F32)
    col = (t % GRID_W).astype(F32)
    inv = ROPE_THETA ** (-jnp.arange(_Q4, dtype=F32) / _Q4)
    ar, ac = row[:, None] * inv, col[:, None] * inv
    cos = jnp.concatenate([jnp.cos(ar), jnp.cos(ar), jnp.cos(ac), jnp.cos(ac)], axis=1)
    sin = jnp.concatenate([-jnp.sin(ar), jnp.sin(ar), -jnp.sin(ac), jnp.sin(ac)], axis=1)
    cos = jnp.concatenate([jnp.tile(cos, (BATCH, 1)), jnp.ones((N_CTX, ROPE_DIM), F32)], axis=0)
    sin = jnp.concatenate([jnp.tile(sin, (BATCH, 1)), jnp.zeros((N_CTX, ROPE_DIM), F32)], axis=0)
    return jnp.concatenate([cos, cos], axis=1), jnp.concatenate([sin, sin], axis=1)


def kernel(x, c, ctx, c_ctx, w_ada, b_ada, norm_g, w_in, qa_g, kva_g, w_uq, w_ukv, sgu_g, w_s, b_s, rpb,
           w_out, final_g):
    assert x.shape == (BATCH, SEQ, D_MODEL) and ctx.shape == (BATCH, CTX_LEN, D_MODEL)
    assert w_in.shape == (DEPTH, D_MODEL, IN_W)
    xf = jnp.concatenate([x.reshape(N_LAT, D_MODEL), ctx.reshape(N_CTX, D_MODEL)], axis=0)
    cc = jnp.zeros((MOD_ROWS, D_MODEL), F32).at[0:BATCH].set(c).at[BATCH].set(c_ctx)
    mods = _modulation(cc, w_ada, b_ada)
    cos2, sin2 = _rope_tables()

    for l in range(DEPTH):
        last = l == DEPTH - 1
        z = _inproj(xf, mods[l], norm_g[l], _prep_w_in(w_in[l]))
        q, k, vt = _upproj(z, cos2, sin2, qa_g[l], kva_g[l], _prep_w_uq(w_uq[l]), _prep_w_ukv(w_ukv[l]))
        b_s_exp = jnp.repeat(b_s[l].T, HEAD_DIM, axis=1)
        ma = _gmlp(z, sgu_g[l], w_s[l].astype(BF16), b_s_exp, LAT_TILES if last else ROW_TILES)
        mb = _mla(q, k, vt, z, with_ctx_queries=not last)
        mc = _natten(z, _na_bias_tables(rpb[l]), with_ctx_queries=not last)
        xf = _outproj(ma, mb, mc, w_out[l].astype(BF16), xf, mods[l], final_g, final=last)
    return xf.reshape(BATCH, SEQ, D_MODEL)
```

```python
import functools
import math

import numpy as np
import jax
import jax.numpy as jnp
from jax import lax
from jax.experimental import pallas as pl
from jax.experimental.pallas import tpu as pltpu

D_MODEL = 2048
BATCH = 2
SEQ = 4096
DEPTH = 2
GRID_W = 64
CTX_LEN = 256
EPS = 1e-6
NEG = -1e30
HEAD_DIM = 128
W_A = D_MODEL // 4
W_B = D_MODEL // 2
W_C = D_MODEL // 4
CHUNK = 128
A_GROUPS = W_A // HEAD_DIM
B_HEADS = W_B // HEAD_DIM
Q_LORA = D_MODEL // 4
KV_LORA = 512
NOPE_DIM = 128
ROPE_DIM = 64
V_DIM = 128
MLA_SCALE = (NOPE_DIM + ROPE_DIM) ** -0.5
ROPE_THETA = 10000.0
C_HEADS = W_C // HEAD_DIM
MAX_KH = 8
KW = 16
C_SCALE = HEAD_DIM ** -0.5
ROWS = SEQ // GRID_W

LANES = 128
MXU_DIM = 256
VMEM_BYTES_V7X = 64 * 1024 * 1024
VMEM_LIMIT_CAP = 56 * 1024 * 1024

F32 = jnp.float32
BF16 = jnp.bfloat16

N_LAT = BATCH * SEQ
N_CTX = BATCH * CTX_LEN
N_TOK = N_LAT + N_CTX
ROW_TILE = N_CTX
LAT_TILES = N_LAT // ROW_TILE
ROW_TILES = N_TOK // ROW_TILE
TILES_PER_SAMPLE = SEQ // ROW_TILE
Q_TILE = CTX_LEN
Q_TILES_PER_SAMPLE = SEQ // Q_TILE
KV_CHUNK = 256
assert ROW_TILE % CHUNK == 0 and SEQ % ROW_TILE == 0 and SEQ % Q_TILE == 0

OFF_GB = 0
OFF_U = OFF_GB + W_B
OFF_VA = OFF_U + W_A
OFF_GA = OFF_VA + W_A
OFF_CQ = OFF_GA + W_A
OFF_CKV = OFF_CQ + Q_LORA
OFF_QC = OFF_CKV + KV_LORA
OFF_KC = OFF_QC + W_C
OFF_VC = OFF_KC + W_C
OFF_GC = OFF_VC + W_C
OFF_KR = OFF_GC + W_C
Z_W = OFF_KR + 4 * ROPE_DIM
Z_CHUNK = 512
assert Z_W % MXU_DIM == 0 and OFF_KR % (4 * ROPE_DIM) == 0

_SRC = {}
_acc = 0
for _name, _w in (("u", W_A), ("va", W_A), ("ga", W_A), ("cq", Q_LORA), ("ckv", KV_LORA), ("kr", ROPE_DIM),
                  ("gb", W_B), ("qc", W_C), ("kc", W_C), ("vc", W_C), ("gc", W_C)):
    _SRC[_name] = (_acc, _acc + _w)
    _acc += _w
IN_W = _acc

_Q4 = ROPE_DIM // 4
_SWAP = np.concatenate([np.arange(_Q4, 2 * _Q4), np.arange(0, _Q4),
                        np.arange(3 * _Q4, 4 * _Q4), np.arange(2 * _Q4, 3 * _Q4)])


def _vmem_limit(nbytes):
    return int(min(VMEM_LIMIT_CAP, max(16 * 1024 * 1024, nbytes * 5 // 4)))


def _silu(x):
    return x * jax.nn.sigmoid(x)


def _rms(x, g):
    return x * lax.rsqrt(jnp.mean(x * x, axis=-1, keepdims=True) + EPS) * g


MOD_ROWS = 8
MOD_TN = 768


def _mod_kernel(c_ref, w_ref, b_ref, o_ref):
    a = _silu(c_ref[...])
    o_ref[0] = jnp.dot(a, w_ref[0], preferred_element_type=F32, precision=lax.Precision.HIGHEST) + b_ref[0]


def _modulation(cc, w_ada, b_ada):
    n = 3 * D_MODEL
    est = 2 * (MOD_ROWS * D_MODEL * 4 + D_MODEL * MOD_TN * 4 + 2 * MOD_ROWS * MOD_TN * 4)
    return pl.pallas_call(
        _mod_kernel,
        grid=(DEPTH, n // MOD_TN),
        in_specs=[pl.BlockSpec((MOD_ROWS, D_MODEL), lambda l, j: (0, 0)),
                  pl.BlockSpec((1, D_MODEL, MOD_TN), lambda l, j: (l, 0, j)),
                  pl.BlockSpec((1, 1, MOD_TN), lambda l, j: (l, 0, j))],
        out_specs=pl.BlockSpec((1, MOD_ROWS, MOD_TN), lambda l, j: (l, 0, j)),
        out_shape=jax.ShapeDtypeStruct((DEPTH, MOD_ROWS, n), F32),
        compiler_params=pltpu.CompilerParams(dimension_semantics=("parallel", "parallel"),
                                             vmem_limit_bytes=_vmem_limit(est)),
    )(cc, w_ada, b_ada.reshape(DEPTH, 1, n))


def _mod_row(t):
    return jnp.where(t < LAT_TILES, t // TILES_PER_SAMPLE, BATCH)


def _inproj_kernel(x_ref, mod_ref, g_ref, w_ref, o_ref, h_ref):
    x = x_ref[...]
    shift = mod_ref[0, :, 0:D_MODEL]
    scale = mod_ref[0, :, D_MODEL:2 * D_MODEL]
    h_ref[...] = (_rms(x, g_ref[...]) * (1.0 + scale) + shift).astype(BF16)
    for c0 in range(0, Z_W, Z_CHUNK):
        c1 = min(c0 + Z_CHUNK, Z_W)
        o_ref[:, c0:c1] = jnp.dot(h_ref[...], w_ref[:, c0:c1], preferred_element_type=F32).astype(BF16)


def _inproj(xf, mods_l, norm_g, w_in_p):
    est = (2 * ROW_TILE * D_MODEL * 4 + D_MODEL * Z_W * 2 + 2 * ROW_TILE * Z_W * 2 + ROW_TILE * D_MODEL * 2
           + 2 * ROW_TILE * Z_CHUNK * 4)
    return pl.pallas_call(
        _inproj_kernel,
        grid=(ROW_TILES,),
        in_specs=[pl.BlockSpec((ROW_TILE, D_MODEL), lambda t: (t, 0)),
                  pl.BlockSpec((1, 1, 3 * D_MODEL), lambda t: (_mod_row(t), 0, 0)),
                  pl.BlockSpec((1, D_MODEL), lambda t: (0, 0)),
                  pl.BlockSpec((D_MODEL, Z_W), lambda t: (0, 0), pipeline_mode=pl.Buffered(1))],
        out_specs=pl.BlockSpec((ROW_TILE, Z_W), lambda t: (t, 0)),
        out_shape=jax.ShapeDtypeStruct((N_TOK, Z_W), BF16),
        scratch_shapes=[pltpu.VMEM((ROW_TILE, D_MODEL), BF16)],
        compiler_params=pltpu.CompilerParams(dimension_semantics=("parallel",),
                                             vmem_limit_bytes=_vmem_limit(est)),
    )(xf, mods_l.reshape(MOD_ROWS, 1, 3 * D_MODEL), norm_g.reshape(1, D_MODEL), w_in_p)


QK_W = 2 * HEAD_DIM
UQ_W = B_HEADS * (NOPE_DIM + 2 * ROPE_DIM)
UKV_W = B_HEADS * (NOPE_DIM + V_DIM)
_Q_PRESCALE = MLA_SCALE * math.log2(math.e)
assert B_HEADS % 2 == 0 and 2 * ROPE_DIM == LANES


def _upproj_kernel(cq_ref, ckv_ref, kr_ref, cos_ref, sin_ref, qg_ref, kg_ref, wq_ref, wkv_ref,
                   q_ref, k_ref, vt_ref):
    cos2 = cos_ref[...]
    sin2 = sin_ref[...]
    n_rope = B_HEADS * ROPE_DIM
    cqn = _rms(cq_ref[...].astype(F32), qg_ref[...]).astype(BF16)
    q_all = jnp.dot(cqn, wq_ref[...], preferred_element_type=F32)
    rope0 = B_HEADS * NOPE_DIM
    for h in range(B_HEADS):
        q_ref[h, :, 0:NOPE_DIM] = (q_all[:, h * NOPE_DIM:(h + 1) * NOPE_DIM] * _Q_PRESCALE).astype(BF16)
    for j in range(B_HEADS // 2):
        a = q_all[:, rope0 + j * LANES: rope0 + (j + 1) * LANES]
        a_sw = q_all[:, rope0 + n_rope + j * LANES: rope0 + n_rope + (j + 1) * LANES]
        rot = ((a * cos2 + a_sw * sin2) * _Q_PRESCALE).astype(BF16)
        q_ref[2 * j, :, NOPE_DIM:QK_W] = rot
        q_ref[2 * j + 1, :, NOPE_DIM:QK_W] = rot

    ckvn = _rms(ckv_ref[...].astype(F32), kg_ref[...]).astype(BF16)
    kv_all = jnp.dot(ckvn, wkv_ref[...], preferred_element_type=F32)
    kr = kr_ref[...].astype(F32)
    krot = kr[:, 0:LANES] * cos2 + kr[:, LANES:2 * LANES] * sin2
    lane = lax.broadcasted_iota(jnp.int32, krot.shape, 1)
    k_lo = jnp.where(lane < ROPE_DIM, krot, 0.0).astype(BF16)
    k_hi = jnp.where(lane >= ROPE_DIM, krot, 0.0).astype(BF16)
    v0 = B_HEADS * NOPE_DIM
    for h in range(B_HEADS):
        k_ref[h, :, 0:NOPE_DIM] = kv_all[:, h * NOPE_DIM:(h + 1) * NOPE_DIM].astype(BF16)
        k_ref[h, :, NOPE_DIM:QK_W] = k_lo if h % 2 == 0 else k_hi
        v_h = kv_all[:, v0 + h * V_DIM: v0 + (h + 1) * V_DIM]
        for c in range(ROW_TILE // KV_CHUNK):
            vt_ref[h, c] = v_h[c * KV_CHUNK:(c + 1) * KV_CHUNK, :].T.astype(BF16)


def _upproj(z, cos2, sin2, qa_g, kva_g, w_uq_p, w_ukv_p):
    tm = ROW_TILE
    est = 2 * (2 * tm * Q_LORA * 2 + tm * 4 * ROPE_DIM * 2 + 2 * tm * LANES * 4 + Q_LORA * UQ_W * 2
               + KV_LORA * UKV_W * 2 + B_HEADS * tm * (2 * QK_W + V_DIM) * 2) + 4 * tm * UQ_W * 4
    return pl.pallas_call(
        _upproj_kernel,
        grid=(ROW_TILES,),
        in_specs=[pl.BlockSpec((tm, Q_LORA), lambda t: (t, OFF_CQ // Q_LORA)),
                  pl.BlockSpec((tm, KV_LORA), lambda t: (t, OFF_CKV // KV_LORA)),
                  pl.BlockSpec((tm, 4 * ROPE_DIM), lambda t: (t, OFF_KR // (4 * ROPE_DIM))),
                  pl.BlockSpec((tm, LANES), lambda t: (t, 0)),
                  pl.BlockSpec((tm, LANES), lambda t: (t, 0)),
                  pl.BlockSpec((1, Q_LORA), lambda t: (0, 0)),
                  pl.BlockSpec((1, KV_LORA), lambda t: (0, 0)),
                  pl.BlockSpec((Q_LORA, UQ_W), lambda t: (0, 0)),
                  pl.BlockSpec((KV_LORA, UKV_W), lambda t: (0, 0))],
        out_specs=[pl.BlockSpec((B_HEADS, tm, QK_W), lambda t: (0, t, 0)),
                   pl.BlockSpec((B_HEADS, tm, QK_W), lambda t: (0, t, 0)),
                   pl.BlockSpec((B_HEADS, tm // KV_CHUNK, V_DIM, KV_CHUNK), lambda t: (0, t, 0, 0))],
        out_shape=[jax.ShapeDtypeStruct((B_HEADS, N_TOK, QK_W), BF16),
                   jax.ShapeDtypeStruct((B_HEADS, N_TOK, QK_W), BF16),
                   jax.ShapeDtypeStruct((B_HEADS, N_TOK // KV_CHUNK, V_DIM, KV_CHUNK), BF16)],
        compiler_params=pltpu.CompilerParams(dimension_semantics=("parallel",),
                                             vmem_limit_bytes=_vmem_limit(est)),
    )(z, z, z, cos2, sin2, qa_g.reshape(1, Q_LORA), kva_g.reshape(1, KV_LORA), w_uq_p, w_ukv_p)


def _gmlp_kernel(u_ref, v_ref, g_ref, sg_ref, ws_ref, bs_ref, o_ref):
    v = jax.nn.gelu(v_ref[...].astype(F32))
    vn = _rms(v, sg_ref[...]).astype(BF16)
    front = jax.nn.gelu(u_ref[...].astype(F32))
    gate = _silu(g_ref[...].astype(F32))
    for c in range(ROW_TILE // CHUNK):
        r0, r1 = c * CHUNK, (c + 1) * CHUNK
        for g in range(A_GROUPS):
            c0, c1 = g * HEAD_DIM, (g + 1) * HEAD_DIM
            s = jnp.dot(ws_ref[g], vn[r0:r1, c0:c1], preferred_element_type=F32) + bs_ref[:, c0:c1]
            o_ref[r0:r1, c0:c1] = (front[r0:r1, c0:c1] * s * gate[r0:r1, c0:c1]).astype(BF16)


def _gmlp(z, sgu_g, w_s_b, b_s_exp, n_tiles):
    tm = ROW_TILE
    est = 2 * (4 * tm * W_A * 2 + A_GROUPS * CHUNK * CHUNK * 2 + CHUNK * W_A * 4) + 5 * tm * W_A * 4
    return pl.pallas_call(
        _gmlp_kernel,
        grid=(n_tiles,),
        in_specs=[pl.BlockSpec((tm, W_A), lambda t: (t, OFF_U // W_A)),
                  pl.BlockSpec((tm, W_A), lambda t: (t, OFF_VA // W_A)),
                  pl.BlockSpec((tm, W_A), lambda t: (t, OFF_GA // W_A)),
                  pl.BlockSpec((1, W_A), lambda t: (0, 0)),
                  pl.BlockSpec((A_GROUPS, CHUNK, CHUNK), lambda t: (0, 0, 0)),
                  pl.BlockSpec((CHUNK, W_A), lambda t: (0, 0))],
        out_specs=pl.BlockSpec((tm, W_A), lambda t: (t, 0)),
        out_shape=jax.ShapeDtypeStruct((n_tiles * tm, W_A), BF16),
        compiler_params=pltpu.CompilerParams(dimension_semantics=("parallel",),
                                             vmem_limit_bytes=_vmem_limit(est)),
    )(z, z, z, sgu_g.reshape(1, W_A), w_s_b, b_s_exp)


MLA_TK = 512
MLA_TQ = 512
assert SEQ % MLA_TK == 0 and MLA_TK % KV_CHUNK == 0 and SEQ % MLA_TQ == 0


def _q_row_tile(b, i):
    return jnp.where(i < Q_TILES_PER_SAMPLE, b * Q_TILES_PER_SAMPLE + i, N_LAT // Q_TILE + b)


def _attend_t(q, chunks):
    def scores(k):
        return lax.dot_general(k, q, (((1,), (1,)), ((), ())), preferred_element_type=F32)

    m = l = acc = None
    s_next = scores(chunks[0][0])
    for j, (_, vts) in enumerate(chunks):
        s = s_next
        if j + 1 < len(chunks):
            s_next = scores(chunks[j + 1][0])
        s_max = jnp.max(s, axis=0, keepdims=True)
        m_new = s_max if m is None else jnp.maximum(m, s_max)
        p = jnp.exp2(s - m_new)
        p_sum = jnp.sum(p, axis=0, keepdims=True)
        pb = p.astype(BF16)
        pv = jnp.dot(vts[0], pb[0:KV_CHUNK], preferred_element_type=F32)
        for n in range(1, len(vts)):
            pv += jnp.dot(vts[n], pb[n * KV_CHUNK:(n + 1) * KV_CHUNK], preferred_element_type=F32)
        if m is None:
            l, acc = p_sum, pv
        else:
            alpha = jnp.exp2(m - m_new)
            l = alpha * l + p_sum
            acc = alpha * acc + pv
        m = m_new
    return acc / l


def _mla_lat_kernel(q_ref, kl_ref, vtl_ref, kc_ref, vtc_ref, gb_ref, o_ref):
    per = MLA_TK // KV_CHUNK
    chunks = [(kl_ref[0, j * MLA_TK:(j + 1) * MLA_TK, :], [vtl_ref[0, j * per + n] for n in range(per)])
              for j in range(SEQ // MLA_TK)]
    chunks.append((kc_ref[0], [vtc_ref[0, n] for n in range(CTX_LEN // KV_CHUNK)]))
    o = _attend_t(q_ref[0], chunks).T
    o_ref[...] = (o * _silu(gb_ref[...].astype(F32))).astype(BF16)


def _mla_ctx_kernel(q_ref, kc_ref, vtc_ref, gb_ref, prev_ref, o_ref):
    del prev_ref
    o = _attend_t(q_ref[0], [(kc_ref[0], [vtc_ref[0, n] for n in range(CTX_LEN // KV_CHUNK)])]).T
    o_ref[...] = (o * _silu(gb_ref[...].astype(F32))).astype(BF16)


def _mla(q, k, vt, z, with_ctx_queries):
    n_rows = N_TOK if with_ctx_queries else N_LAT
    ctx_tile = N_LAT // CTX_LEN
    tq = MLA_TQ
    nq = SEQ // tq
    est = 2 * (tq * QK_W * 2 + SEQ * QK_W * 2 + SEQ * V_DIM * 2 + CTX_LEN * (QK_W + V_DIM) * 2
               + 2 * tq * HEAD_DIM * 2) + 8 * MLA_TK * tq * 4
    mb = pl.pallas_call(
        _mla_lat_kernel,
        grid=(BATCH, B_HEADS, nq),
        in_specs=[pl.BlockSpec((1, tq, QK_W), lambda b, h, i: (h, b * nq + i, 0)),
                  pl.BlockSpec((1, SEQ, QK_W), lambda b, h, i: (h, b, 0)),
                  pl.BlockSpec((1, SEQ // KV_CHUNK, V_DIM, KV_CHUNK), lambda b, h, i: (h, b, 0, 0)),
                  pl.BlockSpec((1, CTX_LEN, QK_W), lambda b, h, i: (h, ctx_tile + b, 0)),
                  pl.BlockSpec((1, CTX_LEN // KV_CHUNK, V_DIM, KV_CHUNK), lambda b, h, i: (h, ctx_tile + b, 0, 0)),
                  pl.BlockSpec((tq, HEAD_DIM), lambda b, h, i: (b * nq + i, OFF_GB // HEAD_DIM + h))],
        out_specs=pl.BlockSpec((tq, HEAD_DIM), lambda b, h, i: (b * nq + i, h)),
        out_shape=jax.ShapeDtypeStruct((n_rows, W_B), BF16),
        compiler_params=pltpu.CompilerParams(dimension_semantics=("parallel", "parallel", "parallel"),
                                             vmem_limit_bytes=_vmem_limit(est)),
    )(q, k, vt, k, vt, z)
    if not with_ctx_queries:
        return mb
    return pl.pallas_call(
        _mla_ctx_kernel,
        grid=(BATCH, B_HEADS),
        in_specs=[pl.BlockSpec((1, CTX_LEN, QK_W), lambda b, h: (h, ctx_tile + b, 0)),
                  pl.BlockSpec((1, CTX_LEN, QK_W), lambda b, h: (h, ctx_tile + b, 0)),
                  pl.BlockSpec((1, CTX_LEN // KV_CHUNK, V_DIM, KV_CHUNK), lambda b, h: (h, ctx_tile + b, 0, 0)),
                  pl.BlockSpec((CTX_LEN, HEAD_DIM), lambda b, h: (ctx_tile + b, OFF_GB // HEAD_DIM + h)),
                  pl.BlockSpec(memory_space=pl.ANY)],
        out_specs=pl.BlockSpec((CTX_LEN, HEAD_DIM), lambda b, h: (ctx_tile + b, h)),
        out_shape=jax.ShapeDtypeStruct((n_rows, W_B), BF16),
        input_output_aliases={4: 0},
        compiler_params=pltpu.CompilerParams(dimension_semantics=("parallel", "parallel")),
    )(q, k, vt, z, mb)


NA_QROWS = Q_TILE // GRID_W
NA_KROWS = 12
NA_KEYS = NA_KROWS * GRID_W
assert NA_QROWS + MAX_KH <= NA_KROWS + 1 and ROWS >= NA_KROWS


def _na_strip_row(j):
    return int(np.clip(j * NA_QROWS - MAX_KH // 2, 0, ROWS - NA_KROWS))


def _na_tables():
    def one(j):
        s = _na_strip_row(j)
        valid = np.zeros((NA_QROWS, NA_KROWS), bool)
        d = np.zeros((NA_QROWS, NA_KROWS), np.int32)
        for a in range(NA_QROWS):
            r = j * NA_QROWS + a
            r0 = int(np.clip(r - MAX_KH // 2, 0, ROWS - MAX_KH))
            for i in range(NA_KROWS):
                valid[a, i] = 0 <= s + i - r0 < MAX_KH
                d[a, i] = s + i - r + (MAX_KH - 1) if valid[a, i] else 0
        return valid, d
    n_tiles = ROWS // NA_QROWS
    first, mid, last = one(0), one(1), one(n_tiles - 1)
    for j in range(1, n_tiles - 1):
        v, d = one(j)
        assert (v == mid[0]).all() and (d == mid[1]).all()
    return np.stack([first[0], mid[0], last[0]]), np.stack([first[1], mid[1], last[1]])


_NA_VALID, _NA_DROW = _na_tables()


def _na_class(j):
    n_tiles = ROWS // NA_QROWS
    return jnp.where(j == 0, 0, jnp.where(j == n_tiles - 1, 2, 1))


def _na_bias_tables(rpb_l):
    col = np.arange(GRID_W)
    c0 = np.clip(col - KW // 2, 0, GRID_W - KW)
    col_ok = (col[None, :] >= c0[:, None]) & (col[None, :] < c0[:, None] + KW)
    dc = np.clip(col[None, :] - col[:, None], -(KW - 1), KW - 1) + (KW - 1)
    t1 = jnp.take(rpb_l, jnp.asarray(dc.reshape(-1)), axis=2).reshape(C_HEADS, 2 * MAX_KH - 1, GRID_W, GRID_W)
    t1 = jnp.where(jnp.asarray(col_ok)[None, None], t1, NEG)
    t2 = jnp.take(t1, jnp.asarray(_NA_DROW.reshape(-1)), axis=1)
    t2 = t2.reshape(C_HEADS, 3, NA_QROWS, NA_KROWS, GRID_W, GRID_W)
    t2 = jnp.where(jnp.asarray(_NA_VALID)[None, :, :, :, None, None], t2, NEG)
    return t2.transpose(1, 0, 2, 4, 3, 5).reshape(3, C_HEADS, Q_TILE, NA_KEYS)


def _softmax_rows(blocks):
    m = functools.reduce(jnp.maximum, [jnp.max(s, axis=-1, keepdims=True) for s in blocks])
    ps = [jnp.exp(s - m) for s in blocks]
    return ps, functools.reduce(jnp.add, [jnp.sum(p, axis=-1, keepdims=True) for p in ps])


def _natten_kernel(q_ref, kl_ref, vl_ref, kc_ref, vc_ref, bias_ref, g_ref, o_ref):
    nt = (((1,), (1,)), ((), ()))
    n_tiles = ROWS // NA_QROWS

    def tile(j, carry):
        rows = pl.ds(pl.multiple_of(j * Q_TILE, Q_TILE), Q_TILE)
        strip_row = jnp.clip(j * NA_QROWS - MAX_KH // 2, 0, ROWS - NA_KROWS)
        strip = pl.ds(pl.multiple_of(strip_row * GRID_W, GRID_W), NA_KEYS)
        cls = jnp.where(j == 0, 0, jnp.where(j == n_tiles - 1, 2, 1))
        q = q_ref[rows, :]
        s_nb = lax.dot_general(q, kl_ref[strip, :], nt, preferred_element_type=F32) * C_SCALE + bias_ref[cls, 0]
        s_cx = lax.dot_general(q, kc_ref[...], nt, preferred_element_type=F32) * C_SCALE
        (p_nb, p_cx), l = _softmax_rows([s_nb, s_cx])
        o = jnp.dot(p_nb.astype(BF16), vl_ref[strip, :], preferred_element_type=F32)
        o += jnp.dot(p_cx.astype(BF16), vc_ref[...], preferred_element_type=F32)
        o_ref[rows, :] = (o / l * _silu(g_ref[rows, :].astype(F32))).astype(BF16)
        return carry

    lax.fori_loop(0, n_tiles, tile, 0, unroll=2)


def _natten_ctx_kernel(q_ref, kc_ref, vc_ref, g_ref, prev_ref, o_ref):
    del prev_ref
    s = lax.dot_general(q_ref[...], kc_ref[...], (((1,), (1,)), ((), ())), preferred_element_type=F32) * C_SCALE
    (p,), l = _softmax_rows([s])
    o = jnp.dot(p.astype(BF16), vc_ref[...], preferred_element_type=F32)
    o_ref[...] = (o / l * _silu(g_ref[...].astype(F32))).astype(BF16)


def _natten(z, bias_tab, with_ctx_queries):
    n_rows = N_TOK if with_ctx_queries else N_LAT
    ctx_tile = N_LAT // CTX_LEN
    hd = HEAD_DIM
    est = (2 * (5 * SEQ * hd * 2 + 2 * CTX_LEN * hd * 2 + 3 * Q_TILE * NA_KEYS * 4)
           + 12 * Q_TILE * (NA_KEYS + CTX_LEN) * 4)
    mc = pl.pallas_call(
        _natten_kernel,
        grid=(BATCH, C_HEADS),
        in_specs=[pl.BlockSpec((SEQ, hd), lambda b, h: (b, OFF_QC // hd + h)),
                  pl.BlockSpec((SEQ, hd), lambda b, h: (b, OFF_KC // hd + h)),
                  pl.BlockSpec((SEQ, hd), lambda b, h: (b, OFF_VC // hd + h)),
                  pl.BlockSpec((CTX_LEN, hd), lambda b, h: (ctx_tile + b, OFF_KC // hd + h)),
                  pl.BlockSpec((CTX_LEN, hd), lambda b, h: (ctx_tile + b, OFF_VC // hd + h)),
                  pl.BlockSpec((3, 1, Q_TILE, NA_KEYS), lambda b, h: (0, h, 0, 0)),
                  pl.BlockSpec((SEQ, hd), lambda b, h: (b, OFF_GC // hd + h))],
        out_specs=pl.BlockSpec((SEQ, hd), lambda b, h: (b, h)),
        out_shape=jax.ShapeDtypeStruct((n_rows, W_C), BF16),
        compiler_params=pltpu.CompilerParams(dimension_semantics=("parallel", "parallel"),
                                             vmem_limit_bytes=_vmem_limit(est)),
    )(z, z, z, z, z, bias_tab, z)
    if not with_ctx_queries:
        return mc
    return pl.pallas_call(
        _natten_ctx_kernel,
        grid=(BATCH, C_HEADS),
        in_specs=[pl.BlockSpec((CTX_LEN, hd), lambda b, h: (ctx_tile + b, OFF_QC // hd + h)),
                  pl.BlockSpec((CTX_LEN, hd), lambda b, h: (ctx_tile + b, OFF_KC // hd + h)),
                  pl.BlockSpec((CTX_LEN, hd), lambda b, h: (ctx_tile + b, OFF_VC // hd + h)),
                  pl.BlockSpec((CTX_LEN, hd), lambda b, h: (ctx_tile + b, OFF_GC // hd + h)),
                  pl.BlockSpec(memory_space=pl.ANY)],
        out_specs=pl.BlockSpec((CTX_LEN, hd), lambda b, h: (ctx_tile + b, h)),
        out_shape=jax.ShapeDtypeStruct((n_rows, W_C), BF16),
        input_output_aliases={4: 0},
        compiler_params=pltpu.CompilerParams(dimension_semantics=("parallel", "parallel")),
    )(z, z, z, z, mc)


OUT_CHUNK = 512


def _outproj_kernel(ma_ref, mb_ref, mc_ref, w_ref, x_ref, mod_ref, fg_ref, o_ref, *, final):
    gate = mod_ref[0, :, 2 * D_MODEL:3 * D_MODEL]
    for c0 in range(0, D_MODEL, OUT_CHUNK):
        c1 = c0 + OUT_CHUNK
        y = jnp.dot(ma_ref[...], w_ref[0:W_A, c0:c1], preferred_element_type=F32)
        y += jnp.dot(mb_ref[...], w_ref[W_A:W_A + W_B, c0:c1], preferred_element_type=F32)
        y += jnp.dot(mc_ref[...], w_ref[W_A + W_B:D_MODEL, c0:c1], preferred_element_type=F32)
        o_ref[:, c0:c1] = x_ref[:, c0:c1] + gate[:, c0:c1] * y
    if final:
        o_ref[...] = _rms(o_ref[...], fg_ref[...])


def _outproj(ma, mb, mc, w_out_b, xf, mods_l, final_g, final):
    tm = ROW_TILE
    n_tiles = LAT_TILES if final else ROW_TILES
    est = (2 * tm * D_MODEL * 2 + D_MODEL * D_MODEL * 2 + 4 * tm * D_MODEL * 4 + 3 * tm * OUT_CHUNK * 4)
    return pl.pallas_call(
        functools.partial(_outproj_kernel, final=final),
        grid=(n_tiles,),
        in_specs=[pl.BlockSpec((tm, W_A), lambda t: (t, 0)),
                  pl.BlockSpec((tm, W_B), lambda t: (t, 0)),
                  pl.BlockSpec((tm, W_C), lambda t: (t, 0)),
                  pl.BlockSpec((D_MODEL, D_MODEL), lambda t: (0, 0), pipeline_mode=pl.Buffered(1)),
                  pl.BlockSpec((tm, D_MODEL), lambda t: (t, 0)),
                  pl.BlockSpec((1, 1, 3 * D_MODEL), lambda t: (_mod_row(t), 0, 0)),
                  pl.BlockSpec((1, D_MODEL), lambda t: (0, 0))],
        out_specs=pl.BlockSpec((tm, D_MODEL), lambda t: (t, 0)),
        out_shape=jax.ShapeDtypeStruct((n_tiles * tm, D_MODEL), F32),
        compiler_params=pltpu.CompilerParams(dimension_semantics=("parallel",),
                                             vmem_limit_bytes=_vmem_limit(est)),
    )(ma, mb, mc, w_out_b, xf, mods_l.reshape(MOD_ROWS, 1, 3 * D_MODEL), final_g.reshape(1, D_MODEL))


def _cols(w, name):
    a, b = _SRC[name]
    return w[:, a:b]


def _prep_w_in(w):
    kr = _cols(w, "kr")
    kr_sw = jnp.concatenate([kr[:, _Q4:2 * _Q4], kr[:, 0:_Q4], kr[:, 3 * _Q4:4 * _Q4], kr[:, 2 * _Q4:3 * _Q4]], axis=1)
    parts = [_cols(w, n) for n in ("gb", "u", "va", "ga", "cq", "ckv", "qc", "kc", "vc", "gc")]
    return jnp.concatenate(parts + [kr, kr, kr_sw, kr_sw], axis=1).astype(BF16)


def _prep_w_uq(w):
    w3 = w.reshape(Q_LORA, B_HEADS, NOPE_DIM + ROPE_DIM)
    nope = w3[:, :, :NOPE_DIM].reshape(Q_LORA, B_HEADS * NOPE_DIM)
    rope = w3[:, :, NOPE_DIM:]
    rope_sw = jnp.concatenate([rope[..., _Q4:2 * _Q4], rope[..., 0:_Q4],
                               rope[..., 3 * _Q4:4 * _Q4], rope[..., 2 * _Q4:3 * _Q4]], axis=-1)
    return jnp.concatenate([nope, rope.reshape(Q_LORA, -1), rope_sw.reshape(Q_LORA, -1)], axis=1).astype(BF16)


def _prep_w_ukv(w):
    w3 = w.reshape(KV_LORA, B_HEADS, NOPE_DIM + V_DIM)
    return jnp.concatenate([w3[:, :, :NOPE_DIM].reshape(KV_LORA, -1),
                            w3[:, :, NOPE_DIM:].reshape(KV_LORA, -1)], axis=1).astype(BF16)


def _rope_tables():
    t = jnp.arange(SEQ)
    row = (t // GRID_W).astype(F32)
    col = (t % GRID_W).astype(F32)
    inv = ROPE_THETA ** (-jnp.arange(_Q4, dtype=F32) / _Q4)
    ar, ac = row[:, None] * inv, col[:, None] * inv
    cos = jnp.concatenate([jnp.cos(ar), jnp.cos(ar), jnp.cos(ac), jnp.cos(ac)], axis=1)
    sin = jnp.concatenate([-jnp.sin(ar), jnp.sin(ar), -jnp.sin(ac), jnp.sin(ac)], axis=1)
    cos = jnp.concatenate([jnp.tile(cos, (BATCH, 1)), jnp.ones((N_CTX, ROPE_DIM), F32)], axis=0)
    sin = jnp.concatenate([jnp.tile(sin, (BATCH, 1)), jnp.zeros((N_CTX, ROPE_DIM), F32)], axis=0)
    return jnp.concatenate([cos, cos], axis=1), jnp.concatenate([sin, sin], axis=1)


def kernel(x, c, ctx, c_ctx, w_ada, b_ada, norm_g, w_in, qa_g, kva_g, w_uq, w_ukv, sgu_g, w_s, b_s, rpb,
           w_out, final_g):
    assert x.shape == (BATCH, SEQ, D_MODEL) and ctx.shape == (BATCH, CTX_LEN, D_MODEL)
    assert w_in.shape == (DEPTH, D_MODEL, IN_W)
    xf = jnp.concatenate([x.reshape(N_LAT, D_MODEL), ctx.reshape(N_CTX, D_MODEL)], axis=0)
    cc = jnp.zeros((MOD_ROWS, D_MODEL), F32).at[0:BATCH].set(c).at[BATCH].set(c_ctx)
    mods = _modulation(cc, w_ada, b_ada)
    cos2, sin2 = _rope_tables()

    for l in range(DEPTH):
        last = l == DEPTH - 1
        z = _inproj(xf, mods[l], norm_g[l], _prep_w_in(w_in[l]))
        q, k, vt = _upproj(z, cos2, sin2, qa_g[l], kva_g[l], _prep_w_uq(w_uq[l]), _prep_w_ukv(w_ukv[l]))
        b_s_exp = jnp.repeat(b_s[l].T, HEAD_DIM, axis=1)
        ma = _gmlp(z, sgu_g[l], w_s[l].astype(BF16), b_s_exp, LAT_TILES if last else ROW_TILES)
        mb = _mla(q, k, vt, z, with_ctx_queries=not last)
        mc = _natten(z, _na_bias_tables(rpb[l]), with_ctx_queries=not last)
        xf = _outproj(ma, mb, mc, w_out[l].astype(BF16), xf, mods[l], final_g, final=last)
    return xf.reshape(BATCH, SEQ, D_MODEL)
```

```python
import functools
import math

import numpy as np
import jax
import jax.numpy as jnp
from jax import lax
from jax.experimental import pallas as pl
from jax.experimental.pallas import tpu as pltpu

D_MODEL = 2048
BATCH = 2
SEQ = 4096
DEPTH = 2
GRID_W = 64
CTX_LEN = 256
EPS = 1e-6
NEG = -1e30
HEAD_DIM = 128
W_A = D_MODEL // 4
W_B = D_MODEL // 2
W_C = D_MODEL // 4
CHUNK = 128
A_GROUPS = W_A // HEAD_DIM
B_HEADS = W_B // HEAD_DIM
Q_LORA = D_MODEL // 4
KV_LORA = 512
NOPE_DIM = 128
ROPE_DIM = 64
V_DIM = 128
MLA_SCALE = (NOPE_DIM + ROPE_DIM) ** -0.5
ROPE_THETA = 10000.0
C_HEADS = W_C // HEAD_DIM
MAX_KH = 8
KW = 16
C_SCALE = HEAD_DIM ** -0.5
ROWS = SEQ // GRID_W

LANES = 128
MXU_DIM = 256
VMEM_BYTES_V7X = 64 * 1024 * 1024
VMEM_LIMIT_CAP = 56 * 1024 * 1024

F32 = jnp.float32
BF16 = jnp.bfloat16

N_LAT = BATCH * SEQ
N_CTX = BATCH * CTX_LEN
N_TOK = N_LAT + N_CTX
ROW_TILE = N_CTX
LAT_TILES = N_LAT // ROW_TILE
ROW_TILES = N_TOK // ROW_TILE
TILES_PER_SAMPLE = SEQ // ROW_TILE
Q_TILE = CTX_LEN
Q_TILES_PER_SAMPLE = SEQ // Q_TILE
KV_CHUNK = 256
assert ROW_TILE % CHUNK == 0 and SEQ % ROW_TILE == 0 and SEQ % Q_TILE == 0

OFF_GB = 0
OFF_U = OFF_GB + W_B
OFF_VA = OFF_U + W_A
OFF_GA = OFF_VA + W_A
OFF_CQ = OFF_GA + W_A
OFF_CKV = OFF_CQ + Q_LORA
OFF_QC = OFF_CKV + KV_LORA
OFF_KC = OFF_QC + W_C
OFF_VC = OFF_KC + W_C
OFF_GC = OFF_VC + W_C
OFF_KR = OFF_GC + W_C
Z_W = OFF_KR + 4 * ROPE_DIM
Z_CHUNK = 512
assert Z_W % MXU_DIM == 0 and OFF_KR % (4 * ROPE_DIM) == 0

_SRC = {}
_acc = 0
for _name, _w in (("u", W_A), ("va", W_A), ("ga", W_A), ("cq", Q_LORA), ("ckv", KV_LORA), ("kr", ROPE_DIM),
                  ("gb", W_B), ("qc", W_C), ("kc", W_C), ("vc", W_C), ("gc", W_C)):
    _SRC[_name] = (_acc, _acc + _w)
    _acc += _w
IN_W = _acc

_Q4 = ROPE_DIM // 4
_SWAP = np.concatenate([np.arange(_Q4, 2 * _Q4), np.arange(0, _Q4),
                        np.arange(3 * _Q4, 4 * _Q4), np.arange(2 * _Q4, 3 * _Q4)])


def _vmem_limit(nbytes):
    return int(min(VMEM_LIMIT_CAP, max(16 * 1024 * 1024, nbytes * 5 // 4)))


def _silu(x):
    return x * jax.nn.sigmoid(x)


def _rms(x, g):
    return x * lax.rsqrt(jnp.mean(x * x, axis=-1, keepdims=True) + EPS) * g


MOD_ROWS = 8
MOD_TN = 768


def _mod_kernel(c_ref, w_ref, b_ref, o_ref):
    a = _silu(c_ref[...])
    o_ref[0] = jnp.dot(a, w_ref[0], preferred_element_type=F32, precision=lax.Precision.HIGHEST) + b_ref[0]


def _modulation(cc, w_ada, b_ada):
    n = 3 * D_MODEL
    est = 2 * (MOD_ROWS * D_MODEL * 4 + D_MODEL * MOD_TN * 4 + 2 * MOD_ROWS * MOD_TN * 4)
    return pl.pallas_call(
        _mod_kernel,
        grid=(DEPTH, n // MOD_TN),
        in_specs=[pl.BlockSpec((MOD_ROWS, D_MODEL), lambda l, j: (0, 0)),
                  pl.BlockSpec((1, D_MODEL, MOD_TN), lambda l, j: (l, 0, j)),
                  pl.BlockSpec((1, 1, MOD_TN), lambda l, j: (l, 0, j))],
        out_specs=pl.BlockSpec((1, MOD_ROWS, MOD_TN), lambda l, j: (l, 0, j)),
        out_shape=jax.ShapeDtypeStruct((DEPTH, MOD_ROWS, n), F32),
        compiler_params=pltpu.CompilerParams(dimension_semantics=("parallel", "parallel"),
                                             vmem_limit_bytes=_vmem_limit(est)),
    )(cc, w_ada, b_ada.reshape(DEPTH, 1, n))


def _mod_row(t):
    return jnp.where(t < LAT_TILES, t // TILES_PER_SAMPLE, BATCH)


def _lat_tile(t):
    return jnp.minimum(t, LAT_TILES - 1)


def _pick_rows(lat_ref, ctx_ref):
    return jnp.where(pl.program_id(0) < LAT_TILES, lat_ref[...], ctx_ref[...])


def _inproj_kernel(xl_ref, xc_ref, mod_ref, g_ref, w_ref, o_ref, h_ref):
    x = _pick_rows(xl_ref, xc_ref)
    shift = mod_ref[0, :, 0:D_MODEL]
    scale = mod_ref[0, :, D_MODEL:2 * D_MODEL]
    h_ref[...] = (_rms(x, g_ref[...]) * (1.0 + scale) + shift).astype(BF16)
    for c0 in range(0, Z_W, Z_CHUNK):
        c1 = min(c0 + Z_CHUNK, Z_W)
        o_ref[:, c0:c1] = jnp.dot(h_ref[...], w_ref[:, c0:c1], preferred_element_type=F32).astype(BF16)


def _inproj(x_lat, x_ctx, ctx_block, mods_l, norm_g, w_in_p):
    est = (3 * ROW_TILE * D_MODEL * 4 + D_MODEL * Z_W * 2 + 2 * ROW_TILE * Z_W * 2 + ROW_TILE * D_MODEL * 2
           + 2 * ROW_TILE * Z_CHUNK * 4)
    return pl.pallas_call(
        _inproj_kernel,
        grid=(ROW_TILES,),
        in_specs=[pl.BlockSpec((ROW_TILE, D_MODEL), lambda t: (_lat_tile(t), 0)),
                  pl.BlockSpec((ROW_TILE, D_MODEL), lambda t: (ctx_block, 0), pipeline_mode=pl.Buffered(1)),
                  pl.BlockSpec((1, 1, 3 * D_MODEL), lambda t: (_mod_row(t), 0, 0)),
                  pl.BlockSpec((1, D_MODEL), lambda t: (0, 0)),
                  pl.BlockSpec((D_MODEL, Z_W), lambda t: (0, 0), pipeline_mode=pl.Buffered(1))],
        out_specs=pl.BlockSpec((ROW_TILE, Z_W), lambda t: (t, 0)),
        out_shape=jax.ShapeDtypeStruct((N_TOK, Z_W), BF16),
        scratch_shapes=[pltpu.VMEM((ROW_TILE, D_MODEL), BF16)],
        compiler_params=pltpu.CompilerParams(dimension_semantics=("parallel",),
                                             vmem_limit_bytes=_vmem_limit(est)),
    )(x_lat, x_ctx, mods_l.reshape(MOD_ROWS, 1, 3 * D_MODEL), norm_g.reshape(1, D_MODEL), w_in_p)


QK_W = 2 * HEAD_DIM
UQ_W = B_HEADS * (NOPE_DIM + 2 * ROPE_DIM)
UKV_W = B_HEADS * (NOPE_DIM + V_DIM)
_Q_PRESCALE = MLA_SCALE * math.log2(math.e)
assert B_HEADS % 2 == 0 and 2 * ROPE_DIM == LANES


def _upproj_kernel(cq_ref, ckv_ref, kr_ref, cos_ref, sin_ref, qg_ref, kg_ref, wq_ref, wkv_ref,
                   q_ref, k_ref, vt_ref):
    cos2 = cos_ref[...]
    sin2 = sin_ref[...]
    n_rope = B_HEADS * ROPE_DIM
    cqn = _rms(cq_ref[...].astype(F32), qg_ref[...]).astype(BF16)
    q_all = jnp.dot(cqn, wq_ref[...], preferred_element_type=F32)
    rope0 = B_HEADS * NOPE_DIM
    for h in range(B_HEADS):
        q_ref[h, :, 0:NOPE_DIM] = (q_all[:, h * NOPE_DIM:(h + 1) * NOPE_DIM] * _Q_PRESCALE).astype(BF16)
    for j in range(B_HEADS // 2):
        a = q_all[:, rope0 + j * LANES: rope0 + (j + 1) * LANES]
        a_sw = q_all[:, rope0 + n_rope + j * LANES: rope0 + n_rope + (j + 1) * LANES]
        rot = ((a * cos2 + a_sw * sin2) * _Q_PRESCALE).astype(BF16)
        q_ref[2 * j, :, NOPE_DIM:QK_W] = rot
        q_ref[2 * j + 1, :, NOPE_DIM:QK_W] = rot

    ckvn = _rms(ckv_ref[...].astype(F32), kg_ref[...]).astype(BF16)
    kv_all = jnp.dot(ckvn, wkv_ref[...], preferred_element_type=F32)
    kr = kr_ref[...].astype(F32)
    krot = kr[:, 0:LANES] * cos2 + kr[:, LANES:2 * LANES] * sin2
    lane = lax.broadcasted_iota(jnp.int32, krot.shape, 1)
    k_lo = jnp.where(lane < ROPE_DIM, krot, 0.0).astype(BF16)
    k_hi = jnp.where(lane >= ROPE_DIM, krot, 0.0).astype(BF16)
    v0 = B_HEADS * NOPE_DIM
    for h in range(B_HEADS):
        k_ref[h, :, 0:NOPE_DIM] = kv_all[:, h * NOPE_DIM:(h + 1) * NOPE_DIM].astype(BF16)
        k_ref[h, :, NOPE_DIM:QK_W] = k_lo if h % 2 == 0 else k_hi
        v_h = kv_all[:, v0 + h * V_DIM: v0 + (h + 1) * V_DIM]
        for c in range(ROW_TILE // KV_CHUNK):
            vt_ref[h, c] = v_h[c * KV_CHUNK:(c + 1) * KV_CHUNK, :].T.astype(BF16)


def _upproj(z, cos2, sin2, qa_g, kva_g, w_uq_p, w_ukv_p):
    tm = ROW_TILE
    est = 2 * (2 * tm * Q_LORA * 2 + tm * 4 * ROPE_DIM * 2 + 2 * tm * LANES * 4 + Q_LORA * UQ_W * 2
               + KV_LORA * UKV_W * 2 + B_HEADS * tm * (2 * QK_W + V_DIM) * 2) + 4 * tm * UQ_W * 4
    return pl.pallas_call(
        _upproj_kernel,
        grid=(ROW_TILES,),
        in_specs=[pl.BlockSpec((tm, Q_LORA), lambda t: (t, OFF_CQ // Q_LORA)),
                  pl.BlockSpec((tm, KV_LORA), lambda t: (t, OFF_CKV // KV_LORA)),
                  pl.BlockSpec((tm, 4 * ROPE_DIM), lambda t: (t, OFF_KR // (4 * ROPE_DIM))),
                  pl.BlockSpec((tm, LANES), lambda t: (_rope_tile(t), 0)),
                  pl.BlockSpec((tm, LANES), lambda t: (_rope_tile(t), 0)),
                  pl.BlockSpec((1, Q_LORA), lambda t: (0, 0)),
                  pl.BlockSpec((1, KV_LORA), lambda t: (0, 0)),
                  pl.BlockSpec((Q_LORA, UQ_W), lambda t: (0, 0)),
                  pl.BlockSpec((KV_LORA, UKV_W), lambda t: (0, 0))],
        out_specs=[pl.BlockSpec((B_HEADS, tm, QK_W), lambda t: (0, t, 0)),
                   pl.BlockSpec((B_HEADS, tm, QK_W), lambda t: (0, t, 0)),
                   pl.BlockSpec((B_HEADS, tm // KV_CHUNK, V_DIM, KV_CHUNK), lambda t: (0, t, 0, 0))],
        out_shape=[jax.ShapeDtypeStruct((B_HEADS, N_TOK, QK_W), BF16),
                   jax.ShapeDtypeStruct((B_HEADS, N_TOK, QK_W), BF16),
                   jax.ShapeDtypeStruct((B_HEADS, N_TOK // KV_CHUNK, V_DIM, KV_CHUNK), BF16)],
        compiler_params=pltpu.CompilerParams(dimension_semantics=("parallel",),
                                             vmem_limit_bytes=_vmem_limit(est)),
    )(z, z, z, cos2, sin2, qa_g.reshape(1, Q_LORA), kva_g.reshape(1, KV_LORA), w_uq_p, w_ukv_p)


def _gmlp_kernel(u_ref, v_ref, g_ref, sg_ref, ws_ref, bs_ref, o_ref):
    v = jax.nn.gelu(v_ref[...].astype(F32))
    vn = _rms(v, sg_ref[...]).astype(BF16)
    front = jax.nn.gelu(u_ref[...].astype(F32))
    gate = _silu(g_ref[...].astype(F32))
    for c in range(ROW_TILE // CHUNK):
        r0, r1 = c * CHUNK, (c + 1) * CHUNK
        for g in range(A_GROUPS):
            c0, c1 = g * HEAD_DIM, (g + 1) * HEAD_DIM
            s = jnp.dot(ws_ref[g], vn[r0:r1, c0:c1], preferred_element_type=F32) + bs_ref[:, c0:c1]
            o_ref[r0:r1, c0:c1] = (front[r0:r1, c0:c1] * s * gate[r0:r1, c0:c1]).astype(BF16)


def _gmlp(z, sgu_g, w_s_b, b_s_exp, n_tiles):
    tm = ROW_TILE
    est = 2 * (4 * tm * W_A * 2 + A_GROUPS * CHUNK * CHUNK * 2 + CHUNK * W_A * 4) + 5 * tm * W_A * 4
    return pl.pallas_call(
        _gmlp_kernel,
        grid=(n_tiles,),
        in_specs=[pl.BlockSpec((tm, W_A), lambda t: (t, OFF_U // W_A)),
                  pl.BlockSpec((tm, W_A), lambda t: (t, OFF_VA // W_A)),
                  pl.BlockSpec((tm, W_A), lambda t: (t, OFF_GA // W_A)),
                  pl.BlockSpec((1, W_A), lambda t: (0, 0)),
                  pl.BlockSpec((A_GROUPS, CHUNK, CHUNK), lambda t: (0, 0, 0)),
                  pl.BlockSpec((CHUNK, W_A), lambda t: (0, 0))],
        out_specs=pl.BlockSpec((tm, W_A), lambda t: (t, 0)),
        out_shape=jax.ShapeDtypeStruct((n_tiles * tm, W_A), BF16),
        compiler_params=pltpu.CompilerParams(dimension_semantics=("parallel",),
                                             vmem_limit_bytes=_vmem_limit(est)),
    )(z, z, z, sgu_g.reshape(1, W_A), w_s_b, b_s_exp)


MLA_TK = 512
MLA_TQ = 512
assert SEQ % MLA_TK == 0 and MLA_TK % KV_CHUNK == 0 and SEQ % MLA_TQ == 0


def _attend_t(q, chunks):
    def scores(k):
        return lax.dot_general(k, q, (((1,), (1,)), ((), ())), preferred_element_type=F32)

    m = l = acc = None
    s_next = scores(chunks[0][0])
    for j, (_, vts) in enumerate(chunks):
        s = s_next
        if j + 1 < len(chunks):
            s_next = scores(chunks[j + 1][0])
        s_max = jnp.max(s, axis=0, keepdims=True)
        m_new = s_max if m is None else jnp.maximum(m, s_max)
        p_sum = pv = None
        for n, vt in enumerate(vts):
            p = jnp.exp2(s[n * KV_CHUNK:(n + 1) * KV_CHUNK] - m_new)
            ps = jnp.sum(p, axis=0, keepdims=True)
            d = jnp.dot(vt, p.astype(BF16), preferred_element_type=F32)
            p_sum, pv = (ps, d) if n == 0 else (p_sum + ps, pv + d)
        if m is None:
            l, acc = p_sum, pv
        else:
            alpha = jnp.exp2(m - m_new)
            l = alpha * l + p_sum
            acc = alpha * acc + pv
        m = m_new
    return acc / l


def _mla_lat_kernel(q_ref, kl_ref, vtl_ref, kc_ref, vtc_ref, gb_ref, o_ref):
    per = MLA_TK // KV_CHUNK
    chunks = [(kl_ref[0, j * MLA_TK:(j + 1) * MLA_TK, :], [vtl_ref[0, j * per + n] for n in range(per)])
              for j in range(SEQ // MLA_TK)]
    chunks.append((kc_ref[0], [vtc_ref[0, n] for n in range(CTX_LEN // KV_CHUNK)]))
    o = _attend_t(q_ref[0], chunks).T
    o_ref[...] = (o * _silu(gb_ref[...].astype(F32))).astype(BF16)


def _mla_ctx_kernel(q_ref, kc_ref, vtc_ref, gb_ref, o_ref):
    o = _attend_t(q_ref[0], [(kc_ref[0], [vtc_ref[0, n] for n in range(CTX_LEN // KV_CHUNK)])]).T
    o_ref[...] = (o * _silu(gb_ref[...].astype(F32))).astype(BF16)


def _mla(q, k, vt, z, with_ctx_queries):
    ctx_tile = N_LAT // CTX_LEN
    tq = MLA_TQ
    nq = SEQ // tq
    est = 2 * (tq * QK_W * 2 + SEQ * QK_W * 2 + SEQ * V_DIM * 2 + CTX_LEN * (QK_W + V_DIM) * 2
               + 2 * tq * HEAD_DIM * 2) + 8 * MLA_TK * tq * 4
    mb = pl.pallas_call(
        _mla_lat_kernel,
        grid=(BATCH, B_HEADS, nq),
        in_specs=[pl.BlockSpec((1, tq, QK_W), lambda b, h, i: (h, b * nq + i, 0)),
                  pl.BlockSpec((1, SEQ, QK_W), lambda b, h, i: (h, b, 0)),
                  pl.BlockSpec((1, SEQ // KV_CHUNK, V_DIM, KV_CHUNK), lambda b, h, i: (h, b, 0, 0)),
                  pl.BlockSpec((1, CTX_LEN, QK_W), lambda b, h, i: (h, ctx_tile + b, 0)),
                  pl.BlockSpec((1, CTX_LEN // KV_CHUNK, V_DIM, KV_CHUNK), lambda b, h, i: (h, ctx_tile + b, 0, 0)),
                  pl.BlockSpec((tq, HEAD_DIM), lambda b, h, i: (b * nq + i, OFF_GB // HEAD_DIM + h))],
        out_specs=pl.BlockSpec((tq, HEAD_DIM), lambda b, h, i: (b * nq + i, h)),
        out_shape=jax.ShapeDtypeStruct((N_LAT, W_B), BF16),
        compiler_params=pltpu.CompilerParams(dimension_semantics=("parallel", "parallel", "parallel"),
                                             vmem_limit_bytes=_vmem_limit(est)),
    )(q, k, vt, k, vt, z)
    if not with_ctx_queries:
        return mb, None
    mb_ctx = pl.pallas_call(
        _mla_ctx_kernel,
        grid=(BATCH, B_HEADS),
        in_specs=[pl.BlockSpec((1, CTX_LEN, QK_W), lambda b, h: (h, ctx_tile + b, 0)),
                  pl.BlockSpec((1, CTX_LEN, QK_W), lambda b, h: (h, ctx_tile + b, 0)),
                  pl.BlockSpec((1, CTX_LEN // KV_CHUNK, V_DIM, KV_CHUNK), lambda b, h: (h, ctx_tile + b, 0, 0)),
                  pl.BlockSpec((CTX_LEN, HEAD_DIM), lambda b, h: (ctx_tile + b, OFF_GB // HEAD_DIM + h))],
        out_specs=pl.BlockSpec((CTX_LEN, HEAD_DIM), lambda b, h: (b, h)),
        out_shape=jax.ShapeDtypeStruct((N_CTX, W_B), BF16),
        compiler_params=pltpu.CompilerParams(dimension_semantics=("parallel", "parallel")),
    )(q, k, vt, z)
    return mb, mb_ctx


NA_QROWS = Q_TILE // GRID_W
NA_KROWS = 12
NA_KEYS = NA_KROWS * GRID_W
assert NA_QROWS + MAX_KH <= NA_KROWS + 1 and ROWS >= NA_KROWS


def _na_strip_row(j):
    return int(np.clip(j * NA_QROWS - MAX_KH // 2, 0, ROWS - NA_KROWS))


def _na_tables():
    def one(j):
        s = _na_strip_row(j)
        valid = np.zeros((NA_QROWS, NA_KROWS), bool)
        d = np.zeros((NA_QROWS, NA_KROWS), np.int32)
        for a in range(NA_QROWS):
            r = j * NA_QROWS + a
            r0 = int(np.clip(r - MAX_KH // 2, 0, ROWS - MAX_KH))
            for i in range(NA_KROWS):
                valid[a, i] = 0 <= s + i - r0 < MAX_KH
                d[a, i] = s + i - r + (MAX_KH - 1) if valid[a, i] else 0
        return valid, d
    n_tiles = ROWS // NA_QROWS
    first, mid, last = one(0), one(1), one(n_tiles - 1)
    for j in range(1, n_tiles - 1):
        v, d = one(j)
        assert (v == mid[0]).all() and (d == mid[1]).all()
    return np.stack([first[0], mid[0], last[0]]), np.stack([first[1], mid[1], last[1]])


_NA_VALID, _NA_DROW = _na_tables()


def _na_bias_tables(rpb_l):
    col = np.arange(GRID_W)
    c0 = np.clip(col - KW // 2, 0, GRID_W - KW)
    col_ok = (col[None, :] >= c0[:, None]) & (col[None, :] < c0[:, None] + KW)
    dc = np.clip(col[None, :] - col[:, None], -(KW - 1), KW - 1) + (KW - 1)
    t1 = jnp.take(rpb_l, jnp.asarray(dc.reshape(-1)), axis=2).reshape(C_HEADS, 2 * MAX_KH - 1, GRID_W, GRID_W)
    t1 = jnp.where(jnp.asarray(col_ok)[None, None], t1, NEG).transpose(0, 2, 1, 3)
    per_class = []
    for cls in range(3):
        per_a = []
        for a in range(NA_QROWS):
            idx = np.nonzero(_NA_VALID[cls, a])[0]
            i_lo, i_hi = int(idx[0]), int(idx[-1]) + 1
            d_lo = int(_NA_DROW[cls, a, i_lo])
            assert (np.diff(idx) == 1).all() and (_NA_DROW[cls, a, i_lo:i_hi] == d_lo + np.arange(i_hi - i_lo)).all()
            piece = t1[:, :, d_lo:d_lo + (i_hi - i_lo), :]
            per_a.append(jnp.pad(piece, ((0, 0), (0, 0), (i_lo, NA_KROWS - i_hi), (0, 0)), constant_values=NEG))
        per_class.append(jnp.stack(per_a, axis=1))
    return jnp.stack(per_class, axis=0).reshape(3, C_HEADS, Q_TILE, NA_KEYS)


def _softmax_rows(blocks):
    m = functools.reduce(jnp.maximum, [jnp.max(s, axis=-1, keepdims=True) for s in blocks])
    ps = [jnp.exp(s - m) for s in blocks]
    return ps, functools.reduce(jnp.add, [jnp.sum(p, axis=-1, keepdims=True) for p in ps])


def _natten_kernel(q_ref, kl_ref, vl_ref, kc_ref, vc_ref, bias_ref, g_ref, o_ref):
    nt = (((1,), (1,)), ((), ()))
    n_tiles = ROWS // NA_QROWS

    def tile(j, carry):
        rows = pl.ds(pl.multiple_of(j * Q_TILE, Q_TILE), Q_TILE)
        strip_row = jnp.clip(j * NA_QROWS - MAX_KH // 2, 0, ROWS - NA_KROWS)
        strip = pl.ds(pl.multiple_of(strip_row * GRID_W, GRID_W), NA_KEYS)
        cls = jnp.where(j == 0, 0, jnp.where(j == n_tiles - 1, 2, 1))
        q = q_ref[rows, :]
        s_nb = lax.dot_general(q, kl_ref[strip, :], nt, preferred_element_type=F32) * C_SCALE + bias_ref[cls, 0]
        s_cx = lax.dot_general(q, kc_ref[...], nt, preferred_element_type=F32) * C_SCALE
        (p_nb, p_cx), l = _softmax_rows([s_nb, s_cx])
        o = jnp.dot(p_nb.astype(BF16), vl_ref[strip, :], preferred_element_type=F32)
        o += jnp.dot(p_cx.astype(BF16), vc_ref[...], preferred_element_type=F32)
        o_ref[rows, :] = (o / l * _silu(g_ref[rows, :].astype(F32))).astype(BF16)
        return carry

    lax.fori_loop(0, n_tiles, tile, 0, unroll=2)


def _natten_ctx_kernel(q_ref, kc_ref, vc_ref, g_ref, o_ref):
    s =lax.dot_general(q_ref[...], kc_ref[...], (((1,), (1,)), ((), ())), preferred_element_type=F32) * C_SCALE
    (p,), l = _softmax_rows([s])
    o = jnp.dot(p.astype(BF16), vc_ref[...], preferred_element_type=F32)
    o_ref[...] = (o / l * _silu(g_ref[...].astype(F32))).astype(BF16)


def _natten(z, bias_tab, with_ctx_queries):
    ctx_tile = N_LAT // CTX_LEN
    hd = HEAD_DIM
    est = (2 * (5 * SEQ * hd * 2 + 2 * CTX_LEN * hd * 2 + 3 * Q_TILE * NA_KEYS * 4)
           + 12 * Q_TILE * (NA_KEYS + CTX_LEN) * 4)
    mc = pl.pallas_call(
        _natten_kernel,
        grid=(BATCH, C_HEADS),
        in_specs=[pl.BlockSpec((SEQ, hd), lambda b, h: (b, OFF_QC // hd + h)),
                  pl.BlockSpec((SEQ, hd), lambda b, h: (b, OFF_KC // hd + h)),
                  pl.BlockSpec((SEQ, hd), lambda b, h: (b, OFF_VC // hd + h)),
                  pl.BlockSpec((CTX_LEN, hd), lambda b, h: (ctx_tile + b, OFF_KC // hd + h)),
                  pl.BlockSpec((CTX_LEN, hd), lambda b, h: (ctx_tile + b, OFF_VC // hd + h)),
                  pl.BlockSpec((3, 1, Q_TILE, NA_KEYS), lambda b, h: (0, h, 0, 0)),
                  pl.BlockSpec((SEQ, hd), lambda b, h: (b, OFF_GC // hd + h))],
        out_specs=pl.BlockSpec((SEQ, hd), lambda b, h: (b, h)),
        out_shape=jax.ShapeDtypeStruct((N_LAT, W_C), BF16),
        compiler_params=pltpu.CompilerParams(dimension_semantics=("parallel", "parallel"),
                                             vmem_limit_bytes=_vmem_limit(est)),
    )(z, z, z, z, z, bias_tab, z)
    if not with_ctx_queries:
        return mc, None
    mc_ctx = pl.pallas_call(
        _natten_ctx_kernel,
        grid=(BATCH, C_HEADS),
        in_specs=[pl.BlockSpec((CTX_LEN, hd), lambda b, h: (ctx_tile + b, OFF_QC // hd + h)),
                  pl.BlockSpec((CTX_LEN, hd), lambda b, h: (ctx_tile + b, OFF_KC // hd + h)),
                  pl.BlockSpec((CTX_LEN, hd), lambda b, h: (ctx_tile + b, OFF_VC // hd + h)),
                  pl.BlockSpec((CTX_LEN, hd), lambda b, h: (ctx_tile + b, OFF_GC // hd + h))],
        out_specs=pl.BlockSpec((CTX_LEN, hd), lambda b, h: (b, h)),
        out_shape=jax.ShapeDtypeStruct((N_CTX, W_C), BF16),
        compiler_params=pltpu.CompilerParams(dimension_semantics=("parallel", "parallel")),
    )(z, z, z, z)
    return mc, mc_ctx


OUT_CHUNK = 512


def _outproj_body(ma, mb, mc, x, w_ref, mod_ref, o_ref):
    gate = mod_ref[0, :, 2 * D_MODEL:3 * D_MODEL]
    for c0 in range(0, D_MODEL, OUT_CHUNK):
        c1 = c0 + OUT_CHUNK
        y = jnp.dot(ma, w_ref[0:W_A, c0:c1], preferred_element_type=F32)
        y += jnp.dot(mb, w_ref[W_A:W_A + W_B, c0:c1], preferred_element_type=F32)
        y += jnp.dot(mc, w_ref[W_A + W_B:D_MODEL, c0:c1], preferred_element_type=F32)
        o_ref[:, c0:c1] = x[:, c0:c1] + gate[:, c0:c1] * y


def _outproj_mid_kernel(ma_ref, mbl_ref, mbc_ref, mcl_ref, mcc_ref, xl_ref, xc_ref, w_ref, mod_ref, o_ref):
    _outproj_body(ma_ref[...], _pick_rows(mbl_ref, mbc_ref), _pick_rows(mcl_ref, mcc_ref),
                  _pick_rows(xl_ref, xc_ref), w_ref, mod_ref, o_ref)


def _outproj_final_kernel(ma_ref, mb_ref, mc_ref, x_ref, w_ref, mod_ref, fg_ref, o_ref):
    _outproj_body(ma_ref[...], mb_ref[...], mc_ref[...], x_ref[...], w_ref, mod_ref, o_ref)
    o_ref[...] = _rms(o_ref[...], fg_ref[...])


def _outproj_mid(ma, mb, mb_ctx, mc, mc_ctx, x_lat, x_ctx, ctx_block, w_out_b, mods_l):
    tm = ROW_TILE
    est = (3 * tm * D_MODEL * 2 + D_MODEL * D_MODEL * 2 + 5 * tm * D_MODEL * 4 + 3 * tm * OUT_CHUNK * 4)
    lat = lambda w: pl.BlockSpec((tm, w), lambda t: (_lat_tile(t), 0))
    ctx = lambda w, blk: pl.BlockSpec((tm, w), lambda t: (blk, 0), pipeline_mode=pl.Buffered(1))
    return pl.pallas_call(
        _outproj_mid_kernel,
        grid=(ROW_TILES,),
        in_specs=[pl.BlockSpec((tm, W_A), lambda t: (t, 0)),
                  lat(W_B), ctx(W_B, 0), lat(W_C), ctx(W_C, 0), lat(D_MODEL), ctx(D_MODEL, ctx_block),
                  pl.BlockSpec((D_MODEL, D_MODEL), lambda t: (0, 0), pipeline_mode=pl.Buffered(1)),
                  pl.BlockSpec((1, 1, 3 * D_MODEL), lambda t: (_mod_row(t), 0, 0))],
        out_specs=pl.BlockSpec((tm, D_MODEL), lambda t: (t, 0)),
        out_shape=jax.ShapeDtypeStruct((N_TOK, D_MODEL), F32),
        compiler_params=pltpu.CompilerParams(dimension_semantics=("parallel",),
                                             vmem_limit_bytes=_vmem_limit(est)),
    )(ma, mb, mb_ctx, mc, mc_ctx, x_lat, x_ctx, w_out_b, mods_l.reshape(MOD_ROWS, 1, 3 * D_MODEL))


def _outproj_final(ma, mb, mc, x_lat, w_out_b, mods_l, final_g):
    tm = ROW_TILE
    est = (2 * tm * D_MODEL * 2 + D_MODEL * D_MODEL * 2 + 4 * tm * D_MODEL * 4 + 3 * tm * OUT_CHUNK * 4)
    return pl.pallas_call(
        _outproj_final_kernel,
        grid=(LAT_TILES,),
        in_specs=[pl.BlockSpec((tm, W_A), lambda t: (t, 0)),
                  pl.BlockSpec((tm, W_B), lambda t: (t, 0)),
                  pl.BlockSpec((tm, W_C), lambda t: (t, 0)),
                  pl.BlockSpec((tm, D_MODEL), lambda t: (t, 0)),
                  pl.BlockSpec((D_MODEL, D_MODEL), lambda t: (0, 0), pipeline_mode=pl.Buffered(1)),
                  pl.BlockSpec((1, 1, 3 * D_MODEL), lambda t: (_mod_row(t), 0, 0)),
                  pl.BlockSpec((1, D_MODEL), lambda t: (0, 0))],
        out_specs=pl.BlockSpec((tm, D_MODEL), lambda t: (t, 0)),
        out_shape=jax.ShapeDtypeStruct((N_LAT, D_MODEL), F32),
        compiler_params=pltpu.CompilerParams(dimension_semantics=("parallel",),
                                             vmem_limit_bytes=_vmem_limit(est)),
    )(ma, mb, mc, x_lat, w_out_b, mods_l.reshape(MOD_ROWS, 1, 3 * D_MODEL), final_g.reshape(1, D_MODEL))


def _cols(w, name):
    a, b = _SRC[name]
    return w[:, a:b]


def _prep_w_in(w):
    kr = _cols(w, "kr")
    kr_sw = jnp.concatenate([kr[:, _Q4:2 * _Q4], kr[:, 0:_Q4], kr[:, 3 * _Q4:4 * _Q4], kr[:, 2 * _Q4:3 * _Q4]], axis=1)
    parts = [_cols(w, n) for n in ("gb", "u", "va", "ga", "cq", "ckv", "qc", "kc", "vc", "gc")]
    return jnp.concatenate(parts + [kr, kr, kr_sw, kr_sw], axis=1).astype(BF16)


def _prep_w_uq(w):
    w3 = w.reshape(Q_LORA, B_HEADS, NOPE_DIM + ROPE_DIM)
    nope = w3[:, :, :NOPE_DIM].reshape(Q_LORA, B_HEADS * NOPE_DIM)
    rope = w3[:, :, NOPE_DIM:]
    rope_sw = jnp.concatenate([rope[..., _Q4:2 * _Q4], rope[..., 0:_Q4],
                               rope[..., 3 * _Q4:4 * _Q4], rope[..., 2 * _Q4:3 * _Q4]], axis=-1)
    return jnp.concatenate([nope, rope.reshape(Q_LORA, -1), rope_sw.reshape(Q_LORA, -1)], axis=1).astype(BF16)


def _prep_w_ukv(w):
    w3 = w.reshape(KV_LORA, B_HEADS, NOPE_DIM + V_DIM)
    return jnp.concatenate([w3[:, :, :NOPE_DIM].reshape(KV_LORA, -1),
                            w3[:, :, NOPE_DIM:].reshape(KV_LORA, -1)], axis=1).astype(BF16)


def _rope_tables():
    t = np.arange(SEQ)
    row = (t // GRID_W).astype(np.float32)
    col = (t % GRID_W).astype(np.float32)
    inv = (np.float32(ROPE_THETA) ** (-np.arange(_Q4, dtype=np.float32) / np.float32(_Q4))).astype(np.float32)
    ar, ac = row[:, None] * inv, col[:, None] * inv
    cos = np.concatenate([np.cos(ar), np.cos(ar), np.cos(ac), np.cos(ac)], axis=1)
    sin = np.concatenate([-np.sin(ar), np.sin(ar), -np.sin(ac), np.sin(ac)], axis=1)
    cos = np.concatenate([cos, np.ones((ROW_TILE, ROPE_DIM))], axis=0).astype(np.float32)
    sin = np.concatenate([sin, np.zeros((ROW_TILE, ROPE_DIM))], axis=0).astype(np.float32)
    return np.concatenate([cos, cos], axis=1), np.concatenate([sin, sin], axis=1)


_COS2, _SIN2 = _rope_tables()


def _rope_tile(t):
    return jnp.where(t < LAT_TILES, t % TILES_PER_SAMPLE, TILES_PER_SAMPLE)


def kernel(x, c, ctx, c_ctx, w_ada, b_ada, norm_g, w_in, qa_g, kva_g, w_uq, w_ukv, sgu_g, w_s, b_s, rpb,
           w_out, final_g):
    assert x.shape == (BATCH, SEQ, D_MODEL) and ctx.shape == (BATCH, CTX_LEN, D_MODEL)
    assert w_in.shape == (DEPTH, D_MODEL, IN_W)
    cc = jnp.concatenate([c, c_ctx[None, :], jnp.zeros((MOD_ROWS - BATCH - 1, D_MODEL), F32)], axis=0)
    mods = _modulation(cc, w_ada, b_ada)
    cos2, sin2 = jnp.asarray(_COS2), jnp.asarray(_SIN2)

    x_lat, x_ctx, ctx_block = x.reshape(N_LAT, D_MODEL), ctx.reshape(N_CTX, D_MODEL), 0
    for l in range(DEPTH):
        last = l == DEPTH - 1
        z = _inproj(x_lat, x_ctx, ctx_block, mods[l], norm_g[l], _prep_w_in(w_in[l]))
        q, k, vt = _upproj(z, cos2, sin2, qa_g[l], kva_g[l], _prep_w_uq(w_uq[l]), _prep_w_ukv(w_ukv[l]))
        b_s_exp = jnp.repeat(b_s[l].T, HEAD_DIM, axis=1)
        ma = _gmlp(z, sgu_g[l], w_s[l].astype(BF16), b_s_exp, LAT_TILES if last else ROW_TILES)
        mb, mb_ctx = _mla(q, k, vt, z, with_ctx_queries=not last)
        mc, mc_ctx = _natten(z, _na_bias_tables(rpb[l]), with_ctx_queries=not last)
        w_out_b = w_out[l].astype(BF16)
        if last:
            out = _outproj_final(ma, mb, mc, x_lat, w_out_b, mods[l], final_g)
            return out.reshape(BATCH, SEQ, D_MODEL)
        xf = _outproj_mid(ma, mb, mb_ctx, mc, mc_ctx, x_lat, x_ctx, ctx_block, w_out_b, mods[l])
        x_lat, x_ctx, ctx_block = xf, xf, LAT_TILES
```

```python
import functools
import math

import numpy as np
import jax
import jax.numpy as jnp
from jax import lax
from jax.experimental import pallas as pl
from jax.experimental.pallas import tpu as pltpu

D_MODEL = 2048
BATCH = 2
SEQ = 4096
DEPTH = 2
GRID_W = 64
CTX_LEN = 256
EPS = 1e-6
NEG = -1e30
HEAD_DIM = 128
W_A = D_MODEL // 4
W_B = D_MODEL // 2
W_C = D_MODEL // 4
CHUNK = 128
A_GROUPS = W_A // HEAD_DIM
B_HEADS = W_B // HEAD_DIM
Q_LORA = D_MODEL // 4
KV_LORA = 512
NOPE_DIM = 128
ROPE_DIM = 64
V_DIM = 128
MLA_SCALE = (NOPE_DIM + ROPE_DIM) ** -0.5
ROPE_THETA = 10000.0
C_HEADS = W_C // HEAD_DIM
MAX_KH = 8
KW = 16
C_SCALE = HEAD_DIM ** -0.5
ROWS = SEQ // GRID_W

LANES = 128
MXU_DIM = 256
VMEM_BYTES_V7X = 64 * 1024 * 1024
VMEM_LIMIT_CAP = 56 * 1024 * 1024

F32 = jnp.float32
BF16 = jnp.bfloat16

N_LAT = BATCH * SEQ
N_CTX = BATCH * CTX_LEN
N_TOK = N_LAT + N_CTX
ROW_TILE = N_CTX
LAT_TILES = N_LAT // ROW_TILE
ROW_TILES = N_TOK // ROW_TILE
TILES_PER_SAMPLE = SEQ // ROW_TILE
Q_TILE = CTX_LEN
Q_TILES_PER_SAMPLE = SEQ // Q_TILE
KV_CHUNK = 256
assert ROW_TILE % CHUNK == 0 and SEQ % ROW_TILE == 0 and SEQ % Q_TILE == 0

OFF_U = 0
OFF_VA = OFF_U + W_A
OFF_GA = OFF_VA + W_A
OFF_CQ = OFF_GA + W_A
OFF_CKV = OFF_CQ + Q_LORA
OFF_GB = OFF_CKV + KV_LORA
OFF_QC = OFF_GB + W_B
OFF_KC = OFF_QC + W_C
OFF_VC = OFF_KC + W_C
OFF_GC = OFF_VC + W_C
OFF_KR = OFF_GC + W_C
Z_W = OFF_KR + 4 * ROPE_DIM
Z_CHUNK = 512
assert Z_W % MXU_DIM == 0 and OFF_KR % (4 * ROPE_DIM) == 0

_SRC = {}
_acc = 0
for _name, _w in (("u", W_A), ("va", W_A), ("ga", W_A), ("cq", Q_LORA), ("ckv", KV_LORA), ("kr", ROPE_DIM),
                  ("gb", W_B), ("qc", W_C), ("kc", W_C), ("vc", W_C), ("gc", W_C)):
    _SRC[_name] = (_acc, _acc + _w)
    _acc += _w
IN_W = _acc

_Q4 = ROPE_DIM // 4
_SWAP = np.concatenate([np.arange(_Q4, 2 * _Q4), np.arange(0, _Q4),
                        np.arange(3 * _Q4, 4 * _Q4), np.arange(2 * _Q4, 3 * _Q4)])


def _vmem_limit(nbytes):
    return int(min(VMEM_LIMIT_CAP, max(16 * 1024 * 1024, nbytes * 5 // 4)))


def _silu(x):
    return x * jax.nn.sigmoid(x)


def _rms(x, g):
    return x * lax.rsqrt(jnp.mean(x * x, axis=-1, keepdims=True) + EPS) * g


MOD_ROWS = 8
MOD_TN = 768


def _mod_kernel(c_ref, w_ref, b_ref, o_ref):
    a = _silu(c_ref[...])
    o_ref[0] = jnp.dot(a, w_ref[0], preferred_element_type=F32, precision=lax.Precision.HIGHEST) + b_ref[0]


def _modulation(cc, w_ada, b_ada):
    n = 3 * D_MODEL
    est = 2 * (MOD_ROWS * D_MODEL * 4 + D_MODEL * MOD_TN * 4 + 2 * MOD_ROWS * MOD_TN * 4)
    return pl.pallas_call(
        _mod_kernel,
        grid=(DEPTH, n // MOD_TN),
        in_specs=[pl.BlockSpec((MOD_ROWS, D_MODEL), lambda l, j: (0, 0)),
                  pl.BlockSpec((1, D_MODEL, MOD_TN), lambda l, j: (l, 0, j)),
                  pl.BlockSpec((1, 1, MOD_TN), lambda l, j: (l, 0, j))],
        out_specs=pl.BlockSpec((1, MOD_ROWS, MOD_TN), lambda l, j: (l, 0, j)),
        out_shape=jax.ShapeDtypeStruct((DEPTH, MOD_ROWS, n), F32),
        compiler_params=pltpu.CompilerParams(dimension_semantics=("parallel", "parallel"),
                                             vmem_limit_bytes=_vmem_limit(est)),
    )(cc, w_ada, b_ada.reshape(DEPTH, 1, n))


def _mod_row(t):
    return jnp.where(t < LAT_TILES, t // TILES_PER_SAMPLE, BATCH)


def _lat_tile(t):
    return jnp.minimum(t, LAT_TILES - 1)


def _pick_rows(lat_ref, ctx_ref):
    return jnp.where(pl.program_id(0) < LAT_TILES, lat_ref[...], ctx_ref[...])


def _inproj_kernel(xl_ref, xc_ref, mod_ref, g_ref, w_ref, o_ref, h_ref):
    x = _pick_rows(xl_ref, xc_ref)
    shift = mod_ref[0, :, 0:D_MODEL]
    scale = mod_ref[0, :, D_MODEL:2 * D_MODEL]
    h_ref[...] = (_rms(x, g_ref[...]) * (1.0 + scale) + shift).astype(BF16)
    for c0 in range(0, Z_W, Z_CHUNK):
        c1 = min(c0 + Z_CHUNK, Z_W)
        o_ref[:, c0:c1] = jnp.dot(h_ref[...], w_ref[:, c0:c1], preferred_element_type=F32).astype(BF16)


def _inproj(x_lat, x_ctx, ctx_block, mods_l, norm_g, w_in_p):
    est = (3 * ROW_TILE * D_MODEL * 4 + D_MODEL * Z_W * 2 + 2 * ROW_TILE * Z_W * 2 + ROW_TILE * D_MODEL * 2
           + 2 * ROW_TILE * Z_CHUNK * 4)
    return pl.pallas_call(
        _inproj_kernel,
        grid=(ROW_TILES,),
        in_specs=[pl.BlockSpec((ROW_TILE, D_MODEL), lambda t: (_lat_tile(t), 0)),
                  pl.BlockSpec((ROW_TILE, D_MODEL), lambda t: (ctx_block, 0), pipeline_mode=pl.Buffered(1)),
                  pl.BlockSpec((1, 1, 3 * D_MODEL), lambda t: (_mod_row(t), 0, 0)),
                  pl.BlockSpec((1, D_MODEL), lambda t: (0, 0)),
                  pl.BlockSpec((D_MODEL, Z_W), lambda t: (0, 0), pipeline_mode=pl.Buffered(1))],
        out_specs=pl.BlockSpec((ROW_TILE, Z_W), lambda t: (t, 0)),
        out_shape=jax.ShapeDtypeStruct((N_TOK, Z_W), BF16),
        scratch_shapes=[pltpu.VMEM((ROW_TILE, D_MODEL), BF16)],
        compiler_params=pltpu.CompilerParams(dimension_semantics=("parallel",),
                                             vmem_limit_bytes=_vmem_limit(est)),
    )(x_lat, x_ctx, mods_l.reshape(MOD_ROWS, 1, 3 * D_MODEL), norm_g.reshape(1, D_MODEL), w_in_p)


QK_W = 2 * HEAD_DIM
UQ_W = B_HEADS * (NOPE_DIM + 2 * ROPE_DIM)
UKV_W = B_HEADS * (NOPE_DIM + V_DIM)
_Q_PRESCALE = MLA_SCALE * math.log2(math.e)
assert B_HEADS % 2 == 0 and 2 * ROPE_DIM == LANES


def _upproj_kernel(cq_ref, ckv_ref, kr_ref, cos_ref, sin_ref, qg_ref, kg_ref, wq_ref, wkv_ref,
                   q_ref, k_ref, vt_ref):
    cos2 = cos_ref[...]
    sin2 = sin_ref[...]
    n_rope = B_HEADS * ROPE_DIM
    cqn = _rms(cq_ref[...].astype(F32), qg_ref[...]).astype(BF16)
    q_all = jnp.dot(cqn, wq_ref[...], preferred_element_type=F32)
    rope0 = B_HEADS * NOPE_DIM
    for h in range(B_HEADS):
        q_ref[h, :, 0:NOPE_DIM] = (q_all[:, h * NOPE_DIM:(h + 1) * NOPE_DIM] * _Q_PRESCALE).astype(BF16)
    for j in range(B_HEADS // 2):
        a = q_all[:, rope0 + j * LANES: rope0 + (j + 1) * LANES]
        a_sw = q_all[:, rope0 + n_rope + j * LANES: rope0 + n_rope + (j + 1) * LANES]
        rot = ((a * cos2 + a_sw * sin2) * _Q_PRESCALE).astype(BF16)
        q_ref[2 * j, :, NOPE_DIM:QK_W] = rot
        q_ref[2 * j + 1, :, NOPE_DIM:QK_W] = rot

    ckvn = _rms(ckv_ref[...].astype(F32), kg_ref[...]).astype(BF16)
    kv_all = jnp.dot(ckvn, wkv_ref[...], preferred_element_type=F32)
    kr = kr_ref[...].astype(F32)
    krot = kr[:, 0:LANES] * cos2 + kr[:, LANES:2 * LANES] * sin2
    lane = lax.broadcasted_iota(jnp.int32, krot.shape, 1)
    k_lo = jnp.where(lane < ROPE_DIM, krot, 0.0).astype(BF16)
    k_hi = jnp.where(lane >= ROPE_DIM, krot, 0.0).astype(BF16)
    v0 = B_HEADS * NOPE_DIM
    for h in range(B_HEADS):
        k_ref[h, :, 0:NOPE_DIM] = kv_all[:, h * NOPE_DIM:(h + 1) * NOPE_DIM].astype(BF16)
        k_ref[h, :, NOPE_DIM:QK_W] = k_lo if h % 2 == 0 else k_hi
        v_h = kv_all[:, v0 + h * V_DIM: v0 + (h + 1) * V_DIM]
        for c in range(ROW_TILE // KV_CHUNK):
            vt_ref[h, c] = v_h[c * KV_CHUNK:(c + 1) * KV_CHUNK, :].T.astype(BF16)


def _upproj(z, cos2, sin2, qa_g, kva_g, w_uq_p, w_ukv_p):
    tm = ROW_TILE
    est = 2 * (2 * tm * Q_LORA * 2 + tm * 4 * ROPE_DIM * 2 + 2 * tm * LANES * 4 + Q_LORA * UQ_W * 2
               + KV_LORA * UKV_W * 2 + B_HEADS * tm * (2 * QK_W + V_DIM) * 2) + 4 * tm * UQ_W * 4
    return pl.pallas_call(
        _upproj_kernel,
        grid=(ROW_TILES,),
        in_specs=[pl.BlockSpec((tm, Q_LORA), lambda t: (t, OFF_CQ // Q_LORA)),
                  pl.BlockSpec((tm, KV_LORA), lambda t: (t, OFF_CKV // KV_LORA)),
                  pl.BlockSpec((tm, 4 * ROPE_DIM), lambda t: (t, OFF_KR // (4 * ROPE_DIM))),
                  pl.BlockSpec((tm, LANES), lambda t: (_rope_tile(t), 0)),
                  pl.BlockSpec((tm, LANES), lambda t: (_rope_tile(t), 0)),
                  pl.BlockSpec((1, Q_LORA), lambda t: (0, 0)),
                  pl.BlockSpec((1, KV_LORA), lambda t: (0, 0)),
                  pl.BlockSpec((Q_LORA, UQ_W), lambda t: (0, 0)),
                  pl.BlockSpec((KV_LORA, UKV_W), lambda t: (0, 0))],
        out_specs=[pl.BlockSpec((B_HEADS, tm, QK_W), lambda t: (0, t, 0)),
                   pl.BlockSpec((B_HEADS, tm, QK_W), lambda t: (0, t, 0)),
                   pl.BlockSpec((B_HEADS, tm // KV_CHUNK, V_DIM, KV_CHUNK), lambda t: (0, t, 0, 0))],
        out_shape=[jax.ShapeDtypeStruct((B_HEADS, N_TOK, QK_W), BF16),
                   jax.ShapeDtypeStruct((B_HEADS, N_TOK, QK_W), BF16),
                   jax.ShapeDtypeStruct((B_HEADS, N_TOK // KV_CHUNK, V_DIM, KV_CHUNK), BF16)],
        compiler_params=pltpu.CompilerParams(dimension_semantics=("parallel",),
                                             vmem_limit_bytes=_vmem_limit(est)),
    )(z, z, z, cos2, sin2, qa_g.reshape(1, Q_LORA), kva_g.reshape(1, KV_LORA), w_uq_p, w_ukv_p)


def _gmlp_kernel(u_ref, v_ref, g_ref, sg_ref, ws_ref, bs_ref, o_ref):
    v = jax.nn.gelu(v_ref[...].astype(F32))
    vn = _rms(v, sg_ref[...]).astype(BF16)
    front = jax.nn.gelu(u_ref[...].astype(F32))
    gate = _silu(g_ref[...].astype(F32))
    for c in range(ROW_TILE // CHUNK):
        r0, r1 = c * CHUNK, (c + 1) * CHUNK
        for g in range(A_GROUPS):
            c0, c1 = g * HEAD_DIM, (g + 1) * HEAD_DIM
            s = jnp.dot(ws_ref[g], vn[r0:r1, c0:c1], preferred_element_type=F32) + bs_ref[:, c0:c1]
            o_ref[r0:r1, c0:c1] = (front[r0:r1, c0:c1] * s * gate[r0:r1, c0:c1]).astype(BF16)


def _gmlp(z, sgu_g, w_s_b, b_s_exp, n_tiles):
    tm = ROW_TILE
    est = 2 * (4 * tm * W_A * 2 + A_GROUPS * CHUNK * CHUNK * 2 + CHUNK * W_A * 4) + 5 * tm * W_A * 4
    return pl.pallas_call(
        _gmlp_kernel,
        grid=(n_tiles,),
        in_specs=[pl.BlockSpec((tm, W_A), lambda t: (t, OFF_U // W_A)),
                  pl.BlockSpec((tm, W_A), lambda t: (t, OFF_VA // W_A)),
                  pl.BlockSpec((tm, W_A), lambda t: (t, OFF_GA // W_A)),
                  pl.BlockSpec((1, W_A), lambda t: (0, 0)),
                  pl.BlockSpec((A_GROUPS, CHUNK, CHUNK), lambda t: (0, 0, 0)),
                  pl.BlockSpec((CHUNK, W_A), lambda t: (0, 0))],
        out_specs=pl.BlockSpec((tm, W_A), lambda t: (t, 0)),
        out_shape=jax.ShapeDtypeStruct((n_tiles * tm, W_A), BF16),
        compiler_params=pltpu.CompilerParams(dimension_semantics=("parallel",),
                                             vmem_limit_bytes=_vmem_limit(est)),
    )(z, z, z, sgu_g.reshape(1, W_A), w_s_b, b_s_exp)


MLA_TK = 512
MLA_TQ = 1024
assert SEQ % MLA_TK == 0 and MLA_TK % KV_CHUNK == 0 and SEQ % MLA_TQ == 0


def _attend_t(q, chunks):
    def scores(k):
        return lax.dot_general(k, q, (((1,), (1,)), ((), ())), preferred_element_type=F32)

    m = l = acc = None
    s_next = scores(chunks[0][0])
    for j, (_, vts) in enumerate(chunks):
        s = s_next
        if j + 1 < len(chunks):
            s_next = scores(chunks[j + 1][0])
        s_max = jnp.max(s, axis=0, keepdims=True)
        m_new = s_max if m is None else jnp.maximum(m, s_max)
        p_sum = pv = None
        for n, vt in enumerate(vts):
            p = jnp.exp2(s[n * KV_CHUNK:(n + 1) * KV_CHUNK] - m_new)
            ps = jnp.sum(p, axis=0, keepdims=True)
            d = jnp.dot(vt, p.astype(BF16), preferred_element_type=F32)
            p_sum, pv = (ps, d) if n == 0 else (p_sum + ps, pv + d)
        if m is None:
            l, acc = p_sum, pv
        else:
            alpha = jnp.exp2(m - m_new)
            l = alpha * l + p_sum
            acc = alpha * acc + pv
        m = m_new
    return acc / l


def _mla_lat_kernel(q_ref, kl_ref, vtl_ref, kc_ref, vtc_ref, gb_ref, o_ref):
    per = MLA_TK // KV_CHUNK
    chunks = [(kl_ref[0, j * MLA_TK:(j + 1) * MLA_TK, :], [vtl_ref[0, j * per + n] for n in range(per)])
              for j in range(SEQ // MLA_TK)]
    chunks.append((kc_ref[0], [vtc_ref[0, n] for n in range(CTX_LEN // KV_CHUNK)]))
    o = _attend_t(q_ref[0], chunks).T
    o_ref[...] = (o * _silu(gb_ref[...].astype(F32))).astype(BF16)


def _mla_ctx_kernel(q_ref, kc_ref, vtc_ref, gb_ref, o_ref):
    o = _attend_t(q_ref[0], [(kc_ref[0], [vtc_ref[0, n] for n in range(CTX_LEN // KV_CHUNK)])]).T
    o_ref[...] = (o * _silu(gb_ref[...].astype(F32))).astype(BF16)


def _mla(q, k, vt, z, with_ctx_queries):
    ctx_tile = N_LAT // CTX_LEN
    tq = MLA_TQ
    nq = SEQ // tq
    est = 2 * (tq * QK_W * 2 + SEQ * QK_W * 2 + SEQ * V_DIM * 2 + CTX_LEN * (QK_W + V_DIM) * 2
               + 2 * tq * HEAD_DIM * 2) + 8 * MLA_TK * tq * 4
    mb = pl.pallas_call(
        _mla_lat_kernel,
        grid=(BATCH, B_HEADS, nq),
        in_specs=[pl.BlockSpec((1, tq, QK_W), lambda b, h, i: (h, b * nq + i, 0)),
                  pl.BlockSpec((1, SEQ, QK_W), lambda b, h, i: (h, b, 0)),
                  pl.BlockSpec((1, SEQ // KV_CHUNK, V_DIM, KV_CHUNK), lambda b, h, i: (h, b, 0, 0)),
                  pl.BlockSpec((1, CTX_LEN, QK_W), lambda b, h, i: (h, ctx_tile + b, 0)),
                  pl.BlockSpec((1, CTX_LEN // KV_CHUNK, V_DIM, KV_CHUNK), lambda b, h, i: (h, ctx_tile + b, 0, 0)),
                  pl.BlockSpec((tq, HEAD_DIM), lambda b, h, i: (b * nq + i, OFF_GB // HEAD_DIM + h))],
        out_specs=pl.BlockSpec((tq, HEAD_DIM), lambda b, h, i: (b * nq + i, h)),
        out_shape=jax.ShapeDtypeStruct((N_LAT, W_B), BF16),
        compiler_params=pltpu.CompilerParams(dimension_semantics=("parallel", "parallel", "parallel"),
                                             vmem_limit_bytes=_vmem_limit(est)),
    )(q, k, vt, k, vt, z)
    if not with_ctx_queries:
        return mb, None
    mb_ctx = pl.pallas_call(
        _mla_ctx_kernel,
        grid=(BATCH, B_HEADS),
        in_specs=[pl.BlockSpec((1, CTX_LEN, QK_W), lambda b, h: (h, ctx_tile + b, 0)),
                  pl.BlockSpec((1, CTX_LEN, QK_W), lambda b, h: (h, ctx_tile + b, 0)),
                  pl.BlockSpec((1, CTX_LEN // KV_CHUNK, V_DIM, KV_CHUNK), lambda b, h: (h, ctx_tile + b, 0, 0)),
                  pl.BlockSpec((CTX_LEN, HEAD_DIM), lambda b, h: (ctx_tile + b, OFF_GB // HEAD_DIM + h))],
        out_specs=pl.BlockSpec((CTX_LEN, HEAD_DIM), lambda b, h: (b, h)),
        out_shape=jax.ShapeDtypeStruct((N_CTX, W_B), BF16),
        compiler_params=pltpu.CompilerParams(dimension_semantics=("parallel", "parallel")),
    )(q, k, vt, z)
    return mb, mb_ctx


NA_QROWS = Q_TILE // GRID_W
NA_KROWS = 12
NA_KEYS = NA_KROWS * GRID_W
assert NA_QROWS + MAX_KH <= NA_KROWS + 1 and ROWS >= NA_KROWS


def _na_strip_row(j):
    return int(np.clip(j * NA_QROWS - MAX_KH // 2, 0, ROWS - NA_KROWS))


def _na_tables():
    def one(j):
        s = _na_strip_row(j)
        valid = np.zeros((NA_QROWS, NA_KROWS), bool)
        d = np.zeros((NA_QROWS, NA_KROWS), np.int32)
        for a in range(NA_QROWS):
            r = j * NA_QROWS + a
            r0 = int(np.clip(r - MAX_KH // 2, 0, ROWS - MAX_KH))
            for i in range(NA_KROWS):
                valid[a, i] = 0 <= s + i - r0 < MAX_KH
                d[a, i] = s + i - r + (MAX_KH - 1) if valid[a, i] else 0
        return valid, d
    n_tiles = ROWS // NA_QROWS
    first, mid, last = one(0), one(1), one(n_tiles - 1)
    for j in range(1, n_tiles - 1):
        v, d = one(j)
        assert (v == mid[0]).all() and (d == mid[1]).all()
    return np.stack([first[0], mid[0], last[0]]), np.stack([first[1], mid[1], last[1]])


_NA_VALID, _NA_DROW = _na_tables()


def _na_bias_tables(rpb_l):
    col = np.arange(GRID_W)
    c0 = np.clip(col - KW // 2, 0, GRID_W - KW)
    col_ok = (col[None, :] >= c0[:, None]) & (col[None, :] < c0[:, None] + KW)
    dc = np.clip(col[None, :] - col[:, None], -(KW - 1), KW - 1) + (KW - 1)
    onehot = (dc.reshape(-1)[None, :] == np.arange(2 * KW - 1)[:, None]).astype(np.float32)
    t1 = jnp.dot(rpb_l.reshape(C_HEADS * (2 * MAX_KH - 1), 2 * KW - 1), jnp.asarray(onehot),
                 precision=lax.Precision.HIGHEST).reshape(C_HEADS, 2 * MAX_KH - 1, GRID_W, GRID_W)
    t1 = jnp.where(jnp.asarray(col_ok)[None, None], t1, NEG)
    return jnp.concatenate([t1, t1], axis=-1)


def _softmax_rows(blocks):
    m = functools.reduce(jnp.maximum, [jnp.max(s, axis=-1, keepdims=True) for s in blocks])
    ps = [jnp.exp(s - m) for s in blocks]
    return ps, functools.reduce(jnp.add, [jnp.sum(p, axis=-1, keepdims=True) for p in ps])


def _assemble_bias(t_ref, bias_ref):
    lane = lax.broadcasted_iota(jnp.int32, (GRID_W, 2 * GRID_W), 1)
    neg = jnp.full((GRID_W, 2 * GRID_W), NEG, F32)
    for cls in range(3):
        for a in range(NA_QROWS):
            for pair in range(NA_KROWS // 2):
                halves = [t_ref[0, int(_NA_DROW[cls, a, i])] if _NA_VALID[cls, a, i] else neg
                          for i in (2 * pair, 2 * pair + 1)]
                tile = halves[0] if halves[0] is halves[1] else jnp.where(lane < GRID_W, halves[0], halves[1])
                bias_ref[cls, a * GRID_W:(a + 1) * GRID_W, pair * 2 * GRID_W:(pair + 1) * 2 * GRID_W] = tile


def _natten_kernel(q_ref, kl_ref, vl_ref, kc_ref, vc_ref, t_ref, g_ref, o_ref, bias_ref):
    nt = (((1,), (1,)), ((), ()))
    n_tiles = ROWS // NA_QROWS
    _assemble_bias(t_ref, bias_ref)

    def tile(j, carry):
        rows = pl.ds(pl.multiple_of(j * Q_TILE, Q_TILE), Q_TILE)
        strip_row = jnp.clip(j * NA_QROWS - MAX_KH // 2, 0, ROWS - NA_KROWS)
        strip = pl.ds(pl.multiple_of(strip_row * GRID_W, GRID_W), NA_KEYS)
        cls = jnp.where(j == 0, 0, jnp.where(j == n_tiles - 1, 2, 1))
        q = q_ref[rows, :]
        s_nb = lax.dot_general(q, kl_ref[strip, :], nt, preferred_element_type=F32) * C_SCALE + bias_ref[cls]
        s_cx = lax.dot_general(q, kc_ref[...], nt, preferred_element_type=F32) * C_SCALE
        (p_nb, p_cx), l = _softmax_rows([s_nb, s_cx])
        o = jnp.dot(p_nb.astype(BF16), vl_ref[strip, :], preferred_element_type=F32)
        o += jnp.dot(p_cx.astype(BF16), vc_ref[...], preferred_element_type=F32)
        o_ref[rows, :] = (o / l * _silu(g_ref[rows, :].astype(F32))).astype(BF16)
        return carry

    lax.fori_loop(0, n_tiles, tile, 0, unroll=2)


def _natten_ctx_kernel(q_ref, kc_ref, vc_ref, g_ref, o_ref):
    s =lax.dot_general(q_ref[...], kc_ref[...], (((1,), (1,)), ((), ())), preferred_element_type=F32) * C_SCALE
    (p,), l = _softmax_rows([s])
    o = jnp.dot(p.astype(BF16), vc_ref[...], preferred_element_type=F32)
    o_ref[...] = (o / l * _silu(g_ref[...].astype(F32))).astype(BF16)


def _natten(z, bias_tab, with_ctx_queries):
    ctx_tile = N_LAT // CTX_LEN
    hd = HEAD_DIM
    est = (2 * (5 * SEQ * hd * 2 + 2 * CTX_LEN * hd * 2 + 3 * Q_TILE * NA_KEYS * 4)
           + 12 * Q_TILE * (NA_KEYS + CTX_LEN) * 4)
    mc = pl.pallas_call(
        _natten_kernel,
        grid=(BATCH, C_HEADS),
        in_specs=[pl.BlockSpec((SEQ, hd), lambda b, h: (b, OFF_QC // hd + h)),
                  pl.BlockSpec((SEQ, hd), lambda b, h: (b, OFF_KC // hd + h)),
                  pl.BlockSpec((SEQ, hd), lambda b, h: (b, OFF_VC // hd + h)),
                  pl.BlockSpec((CTX_LEN, hd), lambda b, h: (ctx_tile + b, OFF_KC // hd + h)),
                  pl.BlockSpec((CTX_LEN, hd), lambda b, h: (ctx_tile + b, OFF_VC // hd + h)),
                  pl.BlockSpec((1, 2 * MAX_KH - 1, GRID_W, 2 * GRID_W), lambda b, h: (h, 0, 0, 0)),
                  pl.BlockSpec((SEQ, hd), lambda b, h: (b, OFF_GC // hd + h))],
        out_specs=pl.BlockSpec((SEQ, hd), lambda b, h: (b, h)),
        out_shape=jax.ShapeDtypeStruct((N_LAT, W_C), BF16),
        scratch_shapes=[pltpu.VMEM((3, Q_TILE, NA_KEYS), F32)],
        compiler_params=pltpu.CompilerParams(dimension_semantics=("parallel", "parallel"),
                                             vmem_limit_bytes=_vmem_limit(est)),
    )(z, z, z, z, z, bias_tab, z)
    if not with_ctx_queries:
        return mc, None
    mc_ctx = pl.pallas_call(
        _natten_ctx_kernel,
        grid=(BATCH, C_HEADS),
        in_specs=[pl.BlockSpec((CTX_LEN, hd), lambda b, h: (ctx_tile + b, OFF_QC // hd + h)),
                  pl.BlockSpec((CTX_LEN, hd), lambda b, h: (ctx_tile + b, OFF_KC // hd + h)),
                  pl.BlockSpec((CTX_LEN, hd), lambda b, h: (ctx_tile + b, OFF_VC // hd + h)),
                  pl.BlockSpec((CTX_LEN, hd), lambda b, h: (ctx_tile + b, OFF_GC // hd + h))],
        out_specs=pl.BlockSpec((CTX_LEN, hd), lambda b, h: (b, h)),
        out_shape=jax.ShapeDtypeStruct((N_CTX, W_C), BF16),
        compiler_params=pltpu.CompilerParams(dimension_semantics=("parallel", "parallel")),
    )(z, z, z, z)
    return mc, mc_ctx


OUT_CHUNK = 512


def _outproj_body(ma, mb, mc, x, w_ref, mod_ref, o_ref):
    gate = mod_ref[0, :, 2 * D_MODEL:3 * D_MODEL]
    for c0 in range(0, D_MODEL, OUT_CHUNK):
        c1 = c0 + OUT_CHUNK
        y = jnp.dot(ma, w_ref[0:W_A, c0:c1], preferred_element_type=F32)
        y += jnp.dot(mb, w_ref[W_A:W_A + W_B, c0:c1], preferred_element_type=F32)
        y += jnp.dot(mc, w_ref[W_A + W_B:D_MODEL, c0:c1], preferred_element_type=F32)
        o_ref[:, c0:c1] = x[:, c0:c1] + gate[:, c0:c1] * y


def _outproj_mid_kernel(ma_ref, mbl_ref, mbc_ref, mcl_ref, mcc_ref, xl_ref, xc_ref, w_ref, mod_ref, o_ref):
    _outproj_body(ma_ref[...], _pick_rows(mbl_ref, mbc_ref), _pick_rows(mcl_ref, mcc_ref),
                  _pick_rows(xl_ref, xc_ref), w_ref, mod_ref, o_ref)


def _outproj_final_kernel(ma_ref, mb_ref, mc_ref, x_ref, w_ref, mod_ref, fg_ref, o_ref):
    _outproj_body(ma_ref[...], mb_ref[...], mc_ref[...], x_ref[...], w_ref, mod_ref, o_ref)
    o_ref[...] = _rms(o_ref[...], fg_ref[...])


def _outproj_mid(ma, mb, mb_ctx, mc, mc_ctx, x_lat, x_ctx, ctx_block, w_out_b, mods_l):
    tm = ROW_TILE
    est = (3 * tm * D_MODEL * 2 + D_MODEL * D_MODEL * 2 + 5 * tm * D_MODEL * 4 + 3 * tm * OUT_CHUNK * 4)
    lat = lambda w: pl.BlockSpec((tm, w), lambda t: (_lat_tile(t), 0))
    ctx = lambda w, blk: pl.BlockSpec((tm, w), lambda t: (blk, 0), pipeline_mode=pl.Buffered(1))
    return pl.pallas_call(
        _outproj_mid_kernel,
        grid=(ROW_TILES,),
        in_specs=[pl.BlockSpec((tm, W_A), lambda t: (t, 0)),
                  lat(W_B), ctx(W_B, 0), lat(W_C), ctx(W_C, 0), lat(D_MODEL), ctx(D_MODEL, ctx_block),
                  pl.BlockSpec((D_MODEL, D_MODEL), lambda t: (0, 0), pipeline_mode=pl.Buffered(1)),
                  pl.BlockSpec((1, 1, 3 * D_MODEL), lambda t: (_mod_row(t), 0, 0))],
        out_specs=pl.BlockSpec((tm, D_MODEL), lambda t: (t, 0)),
        out_shape=jax.ShapeDtypeStruct((N_TOK, D_MODEL), F32),
        compiler_params=pltpu.CompilerParams(dimension_semantics=("parallel",),
                                             vmem_limit_bytes=_vmem_limit(est)),
    )(ma, mb, mb_ctx, mc, mc_ctx, x_lat, x_ctx, w_out_b, mods_l.reshape(MOD_ROWS, 1, 3 * D_MODEL))


def _outproj_final(ma, mb, mc, x_lat, w_out_b, mods_l, final_g):
    tm = ROW_TILE
    est = (2 * tm * D_MODEL * 2 + D_MODEL * D_MODEL * 2 + 4 * tm * D_MODEL * 4 + 3 * tm * OUT_CHUNK * 4)
    return pl.pallas_call(
        _outproj_final_kernel,
        grid=(LAT_TILES,),
        in_specs=[pl.BlockSpec((tm, W_A), lambda t: (t, 0)),
                  pl.BlockSpec((tm, W_B), lambda t: (t, 0)),
                  pl.BlockSpec((tm, W_C), lambda t: (t, 0)),
                  pl.BlockSpec((tm, D_MODEL), lambda t: (t, 0)),
                  pl.BlockSpec((D_MODEL, D_MODEL), lambda t: (0, 0), pipeline_mode=pl.Buffered(1)),
                  pl.BlockSpec((1, 1, 3 * D_MODEL), lambda t: (_mod_row(t), 0, 0)),
                  pl.BlockSpec((1, D_MODEL), lambda t: (0, 0))],
        out_specs=pl.BlockSpec((tm, D_MODEL), lambda t: (t, 0)),
        out_shape=jax.ShapeDtypeStruct((N_LAT, D_MODEL), F32),
        compiler_params=pltpu.CompilerParams(dimension_semantics=("parallel",),
                                             vmem_limit_bytes=_vmem_limit(est)),
    )(ma, mb, mc, x_lat, w_out_b, mods_l.reshape(MOD_ROWS, 1, 3 * D_MODEL), final_g.reshape(1, D_MODEL))


def _cols(w, name):
    a, b = _SRC[name]
    return w[:, a:b]


def _prep_w_in(w):
    kr = _cols(w, "kr")
    kr_sw = jnp.concatenate([kr[:, _Q4:2 * _Q4], kr[:, 0:_Q4], kr[:, 3 * _Q4:4 * _Q4], kr[:, 2 * _Q4:3 * _Q4]], axis=1)
    head = w[:, _SRC["u"][0]:_SRC["ckv"][1]]
    tail = w[:, _SRC["gb"][0]:_SRC["gc"][1]]
    assert _SRC["ckv"][1] == OFF_GB and head.shape[1] + tail.shape[1] == OFF_KR
    return jnp.concatenate([head, tail, kr, kr, kr_sw, kr_sw], axis=1).astype(BF16)


def _prep_w_uq(w):
    w3 = w.reshape(Q_LORA, B_HEADS, NOPE_DIM + ROPE_DIM)
    nope = w3[:, :, :NOPE_DIM].reshape(Q_LORA, B_HEADS * NOPE_DIM)
    rope = w3[:, :, NOPE_DIM:]
    rope_sw = jnp.concatenate([rope[..., _Q4:2 * _Q4], rope[..., 0:_Q4],
                               rope[..., 3 * _Q4:4 * _Q4], rope[..., 2 * _Q4:3 * _Q4]], axis=-1)
    return jnp.concatenate([nope, rope.reshape(Q_LORA, -1), rope_sw.reshape(Q_LORA, -1)], axis=1).astype(BF16)


def _prep_w_ukv(w):
    w3 = w.reshape(KV_LORA, B_HEADS, NOPE_DIM + V_DIM)
    return jnp.concatenate([w3[:, :, :NOPE_DIM].reshape(KV_LORA, -1),
                            w3[:, :, NOPE_DIM:].reshape(KV_LORA, -1)], axis=1).astype(BF16)


def _rope_tables():
    t = np.arange(SEQ)
    row = (t // GRID_W).astype(np.float32)
    col = (t % GRID_W).astype(np.float32)
    inv = (np.float32(ROPE_THETA) ** (-np.arange(_Q4, dtype=np.float32) / np.float32(_Q4))).astype(np.float32)
    ar, ac = row[:, None] * inv, col[:, None] * inv
    cos = np.concatenate([np.cos(ar), np.cos(ar), np.cos(ac), np.cos(ac)], axis=1)
    sin = np.concatenate([-np.sin(ar), np.sin(ar), -np.sin(ac), np.sin(ac)], axis=1)
    cos = np.concatenate([cos, np.ones((ROW_TILE, ROPE_DIM))], axis=0).astype(np.float32)
    sin = np.concatenate([sin, np.zeros((ROW_TILE, ROPE_DIM))], axis=0).astype(np.float32)
    return np.concatenate([cos, cos], axis=1), np.concatenate([sin, sin], axis=1)


_COS2, _SIN2 = _rope_tables()


def _rope_tile(t):
    return jnp.where(t < LAT_TILES, t % TILES_PER_SAMPLE, TILES_PER_SAMPLE)


def kernel(x, c, ctx, c_ctx, w_ada, b_ada, norm_g, w_in, qa_g, kva_g, w_uq, w_ukv, sgu_g, w_s, b_s, rpb,
           w_out, final_g):
    assert x.shape == (BATCH, SEQ, D_MODEL) and ctx.shape == (BATCH, CTX_LEN, D_MODEL)
    assert w_in.shape == (DEPTH, D_MODEL, IN_W)
    cc = jnp.concatenate([c, c_ctx[None, :], jnp.zeros((MOD_ROWS - BATCH - 1, D_MODEL), F32)], axis=0)
    mods = _modulation(cc, w_ada, b_ada)
    cos2, sin2 = jnp.asarray(_COS2), jnp.asarray(_SIN2)

    x_lat, x_ctx, ctx_block = x.reshape(N_LAT, D_MODEL), ctx.reshape(N_CTX, D_MODEL), 0
    for l in range(DEPTH):
        last = l == DEPTH - 1
        z = _inproj(x_lat, x_ctx, ctx_block, mods[l], norm_g[l], _prep_w_in(w_in[l]))
        q, k, vt = _upproj(z, cos2, sin2, qa_g[l], kva_g[l], _prep_w_uq(w_uq[l]), _prep_w_ukv(w_ukv[l]))
        b_s_exp = jnp.repeat(b_s[l].T, HEAD_DIM, axis=1)
        ma = _gmlp(z, sgu_g[l], w_s[l].astype(BF16), b_s_exp, LAT_TILES if last else ROW_TILES)
        mb, mb_ctx = _mla(q, k, vt, z, with_ctx_queries=not last)
        mc, mc_ctx = _natten(z, _na_bias_tables(rpb[l]), with_ctx_queries=not last)
        w_out_b = w_out[l].astype(BF16)
        if last:
            out = _outproj_final(ma, mb, mc, x_lat, w_out_b, mods[l], final_g)
            return out.reshape(BATCH, SEQ, D_MODEL)
        xf = _outproj_mid(ma, mb, mb_ctx, mc, mc_ctx, x_lat, x_ctx, ctx_block, w_out_b, mods[l])
        x_lat, x_ctx, ctx_block = xf, xf, LAT_TILES
```

```python
import functools
import math

import numpy as np
import jax
import jax.numpy as jnp
from jax import lax
from jax.experimental import pallas as pl
from jax.experimental.pallas import tpu as pltpu

D_MODEL = 2048
BATCH = 2
SEQ = 4096
DEPTH = 2
GRID_W = 64
CTX_LEN = 256
EPS = 1e-6
NEG = -1e30
HEAD_DIM = 128
W_A = D_MODEL // 4
W_B = D_MODEL // 2
W_C = D_MODEL // 4
MIX_W = W_A + W_B + W_C
CHUNK = 128
A_GROUPS = W_A // HEAD_DIM
B_HEADS = W_B // HEAD_DIM
Q_LORA = D_MODEL // 4
KV_LORA = 512
NOPE_DIM = 128
ROPE_DIM = 64
V_DIM = 128
MLA_SCALE = (NOPE_DIM + ROPE_DIM) ** -0.5
ROPE_THETA = 10000.0
C_HEADS = W_C // HEAD_DIM
MAX_KH = 8
KW = 16
C_SCALE = HEAD_DIM ** -0.5
ROWS = SEQ // GRID_W

LANES = 128
MXU_DIM = 256
VMEM_BYTES_V7X = 64 * 1024 * 1024
VMEM_LIMIT_CAP = 56 * 1024 * 1024

F32 = jnp.float32
BF16 = jnp.bfloat16

N_LAT = BATCH * SEQ
N_CTX = BATCH * CTX_LEN
N_TOK = N_LAT + N_CTX
ROW_TILE = N_CTX
LAT_TILES = N_LAT // ROW_TILE
ROW_TILES = N_TOK // ROW_TILE
TILES_PER_SAMPLE = SEQ // ROW_TILE
Q_TILE = CTX_LEN
Q_TILES_PER_SAMPLE = SEQ // Q_TILE
KV_CHUNK = 256
assert ROW_TILE % CHUNK == 0 and SEQ % ROW_TILE == 0 and SEQ % Q_TILE == 0

OFF_U = 0
OFF_VA = OFF_U + W_A
OFF_GA = OFF_VA + W_A
OFF_CQ = OFF_GA + W_A
OFF_CKV = OFF_CQ + Q_LORA
OFF_GB = OFF_CKV + KV_LORA
OFF_QC = OFF_GB + W_B
OFF_KC = OFF_QC + W_C
OFF_VC = OFF_KC + W_C
OFF_GC = OFF_VC + W_C
OFF_KR = OFF_GC + W_C
Z_W = OFF_KR + 4 * ROPE_DIM
Z_CHUNK = 512
assert Z_W % MXU_DIM == 0 and OFF_KR % (4 * ROPE_DIM) == 0

_SRC = {}
_acc = 0
for _name, _w in (("u", W_A), ("va", W_A), ("ga", W_A), ("cq", Q_LORA), ("ckv", KV_LORA), ("kr", ROPE_DIM),
                  ("gb", W_B), ("qc", W_C), ("kc", W_C), ("vc", W_C), ("gc", W_C)):
    _SRC[_name] = (_acc, _acc + _w)
    _acc += _w
IN_W = _acc

_Q4 = ROPE_DIM // 4
_SWAP = np.concatenate([np.arange(_Q4, 2 * _Q4), np.arange(0, _Q4),
                        np.arange(3 * _Q4, 4 * _Q4), np.arange(2 * _Q4, 3 * _Q4)])


def _vmem_limit(nbytes):
    return int(min(VMEM_LIMIT_CAP, max(16 * 1024 * 1024, nbytes * 5 // 4)))


def _silu(x):
    return x * jax.nn.sigmoid(x)


def _rms(x, g):
    return x * lax.rsqrt(jnp.mean(x * x, axis=-1, keepdims=True) + EPS) * g


MOD_ROWS = 8
MOD_TN = 768


def _mod_kernel(c_ref, w_ref, b_ref, o_ref):
    a = _silu(c_ref[...])
    a_hi = a.astype(BF16)
    a_lo = (a - a_hi.astype(F32)).astype(BF16)
    w = w_ref[0]
    w_hi = w.astype(BF16)
    w_lo = (w - w_hi.astype(F32)).astype(BF16)
    r = jnp.dot(jnp.concatenate([a_hi, a_lo], axis=0), w_hi, preferred_element_type=F32)
    r_lo = jnp.dot(a_hi, w_lo, preferred_element_type=F32)
    o_ref[0] = r[0:MOD_ROWS] + r[MOD_ROWS:2 * MOD_ROWS] + r_lo + b_ref[0]


def _modulation(cc, w_ada, b_ada):
    n = 3 * D_MODEL
    est = 2 * (MOD_ROWS * D_MODEL * 4 + D_MODEL * MOD_TN * 4 + 2 * MOD_ROWS * MOD_TN * 4)
    return pl.pallas_call(
        _mod_kernel,
        grid=(DEPTH, n // MOD_TN),
        in_specs=[pl.BlockSpec((MOD_ROWS, D_MODEL), lambda l, j: (0, 0)),
                  pl.BlockSpec((1, D_MODEL, MOD_TN), lambda l, j: (l, 0, j)),
                  pl.BlockSpec((1, 1, MOD_TN), lambda l, j: (l, 0, j))],
        out_specs=pl.BlockSpec((1, MOD_ROWS, MOD_TN), lambda l, j: (l, 0, j)),
        out_shape=jax.ShapeDtypeStruct((DEPTH, MOD_ROWS, n), F32),
        compiler_params=pltpu.CompilerParams(dimension_semantics=("parallel", "parallel"),
                                             vmem_limit_bytes=_vmem_limit(est)),
    )(cc, w_ada, b_ada.reshape(DEPTH, 1, n))


def _mod_row(t):
    return jnp.where(t < LAT_TILES, t // TILES_PER_SAMPLE, BATCH)


def _lat_tile(t):
    return jnp.minimum(t, LAT_TILES - 1)


def _pick_rows(lat_ref, ctx_ref):
    return jnp.where(pl.program_id(0) < LAT_TILES, lat_ref[...], ctx_ref[...])


def _inproj_kernel(xl_ref, xc_ref, mod_ref, g_ref, w_ref, o_ref, h_ref):
    x = _pick_rows(xl_ref, xc_ref)
    shift = mod_ref[0, :, 0:D_MODEL]
    scale = mod_ref[0, :, D_MODEL:2 * D_MODEL]
    h_ref[...] = (_rms(x, g_ref[...]) * (1.0 + scale) + shift).astype(BF16)
    for c0 in range(0, Z_W, Z_CHUNK):
        c1 = min(c0 + Z_CHUNK, Z_W)
        o_ref[:, c0:c1] = jnp.dot(h_ref[...], w_ref[:, c0:c1], preferred_element_type=F32).astype(BF16)


def _inproj(x_lat, x_ctx, ctx_block, mods_l, norm_g, w_in_p):
    est = (3 * ROW_TILE * D_MODEL * 4 + D_MODEL * Z_W * 2 + 2 * ROW_TILE * Z_W * 2 + ROW_TILE * D_MODEL * 2
           + 2 * ROW_TILE * Z_CHUNK * 4)
    return pl.pallas_call(
        _inproj_kernel,
        grid=(ROW_TILES,),
        in_specs=[pl.BlockSpec((ROW_TILE, D_MODEL), lambda t: (_lat_tile(t), 0)),
                  pl.BlockSpec((ROW_TILE, D_MODEL), lambda t: (ctx_block, 0), pipeline_mode=pl.Buffered(1)),
                  pl.BlockSpec((1, 1, 3 * D_MODEL), lambda t: (_mod_row(t), 0, 0)),
                  pl.BlockSpec((1, D_MODEL), lambda t: (0, 0)),
                  pl.BlockSpec((D_MODEL, Z_W), lambda t: (0, 0), pipeline_mode=pl.Buffered(1))],
        out_specs=pl.BlockSpec((ROW_TILE, Z_W), lambda t: (t, 0)),
        out_shape=jax.ShapeDtypeStruct((N_TOK, Z_W), BF16),
        scratch_shapes=[pltpu.VMEM((ROW_TILE, D_MODEL), BF16)],
        compiler_params=pltpu.CompilerParams(dimension_semantics=("parallel",),
                                             vmem_limit_bytes=_vmem_limit(est)),
    )(x_lat, x_ctx, mods_l.reshape(MOD_ROWS, 1, 3 * D_MODEL), norm_g.reshape(1, D_MODEL), w_in_p)


QK_W = 2 * HEAD_DIM
UQ_W = B_HEADS * (NOPE_DIM + 2 * ROPE_DIM)
UKV_W = B_HEADS * (NOPE_DIM + V_DIM)
_Q_PRESCALE = MLA_SCALE * math.log2(math.e)
assert B_HEADS % 2 == 0 and 2 * ROPE_DIM == LANES


def _upproj_kernel(cq_ref, ckv_ref, kr_ref, cos_ref, sin_ref, qg_ref, kg_ref, wq_ref, wkv_ref,
                   q_ref, k_ref, vt_ref):
    cos2 = cos_ref[...]
    sin2 = sin_ref[...]
    n_rope = B_HEADS * ROPE_DIM
    cqn = _rms(cq_ref[...].astype(F32), qg_ref[...]).astype(BF16)
    q_all = jnp.dot(cqn, wq_ref[...], preferred_element_type=F32)
    rope0 = B_HEADS * NOPE_DIM
    for h in range(B_HEADS):
        q_ref[h, :, 0:NOPE_DIM] = (q_all[:, h * NOPE_DIM:(h + 1) * NOPE_DIM] * _Q_PRESCALE).astype(BF16)
    for j in range(B_HEADS // 2):
        a = q_all[:, rope0 + j * LANES: rope0 + (j + 1) * LANES]
        a_sw = q_all[:, rope0 + n_rope + j * LANES: rope0 + n_rope + (j + 1) * LANES]
        rot = ((a * cos2 + a_sw * sin2) * _Q_PRESCALE).astype(BF16)
        q_ref[2 * j, :, NOPE_DIM:QK_W] = rot
        q_ref[2 * j + 1, :, NOPE_DIM:QK_W] = rot

    ckvn = _rms(ckv_ref[...].astype(F32), kg_ref[...]).astype(BF16)
    kv_all = jnp.dot(ckvn, wkv_ref[...], preferred_element_type=F32)
    kr = kr_ref[...].astype(F32)
    krot = kr[:, 0:LANES] * cos2 + kr[:, LANES:2 * LANES] * sin2
    lane = lax.broadcasted_iota(jnp.int32, krot.shape, 1)
    k_lo = jnp.where(lane < ROPE_DIM, krot, 0.0).astype(BF16)
    k_hi = jnp.where(lane >= ROPE_DIM, krot, 0.0).astype(BF16)
    v0 = B_HEADS * NOPE_DIM
    for h in range(B_HEADS):
        k_ref[h, :, 0:NOPE_DIM] = kv_all[:, h * NOPE_DIM:(h + 1) * NOPE_DIM].astype(BF16)
        k_ref[h, :, NOPE_DIM:QK_W] = k_lo if h % 2 == 0 else k_hi
        v_h = kv_all[:, v0 + h * V_DIM: v0 + (h + 1) * V_DIM]
        for c in range(ROW_TILE // KV_CHUNK):
            vt_ref[h, c] = v_h[c * KV_CHUNK:(c + 1) * KV_CHUNK, :].T.astype(BF16)


def _upproj(z, cos2, sin2, qa_g, kva_g, w_uq_p, w_ukv_p):
    tm = ROW_TILE
    est = 2 * (2 * tm * Q_LORA * 2 + tm * 4 * ROPE_DIM * 2 + 2 * tm * LANES * 4 + Q_LORA * UQ_W * 2
               + KV_LORA * UKV_W * 2 + B_HEADS * tm * (2 * QK_W + V_DIM) * 2) + 4 * tm * UQ_W * 4
    return pl.pallas_call(
        _upproj_kernel,
        grid=(ROW_TILES,),
        in_specs=[pl.BlockSpec((tm, Q_LORA), lambda t: (t, OFF_CQ // Q_LORA)),
                  pl.BlockSpec((tm, KV_LORA), lambda t: (t, OFF_CKV // KV_LORA)),
                  pl.BlockSpec((tm, 4 * ROPE_DIM), lambda t: (t, OFF_KR // (4 * ROPE_DIM))),
                  pl.BlockSpec((tm, LANES), lambda t: (_rope_tile(t), 0)),
                  pl.BlockSpec((tm, LANES), lambda t: (_rope_tile(t), 0)),
                  pl.BlockSpec((1, Q_LORA), lambda t: (0, 0)),
                  pl.BlockSpec((1, KV_LORA), lambda t: (0, 0)),
                  pl.BlockSpec((Q_LORA, UQ_W), lambda t: (0, 0)),
                  pl.BlockSpec((KV_LORA, UKV_W), lambda t: (0, 0))],
        out_specs=[pl.BlockSpec((B_HEADS, tm, QK_W), lambda t: (0, t, 0)),
                   pl.BlockSpec((B_HEADS, tm, QK_W), lambda t: (0, t, 0)),
                   pl.BlockSpec((B_HEADS, tm // KV_CHUNK, V_DIM, KV_CHUNK), lambda t: (0, t, 0, 0))],
        out_shape=[jax.ShapeDtypeStruct((B_HEADS, N_TOK, QK_W), BF16),
                   jax.ShapeDtypeStruct((B_HEADS, N_TOK, QK_W), BF16),
                   jax.ShapeDtypeStruct((B_HEADS, N_TOK // KV_CHUNK, V_DIM, KV_CHUNK), BF16)],
        compiler_params=pltpu.CompilerParams(dimension_semantics=("parallel",),
                                             vmem_limit_bytes=_vmem_limit(est)),
    )(z, z, z, cos2, sin2, qa_g.reshape(1, Q_LORA), kva_g.reshape(1, KV_LORA), w_uq_p, w_ukv_p)


def _gmlp_kernel(u_ref, v_ref, g_ref, sg_ref, ws_ref, bs_ref, o_ref):
    v = jax.nn.gelu(v_ref[...].astype(F32))
    vn = _rms(v, sg_ref[...]).astype(BF16)
    front = jax.nn.gelu(u_ref[...].astype(F32))
    gate = _silu(g_ref[...].astype(F32))
    for c in range(ROW_TILE // CHUNK):
        r0, r1 = c * CHUNK, (c + 1) * CHUNK
        for g in range(A_GROUPS):
            c0, c1 = g * HEAD_DIM, (g + 1) * HEAD_DIM
            s = jnp.dot(ws_ref[g], vn[r0:r1, c0:c1], preferred_element_type=F32) + bs_ref[:, c0:c1]
            o_ref[r0:r1, c0:c1] = (front[r0:r1, c0:c1] * s * gate[r0:r1, c0:c1]).astype(BF16)


def _gmlp(z, sgu_g, w_s_b, b_s_exp, n_tiles):
    tm = ROW_TILE
    est = 2 * (4 * tm * W_A * 2 + A_GROUPS * CHUNK * CHUNK * 2 + CHUNK * W_A * 4) + 5 * tm * W_A * 4
    return pl.pallas_call(
        _gmlp_kernel,
        grid=(n_tiles,),
        in_specs=[pl.BlockSpec((tm, W_A), lambda t: (t, OFF_U // W_A)),
                  pl.BlockSpec((tm, W_A), lambda t: (t, OFF_VA // W_A)),
                  pl.BlockSpec((tm, W_A), lambda t: (t, OFF_GA // W_A)),
                  pl.BlockSpec((1, W_A), lambda t: (0, 0)),
                  pl.BlockSpec((A_GROUPS, CHUNK, CHUNK), lambda t: (0, 0, 0)),
                  pl.BlockSpec((CHUNK, W_A), lambda t: (0, 0))],
        out_specs=pl.BlockSpec((tm, W_A), lambda t: (t, 0)),
        out_shape=jax.ShapeDtypeStruct((n_tiles * tm, W_A), BF16),
        compiler_params=pltpu.CompilerParams(dimension_semantics=("parallel",),
                                             vmem_limit_bytes=_vmem_limit(est)),
    )(z, z, z, sgu_g.reshape(1, W_A), w_s_b, b_s_exp)


MLA_TK = 512
MLA_TQ = 1024
assert SEQ % MLA_TK == 0 and MLA_TK % KV_CHUNK == 0 and SEQ % MLA_TQ == 0


def _attend_t(q, chunks):
    def scores(k):
        return lax.dot_general(k, q, (((1,), (1,)), ((), ())), preferred_element_type=F32)

    m = l = acc = None
    s_next = scores(chunks[0][0])
    for j, (_, vts) in enumerate(chunks):
        s = s_next
        if j + 1 < len(chunks):
            s_next = scores(chunks[j + 1][0])
        s_max = jnp.max(s, axis=0, keepdims=True)
        m_new = s_max if m is None else jnp.maximum(m, s_max)
        p_sum = pv = None
        for n, vt in enumerate(vts):
            p = jnp.exp2(s[n * KV_CHUNK:(n + 1) * KV_CHUNK] - m_new)
            ps = jnp.sum(p, axis=0, keepdims=True)
            d = jnp.dot(vt, p.astype(BF16), preferred_element_type=F32)
            p_sum, pv = (ps, d) if n == 0 else (p_sum + ps, pv + d)
        if m is None:
            l, acc = p_sum, pv
        else:
            alpha = jnp.exp2(m - m_new)
            l = alpha * l + p_sum
            acc = alpha * acc + pv
        m = m_new
    return acc / l


def _mla_lat_kernel(q_ref, kl_ref, vtl_ref, kc_ref, vtc_ref, gb_ref, o_ref):
    per = MLA_TK // KV_CHUNK
    chunks = [(kl_ref[0, j * MLA_TK:(j + 1) * MLA_TK, :], [vtl_ref[0, j * per + n] for n in range(per)])
              for j in range(SEQ // MLA_TK)]
    chunks.append((kc_ref[0], [vtc_ref[0, n] for n in range(CTX_LEN // KV_CHUNK)]))
    o = _attend_t(q_ref[0], chunks).T
    o_ref[...] = (o * _silu(gb_ref[...].astype(F32))).astype(BF16)


def _mla_ctx_kernel(q_ref, kc_ref, vtc_ref, gb_ref, o_ref):
    o = _attend_t(q_ref[0], [(kc_ref[0], [vtc_ref[0, n] for n in range(CTX_LEN // KV_CHUNK)])]).T
    o_ref[...] = (o * _silu(gb_ref[...].astype(F32))).astype(BF16)


def _mla(q, k, vt, z, with_ctx_queries):
    ctx_tile = N_LAT // CTX_LEN
    tq = MLA_TQ
    nq = SEQ // tq
    est = 2 * (tq * QK_W * 2 + SEQ * QK_W * 2 + SEQ * V_DIM * 2 + CTX_LEN * (QK_W + V_DIM) * 2
               + 2 * tq * HEAD_DIM * 2) + 8 * MLA_TK * tq * 4
    mb = pl.pallas_call(
        _mla_lat_kernel,
        grid=(BATCH, B_HEADS, nq),
        in_specs=[pl.BlockSpec((1, tq, QK_W), lambda b, h, i: (h, b * nq + i, 0)),
                  pl.BlockSpec((1, SEQ, QK_W), lambda b, h, i: (h, b, 0)),
                  pl.BlockSpec((1, SEQ // KV_CHUNK, V_DIM, KV_CHUNK), lambda b, h, i: (h, b, 0, 0)),
                  pl.BlockSpec((1, CTX_LEN, QK_W), lambda b, h, i: (h, ctx_tile + b, 0)),
                  pl.BlockSpec((1, CTX_LEN // KV_CHUNK, V_DIM, KV_CHUNK), lambda b, h, i: (h, ctx_tile + b, 0, 0)),
                  pl.BlockSpec((tq, HEAD_DIM), lambda b, h, i: (b * nq + i, OFF_GB // HEAD_DIM + h))],
        out_specs=pl.BlockSpec((tq, HEAD_DIM), lambda b, h, i: (b * nq + i, h)),
        out_shape=jax.ShapeDtypeStruct((N_LAT, W_B), BF16),
        compiler_params=pltpu.CompilerParams(dimension_semantics=("parallel", "parallel", "parallel"),
                                             vmem_limit_bytes=_vmem_limit(est)),
    )(q, k, vt, k, vt, z)
    if not with_ctx_queries:
        return mb, None
    mb_ctx = pl.pallas_call(
        _mla_ctx_kernel,
        grid=(BATCH, B_HEADS),
        in_specs=[pl.BlockSpec((1, CTX_LEN, QK_W), lambda b, h: (h, ctx_tile + b, 0)),
                  pl.BlockSpec((1, CTX_LEN, QK_W), lambda b, h: (h, ctx_tile + b, 0)),
                  pl.BlockSpec((1, CTX_LEN // KV_CHUNK, V_DIM, KV_CHUNK), lambda b, h: (h, ctx_tile + b, 0, 0)),
                  pl.BlockSpec((CTX_LEN, HEAD_DIM), lambda b, h: (ctx_tile + b, OFF_GB // HEAD_DIM + h))],
        out_specs=pl.BlockSpec((CTX_LEN, HEAD_DIM), lambda b, h: (b, h)),
        out_shape=jax.ShapeDtypeStruct((N_CTX, W_B), BF16),
        compiler_params=pltpu.CompilerParams(dimension_semantics=("parallel", "parallel")),
    )(q, k, vt, z)
    return mb, mb_ctx


NA_QROWS = Q_TILE // GRID_W
NA_KROWS = 12
NA_KEYS = NA_KROWS * GRID_W
assert NA_QROWS + MAX_KH <= NA_KROWS + 1 and ROWS >= NA_KROWS


def _na_strip_row(j):
    return int(np.clip(j * NA_QROWS - MAX_KH // 2, 0, ROWS - NA_KROWS))


def _na_tables():
    def one(j):
        s = _na_strip_row(j)
        valid = np.zeros((NA_QROWS, NA_KROWS), bool)
        d = np.zeros((NA_QROWS, NA_KROWS), np.int32)
        for a in range(NA_QROWS):
            r = j * NA_QROWS + a
            r0 = int(np.clip(r - MAX_KH // 2, 0, ROWS - MAX_KH))
            for i in range(NA_KROWS):
                valid[a, i] = 0 <= s + i - r0 < MAX_KH
                d[a, i] = s + i - r + (MAX_KH - 1) if valid[a, i] else 0
        return valid, d
    n_tiles = ROWS // NA_QROWS
    first, mid, last = one(0), one(1), one(n_tiles - 1)
    for j in range(1, n_tiles - 1):
        v, d = one(j)
        assert (v == mid[0]).all() and (d == mid[1]).all()
    return np.stack([first[0], mid[0], last[0]]), np.stack([first[1], mid[1], last[1]])


_NA_VALID, _NA_DROW = _na_tables()


def _na_bias_tables(rpb_l):
    col = np.arange(GRID_W)
    c0 = np.clip(col - KW // 2, 0, GRID_W - KW)
    col_ok = (col[None, :] >= c0[:, None]) & (col[None, :] < c0[:, None] + KW)
    dc = np.clip(col[None, :] - col[:, None], -(KW - 1), KW - 1) + (KW - 1)
    onehot = (dc.reshape(-1)[None, :] == np.arange(2 * KW - 1)[:, None]).astype(np.float32)
    t1 = jnp.dot(rpb_l.reshape(C_HEADS * (2 * MAX_KH - 1), 2 * KW - 1), jnp.asarray(onehot),
                 precision=lax.Precision.HIGHEST).reshape(C_HEADS, 2 * MAX_KH - 1, GRID_W, GRID_W)
    t1 = jnp.where(jnp.asarray(col_ok)[None, None], t1, NEG)
    return jnp.concatenate([t1, t1], axis=-1)


def _softmax_rows(blocks):
    m = functools.reduce(jnp.maximum, [jnp.max(s, axis=-1, keepdims=True) for s in blocks])
    ps = [jnp.exp(s - m) for s in blocks]
    return ps, functools.reduce(jnp.add, [jnp.sum(p, axis=-1, keepdims=True) for p in ps])


def _assemble_bias(t_ref, bias_ref):
    lane = lax.broadcasted_iota(jnp.int32, (GRID_W, 2 * GRID_W), 1)
    neg = jnp.full((GRID_W, 2 * GRID_W), NEG, F32)
    for cls in range(3):
        for a in range(NA_QROWS):
            for pair in range(NA_KROWS // 2):
                halves = [t_ref[0, int(_NA_DROW[cls, a, i])] if _NA_VALID[cls, a, i] else neg
                          for i in (2 * pair, 2 * pair + 1)]
                tile = halves[0] if halves[0] is halves[1] else jnp.where(lane < GRID_W, halves[0], halves[1])
                bias_ref[cls, a * GRID_W:(a + 1) * GRID_W, pair * 2 * GRID_W:(pair + 1) * 2 * GRID_W] = tile


def _natten_kernel(q_ref, kl_ref, vl_ref, kc_ref, vc_ref, t_ref, g_ref, o_ref, bias_ref):
    nt = (((1,), (1,)), ((), ()))
    n_tiles = ROWS // NA_QROWS
    _assemble_bias(t_ref, bias_ref)

    def tile(j, carry):
        rows = pl.ds(pl.multiple_of(j * Q_TILE, Q_TILE), Q_TILE)
        strip_row = jnp.clip(j * NA_QROWS - MAX_KH // 2, 0, ROWS - NA_KROWS)
        strip = pl.ds(pl.multiple_of(strip_row * GRID_W, GRID_W), NA_KEYS)
        cls = jnp.where(j == 0, 0, jnp.where(j == n_tiles - 1, 2, 1))
        q = q_ref[rows, :]
        s_nb = lax.dot_general(q, kl_ref[strip, :], nt, preferred_element_type=F32) * C_SCALE + bias_ref[cls]
        s_cx = lax.dot_general(q, kc_ref[...], nt, preferred_element_type=F32) * C_SCALE
        (p_nb, p_cx), l = _softmax_rows([s_nb, s_cx])
        o = jnp.dot(p_nb.astype(BF16), vl_ref[strip, :], preferred_element_type=F32)
        o += jnp.dot(p_cx.astype(BF16), vc_ref[...], preferred_element_type=F32)
        o_ref[rows, :] = (o / l * _silu(g_ref[rows, :].astype(F32))).astype(BF16)
        return carry

    lax.fori_loop(0, n_tiles, tile, 0, unroll=2)


def _natten_ctx_kernel(q_ref, kc_ref, vc_ref, g_ref, o_ref):
    s =lax.dot_general(q_ref[...], kc_ref[...], (((1,), (1,)), ((), ())), preferred_element_type=F32) * C_SCALE
    (p,), l = _softmax_rows([s])
    o = jnp.dot(p.astype(BF16), vc_ref[...], preferred_element_type=F32)
    o_ref[...] = (o / l * _silu(g_ref[...].astype(F32))).astype(BF16)


def _natten(z, bias_tab, with_ctx_queries):
    ctx_tile = N_LAT // CTX_LEN
    hd = HEAD_DIM
    est = (2 * (5 * SEQ * hd * 2 + 2 * CTX_LEN * hd * 2 + 3 * Q_TILE * NA_KEYS * 4)
           + 12 * Q_TILE * (NA_KEYS + CTX_LEN) * 4)
    mc = pl.pallas_call(
        _natten_kernel,
        grid=(BATCH, C_HEADS),
        in_specs=[pl.BlockSpec((SEQ, hd), lambda b, h: (b, OFF_QC // hd + h)),
                  pl.BlockSpec((SEQ, hd), lambda b, h: (b, OFF_KC // hd + h)),
                  pl.BlockSpec((SEQ, hd), lambda b, h: (b, OFF_VC // hd + h)),
                  pl.BlockSpec((CTX_LEN, hd), lambda b, h: (ctx_tile + b, OFF_KC // hd + h)),
                  pl.BlockSpec((CTX_LEN, hd), lambda b, h: (ctx_tile + b, OFF_VC // hd + h)),
                  pl.BlockSpec((1, 2 * MAX_KH - 1, GRID_W, 2 * GRID_W), lambda b, h: (h, 0, 0, 0)),
                  pl.BlockSpec((SEQ, hd), lambda b, h: (b, OFF_GC // hd + h))],
        out_specs=pl.BlockSpec((SEQ, hd), lambda b, h: (b, h)),
        out_shape=jax.ShapeDtypeStruct((N_LAT, W_C), BF16),
        scratch_shapes=[pltpu.VMEM((3, Q_TILE, NA_KEYS), F32)],
        compiler_params=pltpu.CompilerParams(dimension_semantics=("parallel", "parallel"),
                                             vmem_limit_bytes=_vmem_limit(est)),
    )(z, z, z, z, z, bias_tab, z)
    if not with_ctx_queries:
        return mc, None
    mc_ctx = pl.pallas_call(
        _natten_ctx_kernel,
        grid=(BATCH, C_HEADS),
        in_specs=[pl.BlockSpec((CTX_LEN, hd), lambda b, h: (ctx_tile + b, OFF_QC // hd + h)),
                  pl.BlockSpec((CTX_LEN, hd), lambda b, h: (ctx_tile + b, OFF_KC // hd + h)),
                  pl.BlockSpec((CTX_LEN, hd), lambda b, h: (ctx_tile + b, OFF_VC // hd + h)),
                  pl.BlockSpec((CTX_LEN, hd), lambda b, h: (ctx_tile + b, OFF_GC // hd + h))],
        out_specs=pl.BlockSpec((CTX_LEN, hd), lambda b, h: (b, h)),
        out_shape=jax.ShapeDtypeStruct((N_CTX, W_C), BF16),
        compiler_params=pltpu.CompilerParams(dimension_semantics=("parallel", "parallel")),
    )(z, z, z, z)
    return mc, mc_ctx


OUT_CHUNK = 512


def _outproj_body(ma, mb, mc, x, w_ref, mod_ref, o_ref):
    gate = mod_ref[0, :, 2 * D_MODEL:3 * D_MODEL]
    for c0 in range(0, D_MODEL, OUT_CHUNK):
        c1 = c0 + OUT_CHUNK
        y = jnp.dot(ma, w_ref[0:W_A, c0:c1], preferred_element_type=F32)
        y += jnp.dot(mb, w_ref[W_A:W_A + W_B, c0:c1], preferred_element_type=F32)
        y += jnp.dot(mc, w_ref[W_A + W_B:D_MODEL, c0:c1], preferred_element_type=F32)
        o_ref[:, c0:c1] = x[:, c0:c1] + gate[:, c0:c1] * y


def _outproj_mid_kernel(ma_ref, mbl_ref, mbc_ref, mcl_ref, mcc_ref, xl_ref, xc_ref, w_ref, mod_ref, o_ref):
    _outproj_body(ma_ref[...], _pick_rows(mbl_ref, mbc_ref), _pick_rows(mcl_ref, mcc_ref),
                  _pick_rows(xl_ref, xc_ref), w_ref, mod_ref, o_ref)


def _outproj_final_kernel(ma_ref, mb_ref, mc_ref, x_ref, w_ref, mod_ref, fg_ref, o_ref):
    _outproj_body(ma_ref[...], mb_ref[...], mc_ref[...], x_ref[...], w_ref, mod_ref, o_ref)
    o_ref[...] = _rms(o_ref[...], fg_ref[...])


def _outproj_mid(ma, mb, mb_ctx, mc, mc_ctx, x_lat, x_ctx, ctx_block, w_out_b, mods_l):
    tm = ROW_TILE
    est = (3 * tm * D_MODEL * 2 + D_MODEL * D_MODEL * 2 + 5 * tm * D_MODEL * 4 + 3 * tm * OUT_CHUNK * 4)
    lat = lambda w: pl.BlockSpec((tm, w), lambda t: (_lat_tile(t), 0))
    ctx = lambda w, blk: pl.BlockSpec((tm, w), lambda t: (blk, 0), pipeline_mode=pl.Buffered(1))
    return pl.pallas_call(
        _outproj_mid_kernel,
        grid=(ROW_TILES,),
        in_specs=[pl.BlockSpec((tm, W_A), lambda t: (t, 0)),
                  lat(W_B), ctx(W_B, 0), lat(W_C), ctx(W_C, 0), lat(D_MODEL), ctx(D_MODEL, ctx_block),
                  pl.BlockSpec((D_MODEL, D_MODEL), lambda t: (0, 0), pipeline_mode=pl.Buffered(1)),
                  pl.BlockSpec((1, 1, 3 * D_MODEL), lambda t: (_mod_row(t), 0, 0))],
        out_specs=pl.BlockSpec((tm, D_MODEL), lambda t: (t, 0)),
        out_shape=jax.ShapeDtypeStruct((N_TOK, D_MODEL), F32),
        compiler_params=pltpu.CompilerParams(dimension_semantics=("parallel",),
                                             vmem_limit_bytes=_vmem_limit(est)),
    )(ma, mb, mb_ctx, mc, mc_ctx, x_lat, x_ctx, w_out_b, mods_l.reshape(MOD_ROWS, 1, 3 * D_MODEL))


def _outproj_final(ma, mb, mc, x_lat, w_out_b, mods_l, final_g):
    tm = ROW_TILE
    est = (2 * tm * D_MODEL * 2 + D_MODEL * D_MODEL * 2 + 4 * tm * D_MODEL * 4 + 3 * tm * OUT_CHUNK * 4)
    return pl.pallas_call(
        _outproj_final_kernel,
        grid=(LAT_TILES,),
        in_specs=[pl.BlockSpec((tm, W_A), lambda t: (t, 0)),
                  pl.BlockSpec((tm, W_B), lambda t: (t, 0)),
                  pl.BlockSpec((tm, W_C), lambda t: (t, 0)),
                  pl.BlockSpec((tm, D_MODEL), lambda t: (t, 0)),
                  pl.BlockSpec((D_MODEL, D_MODEL), lambda t: (0, 0), pipeline_mode=pl.Buffered(1)),
                  pl.BlockSpec((1, 1, 3 * D_MODEL), lambda t: (_mod_row(t), 0, 0)),
                  pl.BlockSpec((1, D_MODEL), lambda t: (0, 0))],
        out_specs=pl.BlockSpec((tm, D_MODEL), lambda t: (t, 0)),
        out_shape=jax.ShapeDtypeStruct((N_LAT, D_MODEL), F32),
        compiler_params=pltpu.CompilerParams(dimension_semantics=("parallel",),
                                             vmem_limit_bytes=_vmem_limit(est)),
    )(ma, mb, mc, x_lat, w_out_b, mods_l.reshape(MOD_ROWS, 1, 3 * D_MODEL), final_g.reshape(1, D_MODEL))


def _cols(w, name):
    a, b = _SRC[name]
    return w[:, a:b]


W_PREP_ROWS = 256


def _prep_w_in_kernel(w_ref, kr_ref, o_ref):
    h0, h1 = _SRC["u"][0], _SRC["ckv"][1]
    t0, t1 = _SRC["gb"][0], _SRC["gc"][1]
    o_ref[:, 0:h1 - h0] = w_ref[0, :, h0:h1].astype(BF16)
    o_ref[:, OFF_GB:OFF_KR] = w_ref[0, :, t0:t1].astype(BF16)
    o_ref[:, OFF_KR:Z_W] = kr_ref[...]


def _prep_w_in(w_in, l):
    assert _SRC["ckv"][1] == OFF_GB and OFF_GB + (_SRC["gc"][1] - _SRC["gb"][0]) == OFF_KR
    kr = w_in[l, :, _SRC["kr"][0]:_SRC["kr"][1]]
    kr_sw = jnp.concatenate([kr[:, _Q4:2 * _Q4], kr[:, 0:_Q4], kr[:, 3 * _Q4:4 * _Q4], kr[:, 2 * _Q4:3 * _Q4]], axis=1)
    kr4 = jnp.concatenate([kr, kr, kr_sw, kr_sw], axis=1).astype(BF16)
    rows = W_PREP_ROWS
    est = 2 * rows * (IN_W * 4 + Z_W * 2 + 4 * ROPE_DIM * 2) + 2 * rows * IN_W * 4
    return pl.pallas_call(
        _prep_w_in_kernel,
        grid=(D_MODEL // rows,),
        in_specs=[pl.BlockSpec((1, rows, IN_W), lambda i: (l, i, 0)),
                  pl.BlockSpec((rows, 4 * ROPE_DIM), lambda i: (i, 0))],
        out_specs=pl.BlockSpec((rows, Z_W), lambda i: (i, 0)),
        out_shape=jax.ShapeDtypeStruct((D_MODEL, Z_W), BF16),
        compiler_params=pltpu.CompilerParams(dimension_semantics=("parallel",),
                                             vmem_limit_bytes=_vmem_limit(est)),
    )(w_in, kr4)


def _cast_kernel(w_ref, o_ref):
    o_ref[...] = w_ref[0].astype(BF16)


def _prep_w_out(w_out, l):
    rows = W_PREP_ROWS
    return pl.pallas_call(
        _cast_kernel,
        grid=(MIX_W // rows,),
        in_specs=[pl.BlockSpec((1, rows, D_MODEL), lambda i: (l, i, 0))],
        out_specs=pl.BlockSpec((rows, D_MODEL), lambda i: (i, 0)),
        out_shape=jax.ShapeDtypeStruct((MIX_W, D_MODEL), BF16),
        compiler_params=pltpu.CompilerParams(dimension_semantics=("parallel",)),
    )(w_out)


def _prep_w_uq(w):
    w3 = w.reshape(Q_LORA, B_HEADS, NOPE_DIM + ROPE_DIM)
    nope = w3[:, :, :NOPE_DIM].reshape(Q_LORA, B_HEADS * NOPE_DIM)
    rope = w3[:, :, NOPE_DIM:]
    rope_sw = jnp.concatenate([rope[..., _Q4:2 * _Q4], rope[..., 0:_Q4],
                               rope[..., 3 * _Q4:4 * _Q4], rope[..., 2 * _Q4:3 * _Q4]], axis=-1)
    return jnp.concatenate([nope, rope.reshape(Q_LORA, -1), rope_sw.reshape(Q_LORA, -1)], axis=1).astype(BF16)


def _prep_w_ukv(w):
    w3 = w.reshape(KV_LORA, B_HEADS, NOPE_DIM + V_DIM)
    return jnp.concatenate([w3[:, :, :NOPE_DIM].reshape(KV_LORA, -1),
                            w3[:, :, NOPE_DIM:].reshape(KV_LORA, -1)], axis=1).astype(BF16)


def _rope_tables():
    t = np.arange(SEQ)
    row = (t // GRID_W).astype(np.float32)
    col = (t % GRID_W).astype(np.float32)
    inv = (np.float32(ROPE_THETA) ** (-np.arange(_Q4, dtype=np.float32) / np.float32(_Q4))).astype(np.float32)
    ar, ac = row[:, None] * inv, col[:, None] * inv
    cos = np.concatenate([np.cos(ar), np.cos(ar), np.cos(ac), np.cos(ac)], axis=1)
    sin = np.concatenate([-np.sin(ar), np.sin(ar), -np.sin(ac), np.sin(ac)], axis=1)
    cos = np.concatenate([cos, np.ones((ROW_TILE, ROPE_DIM))], axis=0).astype(np.float32)
    sin = np.concatenate([sin, np.zeros((ROW_TILE, ROPE_DIM))], axis=0).astype(np.float32)
    return np.concatenate([cos, cos], axis=1), np.concatenate([sin, sin], axis=1)


_COS2, _SIN2 = _rope_tables()


def _rope_tile(t):
    return jnp.where(t < LAT_TILES, t % TILES_PER_SAMPLE, TILES_PER_SAMPLE)


def kernel(x, c, ctx, c_ctx, w_ada, b_ada, norm_g, w_in, qa_g, kva_g, w_uq, w_ukv, sgu_g, w_s, b_s, rpb,
           w_out, final_g):
    assert x.shape == (BATCH, SEQ, D_MODEL) and ctx.shape == (BATCH, CTX_LEN, D_MODEL)
    assert w_in.shape == (DEPTH, D_MODEL, IN_W)
    cc = jnp.concatenate([c, c_ctx[None, :], jnp.zeros((MOD_ROWS - BATCH - 1, D_MODEL), F32)], axis=0)
    mods = _modulation(cc, w_ada, b_ada)
    cos2, sin2 = jnp.asarray(_COS2), jnp.asarray(_SIN2)

    x_lat, x_ctx, ctx_block = x.reshape(N_LAT, D_MODEL), ctx.reshape(N_CTX, D_MODEL), 0
    for l in range(DEPTH):
        last = l == DEPTH - 1
        z = _inproj(x_lat, x_ctx, ctx_block, mods[l], norm_g[l], _prep_w_in(w_in, l))
        q, k, vt = _upproj(z, cos2, sin2, qa_g[l], kva_g[l], _prep_w_uq(w_uq[l]), _prep_w_ukv(w_ukv[l]))
        b_s_exp = jnp.repeat(b_s[l].T, HEAD_DIM, axis=1)
        ma = _gmlp(z, sgu_g[l], w_s[l].astype(BF16), b_s_exp, LAT_TILES if last else ROW_TILES)
        mb, mb_ctx = _mla(q, k, vt, z, with_ctx_queries=not last)
        mc, mc_ctx = _natten(z, _na_bias_tables(rpb[l]), with_ctx_queries=not last)
        w_out_b = _prep_w_out(w_out, l)
        if last:
            out = _outproj_final(ma, mb, mc, x_lat, w_out_b, mods[l], final_g)
            return out.reshape(BATCH, SEQ, D_MODEL)
        xf = _outproj_mid(ma, mb, mb_ctx, mc, mc_ctx, x_lat, x_ctx, ctx_block, w_out_b, mods[l])
        x_lat, x_ctx, ctx_block = xf, xf, LAT_TILES
```

```python
import functools
import math

import numpy as np
import jax
import jax.numpy as jnp
from jax import lax
from jax.experimental import pallas as pl
from jax.experimental.pallas import tpu as pltpu

D_MODEL = 2048
BATCH = 2
SEQ = 4096
DEPTH = 2
GRID_W = 64
CTX_LEN = 256
EPS = 1e-6
NEG = -1e30
HEAD_DIM = 128
W_A = D_MODEL // 4
W_B = D_MODEL // 2
W_C = D_MODEL // 4
MIX_W = W_A + W_B + W_C
CHUNK = 128
A_GROUPS = W_A // HEAD_DIM
B_HEADS = W_B // HEAD_DIM
Q_LORA = D_MODEL // 4
KV_LORA = 512
NOPE_DIM = 128
ROPE_DIM = 64
V_DIM = 128
MLA_SCALE = (NOPE_DIM + ROPE_DIM) ** -0.5
ROPE_THETA = 10000.0
C_HEADS = W_C // HEAD_DIM
MAX_KH = 8
KW = 16
C_SCALE = HEAD_DIM ** -0.5
ROWS = SEQ // GRID_W

LANES = 128
MXU_DIM = 256
VMEM_BYTES_V7X = 64 * 1024 * 1024
VMEM_LIMIT_CAP = 56 * 1024 * 1024

F32 = jnp.float32
BF16 = jnp.bfloat16

N_LAT = BATCH * SEQ
N_CTX = BATCH * CTX_LEN
N_TOK = N_LAT + N_CTX
ROW_TILE = N_CTX
LAT_TILES = N_LAT // ROW_TILE
ROW_TILES = N_TOK // ROW_TILE
TILES_PER_SAMPLE = SEQ // ROW_TILE
Q_TILE = CTX_LEN
Q_TILES_PER_SAMPLE = SEQ // Q_TILE
KV_CHUNK = 256
assert ROW_TILE % CHUNK == 0 and SEQ % ROW_TILE == 0 and SEQ % Q_TILE == 0

OFF_U = 0
OFF_VA = OFF_U + W_A
OFF_GA = OFF_VA + W_A
OFF_CQ = OFF_GA + W_A
OFF_CKV = OFF_CQ + Q_LORA
OFF_GB = OFF_CKV + KV_LORA
OFF_QC = OFF_GB + W_B
OFF_KC = OFF_QC + W_C
OFF_VC = OFF_KC + W_C
OFF_GC = OFF_VC + W_C
OFF_KR = OFF_GC + W_C
Z_W = OFF_KR + 4 * ROPE_DIM
Z_CHUNK = 512
assert Z_W % MXU_DIM == 0 and OFF_KR % (4 * ROPE_DIM) == 0

_SRC = {}
_acc = 0
for _name, _w in (("u", W_A), ("va", W_A), ("ga", W_A), ("cq", Q_LORA), ("ckv", KV_LORA), ("kr", ROPE_DIM),
                  ("gb", W_B), ("qc", W_C), ("kc", W_C), ("vc", W_C), ("gc", W_C)):
    _SRC[_name] = (_acc, _acc + _w)
    _acc += _w
IN_W = _acc

_Q4 = ROPE_DIM // 4
_SWAP = np.concatenate([np.arange(_Q4, 2 * _Q4), np.arange(0, _Q4),
                        np.arange(3 * _Q4, 4 * _Q4), np.arange(2 * _Q4, 3 * _Q4)])


def _vmem_limit(nbytes):
    return int(min(VMEM_LIMIT_CAP, max(16 * 1024 * 1024, nbytes * 5 // 4)))


def _silu(x):
    return x * jax.nn.sigmoid(x)


def _rms(x, g):
    return x * lax.rsqrt(jnp.mean(x * x, axis=-1, keepdims=True) + EPS) * g


MOD_ROWS = 8
MOD_TN = 768


def _mod_kernel(c_ref, w_ref, b_ref, o_ref):
    a = _silu(c_ref[...])
    a_hi = a.astype(BF16)
    a_lo = (a - a_hi.astype(F32)).astype(BF16)
    w = w_ref[0]
    w_hi = w.astype(BF16)
    w_lo = (w - w_hi.astype(F32)).astype(BF16)
    r = jnp.dot(jnp.concatenate([a_hi, a_lo], axis=0), w_hi, preferred_element_type=F32)
    r_lo = jnp.dot(a_hi, w_lo, preferred_element_type=F32)
    o_ref[0] = r[0:MOD_ROWS] + r[MOD_ROWS:2 * MOD_ROWS] + r_lo + b_ref[0]


def _modulation(cc, w_ada, b_ada):
    n = 3 * D_MODEL
    est = 2 * (MOD_ROWS * D_MODEL * 4 + D_MODEL * MOD_TN * 4 + 2 * MOD_ROWS * MOD_TN * 4)
    return pl.pallas_call(
        _mod_kernel,
        grid=(DEPTH, n // MOD_TN),
        in_specs=[pl.BlockSpec((MOD_ROWS, D_MODEL), lambda l, j: (0, 0)),
                  pl.BlockSpec((1, D_MODEL, MOD_TN), lambda l, j: (l, 0, j)),
                  pl.BlockSpec((1, 1, MOD_TN), lambda l, j: (l, 0, j))],
        out_specs=pl.BlockSpec((1, MOD_ROWS, MOD_TN), lambda l, j: (l, 0, j)),
        out_shape=jax.ShapeDtypeStruct((DEPTH, MOD_ROWS, n), F32),
        compiler_params=pltpu.CompilerParams(dimension_semantics=("parallel", "parallel"),
                                             vmem_limit_bytes=_vmem_limit(est)),
    )(cc, w_ada, b_ada.reshape(DEPTH, 1, n))


def _mod_row(t):
    return jnp.where(t < LAT_TILES, t // TILES_PER_SAMPLE, BATCH)


def _lat_tile(t):
    return jnp.minimum(t, LAT_TILES - 1)


def _pick_rows(lat_ref, ctx_ref):
    return jnp.where(pl.program_id(0) < LAT_TILES, lat_ref[...], ctx_ref[...])


def _inproj_kernel(xl_ref, xc_ref, mod_ref, g_ref, w_ref, o_ref, h_ref):
    x = _pick_rows(xl_ref, xc_ref)
    shift = mod_ref[0, :, 0:D_MODEL]
    scale = mod_ref[0, :, D_MODEL:2 * D_MODEL]
    h_ref[...] = (_rms(x, g_ref[...]) * (1.0 + scale) + shift).astype(BF16)
    for c0 in range(0, Z_W, Z_CHUNK):
        c1 = min(c0 + Z_CHUNK, Z_W)
        o_ref[:, c0:c1] = jnp.dot(h_ref[...], w_ref[:, c0:c1], preferred_element_type=F32).astype(BF16)


def _inproj(x_lat, x_ctx, ctx_block, mods_l, norm_g, w_in_p):
    est = (3 * ROW_TILE * D_MODEL * 4 + D_MODEL * Z_W * 2 + 2 * ROW_TILE * Z_W * 2 + ROW_TILE * D_MODEL * 2
           + 2 * ROW_TILE * Z_CHUNK * 4)
    return pl.pallas_call(
        _inproj_kernel,
        grid=(ROW_TILES,),
        in_specs=[pl.BlockSpec((ROW_TILE, D_MODEL), lambda t: (_lat_tile(t), 0)),
                  pl.BlockSpec((ROW_TILE, D_MODEL), lambda t: (ctx_block, 0), pipeline_mode=pl.Buffered(1)),
                  pl.BlockSpec((1, 1, 3 * D_MODEL), lambda t: (_mod_row(t), 0, 0)),
                  pl.BlockSpec((1, D_MODEL), lambda t: (0, 0)),
                  pl.BlockSpec((D_MODEL, Z_W), lambda t: (0, 0), pipeline_mode=pl.Buffered(1))],
        out_specs=pl.BlockSpec((ROW_TILE, Z_W), lambda t: (t, 0)),
        out_shape=jax.ShapeDtypeStruct((N_TOK, Z_W), BF16),
        scratch_shapes=[pltpu.VMEM((ROW_TILE, D_MODEL), BF16)],
        compiler_params=pltpu.CompilerParams(dimension_semantics=("parallel",),
                                             vmem_limit_bytes=_vmem_limit(est)),
    )(x_lat, x_ctx, mods_l.reshape(MOD_ROWS, 1, 3 * D_MODEL), norm_g.reshape(1, D_MODEL), w_in_p)


QK_W = 2 * HEAD_DIM
UQ_W = B_HEADS * (NOPE_DIM + 2 * ROPE_DIM)
UKV_W = B_HEADS * (NOPE_DIM + V_DIM)
_Q_PRESCALE = MLA_SCALE * math.log2(math.e)
assert B_HEADS % 2 == 0 and 2 * ROPE_DIM == LANES


def _upproj_kernel(cq_ref, ckv_ref, kr_ref, cos_ref, sin_ref, qg_ref, kg_ref, wq_ref, wkv_ref,
                   q_ref, k_ref, vt_ref):
    cos2 = cos_ref[...]
    sin2 = sin_ref[...]
    n_rope = B_HEADS * ROPE_DIM
    cqn = _rms(cq_ref[...].astype(F32), qg_ref[...]).astype(BF16)
    q_all = jnp.dot(cqn, wq_ref[...], preferred_element_type=F32)
    rope0 = B_HEADS * NOPE_DIM
    for h in range(B_HEADS):
        q_ref[h, :, 0:NOPE_DIM] = (q_all[:, h * NOPE_DIM:(h + 1) * NOPE_DIM] * _Q_PRESCALE).astype(BF16)
    for j in range(B_HEADS // 2):
        a = q_all[:, rope0 + j * LANES: rope0 + (j + 1) * LANES]
        a_sw = q_all[:, rope0 + n_rope + j * LANES: rope0 + n_rope + (j + 1) * LANES]
        rot = ((a * cos2 + a_sw * sin2) * _Q_PRESCALE).astype(BF16)
        q_ref[2 * j, :, NOPE_DIM:QK_W] = rot
        q_ref[2 * j + 1, :, NOPE_DIM:QK_W] = rot

    ckvn = _rms(ckv_ref[...].astype(F32), kg_ref[...]).astype(BF16)
    kv_all = jnp.dot(ckvn, wkv_ref[...], preferred_element_type=F32)
    kr = kr_ref[...].astype(F32)
    krot = kr[:, 0:LANES] * cos2 + kr[:, LANES:2 * LANES] * sin2
    lane = lax.broadcasted_iota(jnp.int32, krot.shape, 1)
    k_lo = jnp.where(lane < ROPE_DIM, krot, 0.0).astype(BF16)
    k_hi = jnp.where(lane >= ROPE_DIM, krot, 0.0).astype(BF16)
    v0 = B_HEADS * NOPE_DIM
    for h in range(B_HEADS):
        k_ref[h, :, 0:NOPE_DIM] = kv_all[:, h * NOPE_DIM:(h + 1) * NOPE_DIM].astype(BF16)
        k_ref[h, :, NOPE_DIM:QK_W] = k_lo if h % 2 == 0 else k_hi
        v_h = kv_all[:, v0 + h * V_DIM: v0 + (h + 1) * V_DIM]
        for c in range(ROW_TILE // KV_CHUNK):
            vt_ref[h, c] = v_h[c * KV_CHUNK:(c + 1) * KV_CHUNK, :].T.astype(BF16)


def _upproj(z, cos2, sin2, qa_g, kva_g, w_uq_p, w_ukv_p):
    tm = ROW_TILE
    est = 2 * (2 * tm * Q_LORA * 2 + tm * 4 * ROPE_DIM * 2 + 2 * tm * LANES * 4 + Q_LORA * UQ_W * 2
               + KV_LORA * UKV_W * 2 + B_HEADS * tm * (2 * QK_W + V_DIM) * 2) + 4 * tm * UQ_W * 4
    return pl.pallas_call(
        _upproj_kernel,
        grid=(ROW_TILES,),
        in_specs=[pl.BlockSpec((tm, Q_LORA), lambda t: (t, OFF_CQ // Q_LORA)),
                  pl.BlockSpec((tm, KV_LORA), lambda t: (t, OFF_CKV // KV_LORA)),
                  pl.BlockSpec((tm, 4 * ROPE_DIM), lambda t: (t, OFF_KR // (4 * ROPE_DIM))),
                  pl.BlockSpec((tm, LANES), lambda t: (_rope_tile(t), 0)),
                  pl.BlockSpec((tm, LANES), lambda t: (_rope_tile(t), 0)),
                  pl.BlockSpec((1, Q_LORA), lambda t: (0, 0)),
                  pl.BlockSpec((1, KV_LORA), lambda t: (0, 0)),
                  pl.BlockSpec((Q_LORA, UQ_W), lambda t: (0, 0)),
                  pl.BlockSpec((KV_LORA, UKV_W), lambda t: (0, 0))],
        out_specs=[pl.BlockSpec((B_HEADS, tm, QK_W), lambda t: (0, t, 0)),
                   pl.BlockSpec((B_HEADS, tm, QK_W), lambda t: (0, t, 0)),
                   pl.BlockSpec((B_HEADS, tm // KV_CHUNK, V_DIM, KV_CHUNK), lambda t: (0, t, 0, 0))],
        out_shape=[jax.ShapeDtypeStruct((B_HEADS, N_TOK, QK_W), BF16),
                   jax.ShapeDtypeStruct((B_HEADS, N_TOK, QK_W), BF16),
                   jax.ShapeDtypeStruct((B_HEADS, N_TOK // KV_CHUNK, V_DIM, KV_CHUNK), BF16)],
        compiler_params=pltpu.CompilerParams(dimension_semantics=("parallel",),
                                             vmem_limit_bytes=_vmem_limit(est)),
    )(z, z, z, cos2, sin2, qa_g.reshape(1, Q_LORA), kva_g.reshape(1, KV_LORA), w_uq_p, w_ukv_p)


def _gmlp_kernel(u_ref, v_ref, g_ref, sg_ref, ws_ref, bs_ref, o_ref):
    v = jax.nn.gelu(v_ref[...].astype(F32))
    vn = _rms(v, sg_ref[...]).astype(BF16)
    front = jax.nn.gelu(u_ref[...].astype(F32))
    gate = _silu(g_ref[...].astype(F32))
    for c in range(ROW_TILE // CHUNK):
        r0, r1 = c * CHUNK, (c + 1) * CHUNK
        for g in range(A_GROUPS):
            c0, c1 = g * HEAD_DIM, (g + 1) * HEAD_DIM
            s = jnp.dot(ws_ref[g], vn[r0:r1, c0:c1], preferred_element_type=F32) + bs_ref[:, c0:c1]
            o_ref[r0:r1, c0:c1] = (front[r0:r1, c0:c1] * s * gate[r0:r1, c0:c1]).astype(BF16)


def _gmlp(z, sgu_g, w_s_b, b_s_exp, n_tiles):
    tm = ROW_TILE
    est = 2 * (4 * tm * W_A * 2 + A_GROUPS * CHUNK * CHUNK * 2 + CHUNK * W_A * 4) + 5 * tm * W_A * 4
    return pl.pallas_call(
        _gmlp_kernel,
        grid=(n_tiles,),
        in_specs=[pl.BlockSpec((tm, W_A), lambda t: (t, OFF_U // W_A)),
                  pl.BlockSpec((tm, W_A), lambda t: (t, OFF_VA // W_A)),
                  pl.BlockSpec((tm, W_A), lambda t: (t, OFF_GA // W_A)),
                  pl.BlockSpec((1, W_A), lambda t: (0, 0)),
                  pl.BlockSpec((A_GROUPS, CHUNK, CHUNK), lambda t: (0, 0, 0)),
                  pl.BlockSpec((CHUNK, W_A), lambda t: (0, 0))],
        out_specs=pl.BlockSpec((tm, W_A), lambda t: (t, 0)),
        out_shape=jax.ShapeDtypeStruct((n_tiles * tm, W_A), BF16),
        compiler_params=pltpu.CompilerParams(dimension_semantics=("parallel",),
                                             vmem_limit_bytes=_vmem_limit(est)),
    )(z, z, z, sgu_g.reshape(1, W_A), w_s_b, b_s_exp)


MLA_TK = 512
MLA_TQ = 1024
MLA_TQG = 256
assert SEQ % MLA_TK == 0 and MLA_TK % KV_CHUNK == 0 and SEQ % MLA_TQ == 0


def _attend_t(qs, chunks):
    def scores(k, q):
        return lax.dot_general(k, q, (((1,), (1,)), ((), ())), preferred_element_type=F32)

    state = [None] * len(qs)
    s_next = [scores(chunks[0][0], q) for q in qs]
    for j, (_, vts) in enumerate(chunks):
        for g, q in enumerate(qs):
            s = s_next[g]
            if j + 1 < len(chunks):
                s_next[g] = scores(chunks[j + 1][0], q)
            s_max = jnp.max(s, axis=0, keepdims=True)
            m_new = s_max if state[g] is None else jnp.maximum(state[g][0], s_max)
            p_sum = pv = None
            for n, vt in enumerate(vts):
                p = jnp.exp2(s[n * KV_CHUNK:(n + 1) * KV_CHUNK] - m_new)
                ps = jnp.sum(p, axis=0, keepdims=True)
                d = jnp.dot(vt, p.astype(BF16), preferred_element_type=F32)
                p_sum, pv = (ps, d) if n == 0 else (p_sum + ps, pv + d)
            if state[g] is None:
                state[g] = (m_new, p_sum, pv)
            else:
                m, l, acc = state[g]
                alpha = jnp.exp2(m - m_new)
                state[g] = (m_new, alpha * l + p_sum, alpha * acc + pv)
    return [acc / l for _, l, acc in state]


def _mla_lat_kernel(q_ref, kl_ref, vtl_ref, kc_ref, vtc_ref, gb_ref, o_ref):
    per = MLA_TK // KV_CHUNK
    chunks = [(kl_ref[0, j * MLA_TK:(j + 1) * MLA_TK, :], [vtl_ref[0, j * per + n] for n in range(per)])
              for j in range(SEQ // MLA_TK)]
    chunks.append((kc_ref[0], [vtc_ref[0, n] for n in range(CTX_LEN // KV_CHUNK)]))
    groups = [(g * MLA_TQG, (g + 1) * MLA_TQG) for g in range(MLA_TQ // MLA_TQG)]
    outs = _attend_t([q_ref[0, lo:hi, :] for lo, hi in groups], chunks)
    for (lo, hi), o_t in zip(groups, outs):
        o_ref[lo:hi, :] = (o_t.T * _silu(gb_ref[lo:hi, :].astype(F32))).astype(BF16)


def _mla_ctx_kernel(q_ref, kc_ref, vtc_ref, gb_ref, o_ref):
    (o_t,) = _attend_t([q_ref[0]], [(kc_ref[0], [vtc_ref[0, n] for n in range(CTX_LEN // KV_CHUNK)])])
    o = o_t.T
    o_ref[...] = (o * _silu(gb_ref[...].astype(F32))).astype(BF16)


def _mla(q, k, vt, z, with_ctx_queries):
    ctx_tile = N_LAT // CTX_LEN
    tq = MLA_TQ
    nq = SEQ // tq
    est = 2 * (tq * QK_W * 2 + SEQ * QK_W * 2 + SEQ * V_DIM * 2 + CTX_LEN * (QK_W + V_DIM) * 2
               + 2 * tq * HEAD_DIM * 2) + 8 * MLA_TK * tq * 4
    mb = pl.pallas_call(
        _mla_lat_kernel,
        grid=(BATCH, B_HEADS, nq),
        in_specs=[pl.BlockSpec((1, tq, QK_W), lambda b, h, i: (h, b * nq + i, 0)),
                  pl.BlockSpec((1, SEQ, QK_W), lambda b, h, i: (h, b, 0)),
                  pl.BlockSpec((1, SEQ // KV_CHUNK, V_DIM, KV_CHUNK), lambda b, h, i: (h, b, 0, 0)),
                  pl.BlockSpec((1, CTX_LEN, QK_W), lambda b, h, i: (h, ctx_tile + b, 0)),
                  pl.BlockSpec((1, CTX_LEN // KV_CHUNK, V_DIM, KV_CHUNK), lambda b, h, i: (h, ctx_tile + b, 0, 0)),
                  pl.BlockSpec((tq, HEAD_DIM), lambda b, h, i: (b * nq + i, OFF_GB // HEAD_DIM + h))],
        out_specs=pl.BlockSpec((tq, HEAD_DIM), lambda b, h, i: (b * nq + i, h)),
        out_shape=jax.ShapeDtypeStruct((N_LAT, W_B), BF16),
        compiler_params=pltpu.CompilerParams(dimension_semantics=("parallel", "parallel", "parallel"),
                                             vmem_limit_bytes=_vmem_limit(est)),
    )(q, k, vt, k, vt, z)
    if not with_ctx_queries:
        return mb, None
    mb_ctx = pl.pallas_call(
        _mla_ctx_kernel,
        grid=(BATCH, B_HEADS),
        in_specs=[pl.BlockSpec((1, CTX_LEN, QK_W), lambda b, h: (h, ctx_tile + b, 0)),
                  pl.BlockSpec((1, CTX_LEN, QK_W), lambda b, h: (h, ctx_tile + b, 0)),
                  pl.BlockSpec((1, CTX_LEN // KV_CHUNK, V_DIM, KV_CHUNK), lambda b, h: (h, ctx_tile + b, 0, 0)),
                  pl.BlockSpec((CTX_LEN, HEAD_DIM), lambda b, h: (ctx_tile + b, OFF_GB // HEAD_DIM + h))],
        out_specs=pl.BlockSpec((CTX_LEN, HEAD_DIM), lambda b, h: (b, h)),
        out_shape=jax.ShapeDtypeStruct((N_CTX, W_B), BF16),
        compiler_params=pltpu.CompilerParams(dimension_semantics=("parallel", "parallel")),
    )(q, k, vt, z)
    return mb, mb_ctx


NA_QROWS = Q_TILE // GRID_W
NA_KROWS = 12
NA_KEYS = NA_KROWS * GRID_W
assert NA_QROWS + MAX_KH <= NA_KROWS + 1 and ROWS >= NA_KROWS


def _na_strip_row(j):
    return int(np.clip(j * NA_QROWS - MAX_KH // 2, 0, ROWS - NA_KROWS))


def _na_tables():
    def one(j):
        s = _na_strip_row(j)
        valid = np.zeros((NA_QROWS, NA_KROWS), bool)
        d = np.zeros((NA_QROWS, NA_KROWS), np.int32)
        for a in range(NA_QROWS):
            r = j * NA_QROWS + a
            r0 = int(np.clip(r - MAX_KH // 2, 0, ROWS - MAX_KH))
            for i in range(NA_KROWS):
                valid[a, i] = 0 <= s + i - r0 < MAX_KH
                d[a, i] = s + i - r + (MAX_KH - 1) if valid[a, i] else 0
        return valid, d
    n_tiles = ROWS // NA_QROWS
    first, mid, last = one(0), one(1), one(n_tiles - 1)
    for j in range(1, n_tiles - 1):
        v, d = one(j)
        assert (v == mid[0]).all() and (d == mid[1]).all()
    return np.stack([first[0], mid[0], last[0]]), np.stack([first[1], mid[1], last[1]])


_NA_VALID, _NA_DROW = _na_tables()


def _na_bias_tables(rpb_l):
    col = np.arange(GRID_W)
    c0 = np.clip(col - KW // 2, 0, GRID_W - KW)
    col_ok = (col[None, :] >= c0[:, None]) & (col[None, :] < c0[:, None] + KW)
    dc = np.clip(col[None, :] - col[:, None], -(KW - 1), KW - 1) + (KW - 1)
    onehot = (dc.reshape(-1)[None, :] == np.arange(2 * KW - 1)[:, None]).astype(np.float32)
    t1 = jnp.dot(rpb_l.reshape(C_HEADS * (2 * MAX_KH - 1), 2 * KW - 1), jnp.asarray(onehot),
                 precision=lax.Precision.HIGHEST).reshape(C_HEADS, 2 * MAX_KH - 1, GRID_W, GRID_W)
    t1 = jnp.where(jnp.asarray(col_ok)[None, None], t1, NEG)
    return jnp.concatenate([t1, t1], axis=-1)


def _softmax_rows(blocks):
    m = functools.reduce(jnp.maximum, [jnp.max(s, axis=-1, keepdims=True) for s in blocks])
    ps = [jnp.exp(s - m) for s in blocks]
    return ps, functools.reduce(jnp.add, [jnp.sum(p, axis=-1, keepdims=True) for p in ps])


def _assemble_bias(t_ref, bias_ref):
    lane = lax.broadcasted_iota(jnp.int32, (GRID_W, 2 * GRID_W), 1)
    neg = jnp.full((GRID_W, 2 * GRID_W), NEG, F32)
    for cls in range(3):
        for a in range(NA_QROWS):
            for pair in range(NA_KROWS // 2):
                halves = [t_ref[0, int(_NA_DROW[cls, a, i])] if _NA_VALID[cls, a, i] else neg
                          for i in (2 * pair, 2 * pair + 1)]
                tile = halves[0] if halves[0] is halves[1] else jnp.where(lane < GRID_W, halves[0], halves[1])
                bias_ref[cls, a * GRID_W:(a + 1) * GRID_W, pair * 2 * GRID_W:(pair + 1) * 2 * GRID_W] = tile


def _natten_kernel(q_ref, kl_ref, vl_ref, kc_ref, vc_ref, t_ref, g_ref, o_ref, bias_ref):
    nt = (((1,), (1,)), ((), ()))
    n_tiles = ROWS // NA_QROWS
    _assemble_bias(t_ref, bias_ref)

    def tile(j, carry):
        rows = pl.ds(pl.multiple_of(j * Q_TILE, Q_TILE), Q_TILE)
        strip_row = jnp.clip(j * NA_QROWS - MAX_KH // 2, 0, ROWS - NA_KROWS)
        strip = pl.ds(pl.multiple_of(strip_row * GRID_W, GRID_W), NA_KEYS)
        cls = jnp.where(j == 0, 0, jnp.where(j == n_tiles - 1, 2, 1))
        q = q_ref[rows, :]
        s_nb = lax.dot_general(q, kl_ref[strip, :], nt, preferred_element_type=F32) * C_SCALE + bias_ref[cls]
        s_cx = lax.dot_general(q, kc_ref[...], nt, preferred_element_type=F32) * C_SCALE
        (p_nb, p_cx), l = _softmax_rows([s_nb, s_cx])
        o = jnp.dot(p_nb.astype(BF16), vl_ref[strip, :], preferred_element_type=F32)
        o += jnp.dot(p_cx.astype(BF16), vc_ref[...], preferred_element_type=F32)
        o_ref[rows, :] = (o / l * _silu(g_ref[rows, :].astype(F32))).astype(BF16)
        return carry

    lax.fori_loop(0, n_tiles, tile, 0, unroll=2)


def _natten_ctx_kernel(q_ref, kc_ref, vc_ref, g_ref, o_ref):
    s =lax.dot_general(q_ref[...], kc_ref[...], (((1,), (1,)), ((), ())), preferred_element_type=F32) * C_SCALE
    (p,), l = _softmax_rows([s])
    o = jnp.dot(p.astype(BF16), vc_ref[...], preferred_element_type=F32)
    o_ref[...] = (o / l * _silu(g_ref[...].astype(F32))).astype(BF16)


def _natten(z, bias_tab, with_ctx_queries):
    ctx_tile = N_LAT // CTX_LEN
    hd = HEAD_DIM
    est = (2 * (5 * SEQ * hd * 2 + 2 * CTX_LEN * hd * 2 + 3 * Q_TILE * NA_KEYS * 4)
           + 12 * Q_TILE * (NA_KEYS + CTX_LEN) * 4)
    mc = pl.pallas_call(
        _natten_kernel,
        grid=(BATCH, C_HEADS),
        in_specs=[pl.BlockSpec((SEQ, hd), lambda b, h: (b, OFF_QC // hd + h)),
                  pl.BlockSpec((SEQ, hd), lambda b, h: (b, OFF_KC // hd + h)),
                  pl.BlockSpec((SEQ, hd), lambda b, h: (b, OFF_VC // hd + h)),
                  pl.BlockSpec((CTX_LEN, hd), lambda b, h: (ctx_tile + b, OFF_KC // hd + h)),
                  pl.BlockSpec((CTX_LEN, hd), lambda b, h: (ctx_tile + b, OFF_VC // hd + h)),
                  pl.BlockSpec((1, 2 * MAX_KH - 1, GRID_W, 2 * GRID_W), lambda b, h: (h, 0, 0, 0)),
                  pl.BlockSpec((SEQ, hd), lambda b, h: (b, OFF_GC // hd + h))],
        out_specs=pl.BlockSpec((SEQ, hd), lambda b, h: (b, h)),
        out_shape=jax.ShapeDtypeStruct((N_LAT, W_C), BF16),
        scratch_shapes=[pltpu.VMEM((3, Q_TILE, NA_KEYS), F32)],
        compiler_params=pltpu.CompilerParams(dimension_semantics=("parallel", "parallel"),
                                             vmem_limit_bytes=_vmem_limit(est)),
    )(z, z, z, z, z, bias_tab, z)
    if not with_ctx_queries:
        return mc, None
    mc_ctx = pl.pallas_call(
        _natten_ctx_kernel,
        grid=(BATCH, C_HEADS),
        in_specs=[pl.BlockSpec((CTX_LEN, hd), lambda b, h: (ctx_tile + b, OFF_QC // hd + h)),
                  pl.BlockSpec((CTX_LEN, hd), lambda b, h: (ctx_tile + b, OFF_KC // hd + h)),
                  pl.BlockSpec((CTX_LEN, hd), lambda b, h: (ctx_tile + b, OFF_VC // hd + h)),
                  pl.BlockSpec((CTX_LEN, hd), lambda b, h: (ctx_tile + b, OFF_GC // hd + h))],
        out_specs=pl.BlockSpec((CTX_LEN, hd), lambda b, h: (b, h)),
        out_shape=jax.ShapeDtypeStruct((N_CTX, W_C), BF16),
        compiler_params=pltpu.CompilerParams(dimension_semantics=("parallel", "parallel")),
    )(z, z, z, z)
    return mc, mc_ctx


OUT_CHUNK = 512


def _outproj_body(ma, mb, mc, x, w_ref, mod_ref, o_ref):
    gate = mod_ref[0, :, 2 * D_MODEL:3 * D_MODEL]
    for c0 in range(0, D_MODEL, OUT_CHUNK):
        c1 = c0 + OUT_CHUNK
        y = jnp.dot(ma, w_ref[0:W_A, c0:c1], preferred_element_type=F32)
        y += jnp.dot(mb, w_ref[W_A:W_A + W_B, c0:c1], preferred_element_type=F32)
        y += jnp.dot(mc, w_ref[W_A + W_B:D_MODEL, c0:c1], preferred_element_type=F32)
        o_ref[:, c0:c1] = x[:, c0:c1] + gate[:, c0:c1] * y


def _outproj_mid_kernel(ma_ref, mbl_ref, mbc_ref, mcl_ref, mcc_ref, xl_ref, xc_ref, w_ref, mod_ref, o_ref):
    _outproj_body(ma_ref[...], _pick_rows(mbl_ref, mbc_ref), _pick_rows(mcl_ref, mcc_ref),
                  _pick_rows(xl_ref, xc_ref), w_ref, mod_ref, o_ref)


def _outproj_final_kernel(ma_ref, mb_ref, mc_ref, x_ref, w_ref, mod_ref, fg_ref, o_ref):
    _outproj_body(ma_ref[...], mb_ref[...], mc_ref[...], x_ref[...], w_ref, mod_ref, o_ref)
    o_ref[...] = _rms(o_ref[...], fg_ref[...])


def _outproj_mid(ma, mb, mb_ctx, mc, mc_ctx, x_lat, x_ctx, ctx_block, w_out_b, mods_l):
    tm = ROW_TILE
    est = (3 * tm * D_MODEL * 2 + D_MODEL * D_MODEL * 2 + 5 * tm * D_MODEL * 4 + 3 * tm * OUT_CHUNK * 4)
    lat = lambda w: pl.BlockSpec((tm, w), lambda t: (_lat_tile(t), 0))
    ctx = lambda w, blk: pl.BlockSpec((tm, w), lambda t: (blk, 0), pipeline_mode=pl.Buffered(1))
    return pl.pallas_call(
        _outproj_mid_kernel,
        grid=(ROW_TILES,),
        in_specs=[pl.BlockSpec((tm, W_A), lambda t: (t, 0)),
                  lat(W_B), ctx(W_B, 0), lat(W_C), ctx(W_C, 0), lat(D_MODEL), ctx(D_MODEL, ctx_block),
                  pl.BlockSpec((D_MODEL, D_MODEL), lambda t: (0, 0), pipeline_mode=pl.Buffered(1)),
                  pl.BlockSpec((1, 1, 3 * D_MODEL), lambda t: (_mod_row(t), 0, 0))],
        out_specs=pl.BlockSpec((tm, D_MODEL), lambda t: (t, 0)),
        out_shape=jax.ShapeDtypeStruct((N_TOK, D_MODEL), F32),
        compiler_params=pltpu.CompilerParams(dimension_semantics=("parallel",),
                                             vmem_limit_bytes=_vmem_limit(est)),
    )(ma, mb, mb_ctx, mc, mc_ctx, x_lat, x_ctx, w_out_b, mods_l.reshape(MOD_ROWS, 1, 3 * D_MODEL))


def _outproj_final(ma, mb, mc, x_lat, w_out_b, mods_l, final_g):
    tm = ROW_TILE
    est = (2 * tm * D_MODEL * 2 + D_MODEL * D_MODEL * 2 + 4 * tm * D_MODEL * 4 + 3 * tm * OUT_CHUNK * 4)
    return pl.pallas_call(
        _outproj_final_kernel,
        grid=(LAT_TILES,),
        in_specs=[pl.BlockSpec((tm, W_A), lambda t: (t, 0)),
                  pl.BlockSpec((tm, W_B), lambda t: (t, 0)),
                  pl.BlockSpec((tm, W_C), lambda t: (t, 0)),
                  pl.BlockSpec((tm, D_MODEL), lambda t: (t, 0)),
                  pl.BlockSpec((D_MODEL, D_MODEL), lambda t: (0, 0), pipeline_mode=pl.Buffered(1)),
                  pl.BlockSpec((1, 1, 3 * D_MODEL), lambda t: (_mod_row(t), 0, 0)),
                  pl.BlockSpec((1, D_MODEL), lambda t: (0, 0))],
        out_specs=pl.BlockSpec((tm, D_MODEL), lambda t: (t, 0)),
        out_shape=jax.ShapeDtypeStruct((N_LAT, D_MODEL), F32),
        compiler_params=pltpu.CompilerParams(dimension_semantics=("parallel",),
                                             vmem_limit_bytes=_vmem_limit(est)),
    )(ma, mb, mc, x_lat, w_out_b, mods_l.reshape(MOD_ROWS, 1, 3 * D_MODEL), final_g.reshape(1, D_MODEL))


def _cols(w, name):
    a, b = _SRC[name]
    return w[:, a:b]


W_PREP_ROWS = 256


def _prep_w_in_kernel(w_ref, kr_ref, o_ref):
    h0, h1 = _SRC["u"][0], _SRC["ckv"][1]
    t0, t1 = _SRC["gb"][0], _SRC["gc"][1]
    o_ref[:, 0:h1 - h0] = w_ref[:, h0:h1].astype(BF16)
    o_ref[:, OFF_GB:OFF_KR] = w_ref[:, t0:t1].astype(BF16)
    o_ref[:, OFF_KR:Z_W] = kr_ref[...]


def _prep_w_in(w_in, l):
    assert _SRC["ckv"][1] == OFF_GB and OFF_GB + (_SRC["gc"][1] - _SRC["gb"][0]) == OFF_KR
    kr = w_in[l, :, _SRC["kr"][0]:_SRC["kr"][1]]
    kr_sw = jnp.concatenate([kr[:, _Q4:2 * _Q4], kr[:, 0:_Q4], kr[:, 3 * _Q4:4 * _Q4], kr[:, 2 * _Q4:3 * _Q4]], axis=1)
    kr4 = jnp.concatenate([kr, kr, kr_sw, kr_sw], axis=1).astype(BF16)
    rows = W_PREP_ROWS
    est = 2 * rows * (IN_W * 4 + Z_W * 2 + 4 * ROPE_DIM * 2) + 2 * rows * IN_W * 4
    return pl.pallas_call(
        _prep_w_in_kernel,
        grid=(D_MODEL // rows,),
        in_specs=[pl.BlockSpec((rows, IN_W), lambda i: (l * (D_MODEL // rows) + i, 0)),
                  pl.BlockSpec((rows, 4 * ROPE_DIM), lambda i: (i, 0))],
        out_specs=pl.BlockSpec((rows, Z_W), lambda i: (i, 0)),
        out_shape=jax.ShapeDtypeStruct((D_MODEL, Z_W), BF16),
        compiler_params=pltpu.CompilerParams(dimension_semantics=("parallel",),
                                             vmem_limit_bytes=_vmem_limit(est)),
    )(w_in.reshape(DEPTH * D_MODEL, IN_W), kr4)


def _cast_kernel(w_ref, o_ref):
    o_ref[...] = w_ref[...].astype(BF16)


def _prep_w_out(w_out, l):
    rows = W_PREP_ROWS
    return pl.pallas_call(
        _cast_kernel,
        grid=(MIX_W // rows,),
        in_specs=[pl.BlockSpec((rows, D_MODEL), lambda i: (l * (MIX_W // rows) + i, 0))],
        out_specs=pl.BlockSpec((rows, D_MODEL), lambda i: (i, 0)),
        out_shape=jax.ShapeDtypeStruct((MIX_W, D_MODEL), BF16),
        compiler_params=pltpu.CompilerParams(dimension_semantics=("parallel",)),
    )(w_out.reshape(DEPTH * MIX_W, D_MODEL))


def _prep_w_uq(w):
    w3 = w.reshape(Q_LORA, B_HEADS, NOPE_DIM + ROPE_DIM)
    nope = w3[:, :, :NOPE_DIM].reshape(Q_LORA, B_HEADS * NOPE_DIM)
    rope = w3[:, :, NOPE_DIM:]
    rope_sw = jnp.concatenate([rope[..., _Q4:2 * _Q4], rope[..., 0:_Q4],
                               rope[..., 3 * _Q4:4 * _Q4], rope[..., 2 * _Q4:3 * _Q4]], axis=-1)
    return jnp.concatenate([nope, rope.reshape(Q_LORA, -1), rope_sw.reshape(Q_LORA, -1)], axis=1).astype(BF16)


def _prep_w_ukv(w):
    w3 = w.reshape(KV_LORA, B_HEADS, NOPE_DIM + V_DIM)
    return jnp.concatenate([w3[:, :, :NOPE_DIM].reshape(KV_LORA, -1),
                            w3[:, :, NOPE_DIM:].reshape(KV_LORA, -1)], axis=1).astype(BF16)


def _rope_tables():
    t = np.arange(SEQ)
    row = (t // GRID_W).astype(np.float32)
    col = (t % GRID_W).astype(np.float32)
    inv = (np.float32(ROPE_THETA) ** (-np.arange(_Q4, dtype=np.float32) / np.float32(_Q4))).astype(np.float32)
    ar, ac = row[:, None] * inv, col[:, None] * inv
    cos = np.concatenate([np.cos(ar), np.cos(ar), np.cos(ac), np.cos(ac)], axis=1)
    sin = np.concatenate([-np.sin(ar), np.sin(ar), -np.sin(ac), np.sin(ac)], axis=1)
    cos = np.concatenate([cos, np.ones((ROW_TILE, ROPE_DIM))], axis=0).astype(np.float32)
    sin = np.concatenate([sin, np.zeros((ROW_TILE, ROPE_DIM))], axis=0).astype(np.float32)
    return np.concatenate([cos, cos], axis=1), np.concatenate([sin, sin], axis=1)


_COS2, _SIN2 = _rope_tables()


def _rope_tile(t):
    return jnp.where(t < LAT_TILES, t % TILES_PER_SAMPLE, TILES_PER_SAMPLE)


def kernel(x, c, ctx, c_ctx, w_ada, b_ada, norm_g, w_in, qa_g, kva_g, w_uq, w_ukv, sgu_g, w_s, b_s, rpb,
           w_out, final_g):
    assert x.shape == (BATCH, SEQ, D_MODEL) and ctx.shape == (BATCH, CTX_LEN, D_MODEL)
    assert w_in.shape == (DEPTH, D_MODEL, IN_W)
    cc = jnp.concatenate([c, c_ctx[None, :], jnp.zeros((MOD_ROWS - BATCH - 1, D_MODEL), F32)], axis=0)
    mods = _modulation(cc, w_ada, b_ada)
    cos2, sin2 = jnp.asarray(_COS2), jnp.asarray(_SIN2)

    x_lat, x_ctx, ctx_block = x.reshape(N_LAT, D_MODEL), ctx.reshape(N_CTX, D_MODEL), 0
    for l in range(DEPTH):
        last = l == DEPTH - 1
        z = _inproj(x_lat, x_ctx, ctx_block, mods[l], norm_g[l], _prep_w_in(w_in, l))
        q, k, vt = _upproj(z, cos2, sin2, qa_g[l], kva_g[l], _prep_w_uq(w_uq[l]), _prep_w_ukv(w_ukv[l]))
        b_s_exp = jnp.repeat(b_s[l].T, HEAD_DIM, axis=1)
        ma = _gmlp(z, sgu_g[l], w_s[l].astype(BF16), b_s_exp, LAT_TILES if last else ROW_TILES)
        mb, mb_ctx = _mla(q, k, vt, z, with_ctx_queries=not last)
        mc, mc_ctx = _natten(z, _na_bias_tables(rpb[l]), with_ctx_queries=not last)
        w_out_b = _prep_w_out(w_out, l)
        if last:
            out = _outproj_final(ma, mb, mc, x_lat, w_out_b, mods[l], final_g)
            return out.reshape(BATCH, SEQ, D_MODEL)
        xf = _outproj_mid(ma, mb, mb_ctx, mc, mc_ctx, x_lat, x_ctx, ctx_block, w_out_b, mods[l])
        x_lat, x_ctx, ctx_block = xf, xf, LAT_TILES
```

```python
import functools
import math

import numpy as np
import jax
import jax.numpy as jnp
from jax import lax
from jax.experimental import pallas as pl
from jax.experimental.pallas import tpu as pltpu

D_MODEL = 2048
BATCH = 2
SEQ = 4096
DEPTH = 2
GRID_W = 64
CTX_LEN = 256
EPS = 1e-6
NEG = -1e30
HEAD_DIM = 128
W_A = D_MODEL // 4
W_B = D_MODEL // 2
W_C = D_MODEL // 4
MIX_W = W_A + W_B + W_C
CHUNK = 128
A_GROUPS = W_A // HEAD_DIM
B_HEADS = W_B // HEAD_DIM
Q_LORA = D_MODEL // 4
KV_LORA = 512
NOPE_DIM = 128
ROPE_DIM = 64
V_DIM = 128
MLA_SCALE = (NOPE_DIM + ROPE_DIM) ** -0.5
ROPE_THETA = 10000.0
C_HEADS = W_C // HEAD_DIM
MAX_KH = 8
KW = 16
C_SCALE = HEAD_DIM ** -0.5
ROWS = SEQ // GRID_W

LANES = 128
MXU_DIM = 256
VMEM_BYTES_V7X = 64 * 1024 * 1024
VMEM_LIMIT_CAP = 56 * 1024 * 1024

F32 = jnp.float32
BF16 = jnp.bfloat16

N_LAT = BATCH * SEQ
N_CTX = BATCH * CTX_LEN
N_TOK = N_LAT + N_CTX
ROW_TILE = N_CTX
LAT_TILES = N_LAT // ROW_TILE
ROW_TILES = N_TOK // ROW_TILE
TILES_PER_SAMPLE = SEQ // ROW_TILE
Q_TILE = CTX_LEN
Q_TILES_PER_SAMPLE = SEQ // Q_TILE
KV_CHUNK = 256
assert ROW_TILE % CHUNK == 0 and SEQ % ROW_TILE == 0 and SEQ % Q_TILE == 0

OFF_U = 0
OFF_VA = OFF_U + W_A
OFF_GA = OFF_VA + W_A
OFF_CQ = OFF_GA + W_A
OFF_CKV = OFF_CQ + Q_LORA
OFF_GB = OFF_CKV + KV_LORA
OFF_QC = OFF_GB + W_B
OFF_KC = OFF_QC + W_C
OFF_VC = OFF_KC + W_C
OFF_GC = OFF_VC + W_C
OFF_KR = OFF_GC + W_C
Z_W = OFF_KR + 4 * ROPE_DIM
Z_CHUNK = 512
assert Z_W % MXU_DIM == 0 and OFF_KR % (4 * ROPE_DIM) == 0

_SRC = {}
_acc = 0
for _name, _w in (("u", W_A), ("va", W_A), ("ga", W_A), ("cq", Q_LORA), ("ckv", KV_LORA), ("kr", ROPE_DIM),
                  ("gb", W_B), ("qc", W_C), ("kc", W_C), ("vc", W_C), ("gc", W_C)):
    _SRC[_name] = (_acc, _acc + _w)
    _acc += _w
IN_W = _acc

_Q4 = ROPE_DIM // 4
_SWAP = np.concatenate([np.arange(_Q4, 2 * _Q4), np.arange(0, _Q4),
                        np.arange(3 * _Q4, 4 * _Q4), np.arange(2 * _Q4, 3 * _Q4)])


def _vmem_limit(nbytes):
    return int(min(VMEM_LIMIT_CAP, max(16 * 1024 * 1024, nbytes * 5 // 4)))


def _silu(x):
    return x * jax.nn.sigmoid(x)


def _rms(x, g):
    return x * lax.rsqrt(jnp.mean(x * x, axis=-1, keepdims=True) + EPS) * g


MOD_ROWS = 8
MOD_TN = 768


def _mod_kernel(c_ref, w_ref, b_ref, o_ref):
    a = _silu(c_ref[...])
    a_hi = a.astype(BF16)
    a_lo = (a - a_hi.astype(F32)).astype(BF16)
    w = w_ref[0]
    w_hi = w.astype(BF16)
    w_lo = (w - w_hi.astype(F32)).astype(BF16)
    r = jnp.dot(jnp.concatenate([a_hi, a_lo], axis=0), w_hi, preferred_element_type=F32)
    r_lo = jnp.dot(a_hi, w_lo, preferred_element_type=F32)
    o_ref[0] = r[0:MOD_ROWS] + r[MOD_ROWS:2 * MOD_ROWS] + r_lo + b_ref[0]


def _modulation(cc, w_ada, b_ada):
    n = 3 * D_MODEL
    est = 2 * (MOD_ROWS * D_MODEL * 4 + D_MODEL * MOD_TN * 4 + 2 * MOD_ROWS * MOD_TN * 4)
    return pl.pallas_call(
        _mod_kernel,
        grid=(DEPTH, n // MOD_TN),
        in_specs=[pl.BlockSpec((MOD_ROWS, D_MODEL), lambda l, j: (0, 0)),
                  pl.BlockSpec((1, D_MODEL, MOD_TN), lambda l, j: (l, 0, j)),
                  pl.BlockSpec((1, 1, MOD_TN), lambda l, j: (l, 0, j))],
        out_specs=pl.BlockSpec((1, MOD_ROWS, MOD_TN), lambda l, j: (l, 0, j)),
        out_shape=jax.ShapeDtypeStruct((DEPTH, MOD_ROWS, n), F32),
        compiler_params=pltpu.CompilerParams(dimension_semantics=("parallel", "parallel"),
                                             vmem_limit_bytes=_vmem_limit(est)),
    )(cc, w_ada, b_ada.reshape(DEPTH, 1, n))


def _mod_row(t):
    return jnp.where(t < LAT_TILES, t // TILES_PER_SAMPLE, BATCH)


def _lat_tile(t):
    return jnp.minimum(t, LAT_TILES - 1)


def _pick_rows(lat_ref, ctx_ref):
    return jnp.where(pl.program_id(0) < LAT_TILES, lat_ref[...], ctx_ref[...])


def _inproj_kernel(xl_ref, xc_ref, mod_ref, g_ref, w_ref, o_ref, h_ref):
    x = _pick_rows(xl_ref, xc_ref)
    shift = mod_ref[0, :, 0:D_MODEL]
    scale = mod_ref[0, :, D_MODEL:2 * D_MODEL]
    h_ref[...] = (_rms(x, g_ref[...]) * (1.0 + scale) + shift).astype(BF16)
    for c0 in range(0, Z_W, Z_CHUNK):
        c1 = min(c0 + Z_CHUNK, Z_W)
        zc = lax.dot_general(h_ref[...], w_ref[c0:c1, :], (((1,), (1,)), ((), ())), preferred_element_type=F32)
        o_ref[:, c0:c1] = zc.astype(BF16)


def _inproj(x_lat, x_ctx, ctx_block, mods_l, norm_g, w_in_p):
    est = (3 * ROW_TILE * D_MODEL * 4 + D_MODEL * Z_W * 2 + 2 * ROW_TILE * Z_W * 2 + ROW_TILE * D_MODEL * 2
           + 2 * ROW_TILE * Z_CHUNK * 4)
    return pl.pallas_call(
        _inproj_kernel,
        grid=(ROW_TILES,),
        in_specs=[pl.BlockSpec((ROW_TILE, D_MODEL), lambda t: (_lat_tile(t), 0)),
                  pl.BlockSpec((ROW_TILE, D_MODEL), lambda t: (ctx_block, 0), pipeline_mode=pl.Buffered(1)),
                  pl.BlockSpec((1, 1, 3 * D_MODEL), lambda t: (_mod_row(t), 0, 0)),
                  pl.BlockSpec((1, D_MODEL), lambda t: (0, 0)),
                  pl.BlockSpec((Z_W, D_MODEL), lambda t: (0, 0), pipeline_mode=pl.Buffered(1))],
        out_specs=pl.BlockSpec((ROW_TILE, Z_W), lambda t: (t, 0)),
        out_shape=jax.ShapeDtypeStruct((N_TOK, Z_W), BF16),
        scratch_shapes=[pltpu.VMEM((ROW_TILE, D_MODEL), BF16)],
        compiler_params=pltpu.CompilerParams(dimension_semantics=("parallel",),
                                             vmem_limit_bytes=_vmem_limit(est)),
    )(x_lat, x_ctx, mods_l.reshape(MOD_ROWS, 1, 3 * D_MODEL), norm_g.reshape(1, D_MODEL), w_in_p)


QK_W = 2 * HEAD_DIM
UQ_W = B_HEADS * (NOPE_DIM + 2 * ROPE_DIM)
UKV_W = B_HEADS * (NOPE_DIM + V_DIM)
_Q_PRESCALE = MLA_SCALE * math.log2(math.e)
assert B_HEADS % 2 == 0 and 2 * ROPE_DIM == LANES


def _upproj_kernel(cq_ref, ckv_ref, kr_ref, cos_ref, sin_ref, qg_ref, kg_ref, wq_ref, wkv_ref,
                   q_ref, k_ref, vt_ref):
    cos2 = cos_ref[...]
    sin2 = sin_ref[...]
    n_rope = B_HEADS * ROPE_DIM
    cqn = _rms(cq_ref[...].astype(F32), qg_ref[...]).astype(BF16)
    q_all = jnp.dot(cqn, wq_ref[...], preferred_element_type=F32)
    rope0 = B_HEADS * NOPE_DIM
    for h in range(B_HEADS):
        q_ref[h, :, 0:NOPE_DIM] = (q_all[:, h * NOPE_DIM:(h + 1) * NOPE_DIM] * _Q_PRESCALE).astype(BF16)
    for j in range(B_HEADS // 2):
        a = q_all[:, rope0 + j * LANES: rope0 + (j + 1) * LANES]
        a_sw = q_all[:, rope0 + n_rope + j * LANES: rope0 + n_rope + (j + 1) * LANES]
        rot = ((a * cos2 + a_sw * sin2) * _Q_PRESCALE).astype(BF16)
        q_ref[2 * j, :, NOPE_DIM:QK_W] = rot
        q_ref[2 * j + 1, :, NOPE_DIM:QK_W] = rot

    ckvn = _rms(ckv_ref[...].astype(F32), kg_ref[...]).astype(BF16)
    kv_all = jnp.dot(ckvn, wkv_ref[...], preferred_element_type=F32)
    kr = kr_ref[...].astype(F32)
    krot = kr[:, 0:LANES] * cos2 + kr[:, LANES:2 * LANES] * sin2
    lane = lax.broadcasted_iota(jnp.int32, krot.shape, 1)
    k_lo = jnp.where(lane < ROPE_DIM, krot, 0.0).astype(BF16)
    k_hi = jnp.where(lane >= ROPE_DIM, krot, 0.0).astype(BF16)
    v0 = B_HEADS * NOPE_DIM
    for h in range(B_HEADS):
        k_ref[h, :, 0:NOPE_DIM] = kv_all[:, h * NOPE_DIM:(h + 1) * NOPE_DIM].astype(BF16)
        k_ref[h, :, NOPE_DIM:QK_W] = k_lo if h % 2 == 0 else k_hi
        v_h = kv_all[:, v0 + h * V_DIM: v0 + (h + 1) * V_DIM]
        for c in range(ROW_TILE // KV_CHUNK):
            vt_ref[h, c] = v_h[c * KV_CHUNK:(c + 1) * KV_CHUNK, :].T.astype(BF16)


def _upproj(z, cos2, sin2, qa_g, kva_g, w_uq_p, w_ukv_p):
    tm = ROW_TILE
    est = 2 * (2 * tm * Q_LORA * 2 + tm * 4 * ROPE_DIM * 2 + 2 * tm * LANES * 4 + Q_LORA * UQ_W * 2
               + KV_LORA * UKV_W * 2 + B_HEADS * tm * (2 * QK_W + V_DIM) * 2) + 4 * tm * UQ_W * 4
    return pl.pallas_call(
        _upproj_kernel,
        grid=(ROW_TILES,),
        in_specs=[pl.BlockSpec((tm, Q_LORA), lambda t: (t, OFF_CQ // Q_LORA)),
                  pl.BlockSpec((tm, KV_LORA), lambda t: (t, OFF_CKV // KV_LORA)),
                  pl.BlockSpec((tm, 4 * ROPE_DIM), lambda t: (t, OFF_KR // (4 * ROPE_DIM))),
                  pl.BlockSpec((tm, LANES), lambda t: (_rope_tile(t), 0)),
                  pl.BlockSpec((tm, LANES), lambda t: (_rope_tile(t), 0)),
                  pl.BlockSpec((1, Q_LORA), lambda t: (0, 0)),
                  pl.BlockSpec((1, KV_LORA), lambda t: (0, 0)),
                  pl.BlockSpec((Q_LORA, UQ_W), lambda t: (0, 0)),
                  pl.BlockSpec((KV_LORA, UKV_W), lambda t: (0, 0))],
        out_specs=[pl.BlockSpec((B_HEADS, tm, QK_W), lambda t: (0, t, 0)),
                   pl.BlockSpec((B_HEADS, tm, QK_W), lambda t: (0, t, 0)),
                   pl.BlockSpec((B_HEADS, tm // KV_CHUNK, V_DIM, KV_CHUNK), lambda t: (0, t, 0, 0))],
        out_shape=[jax.ShapeDtypeStruct((B_HEADS, N_TOK, QK_W), BF16),
                   jax.ShapeDtypeStruct((B_HEADS, N_TOK, QK_W), BF16),
                   jax.ShapeDtypeStruct((B_HEADS, N_TOK // KV_CHUNK, V_DIM, KV_CHUNK), BF16)],
        compiler_params=pltpu.CompilerParams(dimension_semantics=("parallel",),
                                             vmem_limit_bytes=_vmem_limit(est)),
    )(z, z, z, cos2, sin2, qa_g.reshape(1, Q_LORA), kva_g.reshape(1, KV_LORA), w_uq_p, w_ukv_p)


def _gmlp_kernel(u_ref, v_ref, g_ref, sg_ref, ws_ref, bs_ref, o_ref):
    v = jax.nn.gelu(v_ref[...].astype(F32))
    vn = _rms(v, sg_ref[...]).astype(BF16)
    front = jax.nn.gelu(u_ref[...].astype(F32))
    gate = _silu(g_ref[...].astype(F32))
    for c in range(ROW_TILE // CHUNK):
        r0, r1 = c * CHUNK, (c + 1) * CHUNK
        for g in range(A_GROUPS):
            c0, c1 = g * HEAD_DIM, (g + 1) * HEAD_DIM
            s = jnp.dot(ws_ref[g], vn[r0:r1, c0:c1], preferred_element_type=F32) + bs_ref[:, c0:c1]
            o_ref[r0:r1, c0:c1] = (front[r0:r1, c0:c1] * s * gate[r0:r1, c0:c1]).astype(BF16)


def _gmlp(z, sgu_g, w_s_b, b_s_exp, n_tiles):
    tm = ROW_TILE
    est = 2 * (4 * tm * W_A * 2 + A_GROUPS * CHUNK * CHUNK * 2 + CHUNK * W_A * 4) + 5 * tm * W_A * 4
    return pl.pallas_call(
        _gmlp_kernel,
        grid=(n_tiles,),
        in_specs=[pl.BlockSpec((tm, W_A), lambda t: (t, OFF_U // W_A)),
                  pl.BlockSpec((tm, W_A), lambda t: (t, OFF_VA // W_A)),
                  pl.BlockSpec((tm, W_A), lambda t: (t, OFF_GA // W_A)),
                  pl.BlockSpec((1, W_A), lambda t: (0, 0)),
                  pl.BlockSpec((A_GROUPS, CHUNK, CHUNK), lambda t: (0, 0, 0)),
                  pl.BlockSpec((CHUNK, W_A), lambda t: (0, 0))],
        out_specs=pl.BlockSpec((tm, W_A), lambda t: (t, 0)),
        out_shape=jax.ShapeDtypeStruct((n_tiles * tm, W_A), BF16),
        compiler_params=pltpu.CompilerParams(dimension_semantics=("parallel",),
                                             vmem_limit_bytes=_vmem_limit(est)),
    )(z, z, z, sgu_g.reshape(1, W_A), w_s_b, b_s_exp)


MLA_TK = 512
MLA_TQ = 1024
MLA_TQG = 256
assert SEQ % MLA_TK == 0 and MLA_TK % KV_CHUNK == 0 and SEQ % MLA_TQ == 0


def _attend_t(qs, chunks):
    def scores(k, q):
        return lax.dot_general(k, q, (((1,), (1,)), ((), ())), preferred_element_type=F32)

    state = [None] * len(qs)
    s_next = [scores(chunks[0][0], q) for q in qs]
    for j, (_, vts) in enumerate(chunks):
        for g, q in enumerate(qs):
            s = s_next[g]
            if j + 1 < len(chunks):
                s_next[g] = scores(chunks[j + 1][0], q)
            s_max = jnp.max(s, axis=0, keepdims=True)
            m_new = s_max if state[g] is None else jnp.maximum(state[g][0], s_max)
            p_sum = pv = None
            for n, vt in enumerate(vts):
                p = jnp.exp2(s[n * KV_CHUNK:(n + 1) * KV_CHUNK] - m_new)
                ps = jnp.sum(p, axis=0, keepdims=True)
                d = jnp.dot(vt, p.astype(BF16), preferred_element_type=F32)
                p_sum, pv = (ps, d) if n == 0 else (p_sum + ps, pv + d)
            if state[g] is None:
                state[g] = (m_new, p_sum, pv)
            else:
                m, l, acc = state[g]
                alpha = jnp.exp2(m - m_new)
                state[g] = (m_new, alpha * l + p_sum, alpha * acc + pv)
    return [acc / l for _, l, acc in state]


def _mla_lat_kernel(q_ref, kl_ref, vtl_ref, kc_ref, vtc_ref, gb_ref, o_ref):
    per = MLA_TK // KV_CHUNK
    chunks = [(kl_ref[0, j * MLA_TK:(j + 1) * MLA_TK, :], [vtl_ref[0, j * per + n] for n in range(per)])
              for j in range(SEQ // MLA_TK)]
    chunks.append((kc_ref[0], [vtc_ref[0, n] for n in range(CTX_LEN // KV_CHUNK)]))
    groups = [(g * MLA_TQG, (g + 1) * MLA_TQG) for g in range(MLA_TQ // MLA_TQG)]
    outs = _attend_t([q_ref[0, lo:hi, :] for lo, hi in groups], chunks)
    for (lo, hi), o_t in zip(groups, outs):
        o_ref[lo:hi, :] = (o_t.T * _silu(gb_ref[lo:hi, :].astype(F32))).astype(BF16)


def _mla_ctx_kernel(q_ref, kc_ref, vtc_ref, gb_ref, o_ref):
    (o_t,) = _attend_t([q_ref[0]], [(kc_ref[0], [vtc_ref[0, n] for n in range(CTX_LEN // KV_CHUNK)])])
    o = o_t.T
    o_ref[...] = (o * _silu(gb_ref[...].astype(F32))).astype(BF16)


def _mla(q, k, vt, z, with_ctx_queries):
    ctx_tile = N_LAT // CTX_LEN
    tq = MLA_TQ
    nq = SEQ // tq
    est = 2 * (tq * QK_W * 2 + SEQ * QK_W * 2 + SEQ * V_DIM * 2 + CTX_LEN * (QK_W + V_DIM) * 2
               + 2 * tq * HEAD_DIM * 2) + 8 * MLA_TK * tq * 4
    mb = pl.pallas_call(
        _mla_lat_kernel,
        grid=(BATCH, B_HEADS, nq),
        in_specs=[pl.BlockSpec((1, tq, QK_W), lambda b, h, i: (h, b * nq + i, 0)),
                  pl.BlockSpec((1, SEQ, QK_W), lambda b, h, i: (h, b, 0)),
                  pl.BlockSpec((1, SEQ // KV_CHUNK, V_DIM, KV_CHUNK), lambda b, h, i: (h, b, 0, 0)),
                  pl.BlockSpec((1, CTX_LEN, QK_W), lambda b, h, i: (h, ctx_tile + b, 0)),
                  pl.BlockSpec((1, CTX_LEN // KV_CHUNK, V_DIM, KV_CHUNK), lambda b, h, i: (h, ctx_tile + b, 0, 0)),
                  pl.BlockSpec((tq, HEAD_DIM), lambda b, h, i: (b * nq + i, OFF_GB // HEAD_DIM + h))],
        out_specs=pl.BlockSpec((tq, HEAD_DIM), lambda b, h, i: (b * nq + i, h)),
        out_shape=jax.ShapeDtypeStruct((N_LAT, W_B), BF16),
        compiler_params=pltpu.CompilerParams(dimension_semantics=("parallel", "parallel", "parallel"),
                                             vmem_limit_bytes=_vmem_limit(est)),
    )(q, k, vt, k, vt, z)
    if not with_ctx_queries:
        return mb, None
    mb_ctx = pl.pallas_call(
        _mla_ctx_kernel,
        grid=(BATCH, B_HEADS),
        in_specs=[pl.BlockSpec((1, CTX_LEN, QK_W), lambda b, h: (h, ctx_tile + b, 0)),
                  pl.BlockSpec((1, CTX_LEN, QK_W), lambda b, h: (h, ctx_tile + b, 0)),
                  pl.BlockSpec((1, CTX_LEN // KV_CHUNK, V_DIM, KV_CHUNK), lambda b, h: (h, ctx_tile + b, 0, 0)),
                  pl.BlockSpec((CTX_LEN, HEAD_DIM), lambda b, h: (ctx_tile + b, OFF_GB // HEAD_DIM + h))],
        out_specs=pl.BlockSpec((CTX_LEN, HEAD_DIM), lambda b, h: (b, h)),
        out_shape=jax.ShapeDtypeStruct((N_CTX, W_B), BF16),
        compiler_params=pltpu.CompilerParams(dimension_semantics=("parallel", "parallel")),
    )(q, k, vt, z)
    return mb, mb_ctx


NA_QROWS = Q_TILE // GRID_W
NA_KROWS = 12
NA_KEYS = NA_KROWS * GRID_W
assert NA_QROWS + MAX_KH <= NA_KROWS + 1 and ROWS >= NA_KROWS


def _na_strip_row(j):
    return int(np.clip(j * NA_QROWS - MAX_KH // 2, 0, ROWS - NA_KROWS))


def _na_tables():
    def one(j):
        s = _na_strip_row(j)
        valid = np.zeros((NA_QROWS, NA_KROWS), bool)
        d = np.zeros((NA_QROWS, NA_KROWS), np.int32)
        for a in range(NA_QROWS):
            r = j * NA_QROWS + a
            r0 = int(np.clip(r - MAX_KH // 2, 0, ROWS - MAX_KH))
            for i in range(NA_KROWS):
                valid[a, i] = 0 <= s + i - r0 < MAX_KH
                d[a, i] = s + i - r + (MAX_KH - 1) if valid[a, i] else 0
        return valid, d
    n_tiles = ROWS // NA_QROWS
    first, mid, last = one(0), one(1), one(n_tiles - 1)
    for j in range(1, n_tiles - 1):
        v, d = one(j)
        assert (v == mid[0]).all() and (d == mid[1]).all()
    return np.stack([first[0], mid[0], last[0]]), np.stack([first[1], mid[1], last[1]])


_NA_VALID, _NA_DROW = _na_tables()


def _na_bias_tables(rpb_l):
    col = np.arange(GRID_W)
    c0 = np.clip(col - KW // 2, 0, GRID_W - KW)
    col_ok = (col[None, :] >= c0[:, None]) & (col[None, :] < c0[:, None] + KW)
    dc = np.clip(col[None, :] - col[:, None], -(KW - 1), KW - 1) + (KW - 1)
    onehot = (dc.reshape(-1)[None, :] == np.arange(2 * KW - 1)[:, None]).astype(np.float32)
    t1 = jnp.dot(rpb_l.reshape(C_HEADS * (2 * MAX_KH - 1), 2 * KW - 1), jnp.asarray(onehot),
                 precision=lax.Precision.HIGHEST).reshape(C_HEADS, 2 * MAX_KH - 1, GRID_W, GRID_W)
    t1 = jnp.where(jnp.asarray(col_ok)[None, None], t1, NEG)
    return jnp.concatenate([t1, t1], axis=-1)


def _softmax_rows(blocks):
    m = functools.reduce(jnp.maximum, [jnp.max(s, axis=-1, keepdims=True) for s in blocks])
    ps = [jnp.exp(s - m) for s in blocks]
    return ps, functools.reduce(jnp.add, [jnp.sum(p, axis=-1, keepdims=True) for p in ps])


def _assemble_bias(t_ref, bias_ref):
    lane = lax.broadcasted_iota(jnp.int32, (GRID_W, 2 * GRID_W), 1)
    neg = jnp.full((GRID_W, 2 * GRID_W), NEG, F32)
    for cls in range(3):
        for a in range(NA_QROWS):
            for pair in range(NA_KROWS // 2):
                halves = [t_ref[0, int(_NA_DROW[cls, a, i])] if _NA_VALID[cls, a, i] else neg
                          for i in (2 * pair, 2 * pair + 1)]
                tile = halves[0] if halves[0] is halves[1] else jnp.where(lane < GRID_W, halves[0], halves[1])
                bias_ref[cls, a * GRID_W:(a + 1) * GRID_W, pair * 2 * GRID_W:(pair + 1) * 2 * GRID_W] = tile


def _natten_kernel(q_ref, kl_ref, vl_ref, kc_ref, vc_ref, t_ref, g_ref, o_ref, bias_ref):
    nt = (((1,), (1,)), ((), ()))
    n_tiles = ROWS // NA_QROWS
    _assemble_bias(t_ref, bias_ref)

    def tile(j, carry):
        rows = pl.ds(pl.multiple_of(j * Q_TILE, Q_TILE), Q_TILE)
        strip_row = jnp.clip(j * NA_QROWS - MAX_KH // 2, 0, ROWS - NA_KROWS)
        strip = pl.ds(pl.multiple_of(strip_row * GRID_W, GRID_W), NA_KEYS)
        cls = jnp.where(j == 0, 0, jnp.where(j == n_tiles - 1, 2, 1))
        q = q_ref[rows, :]
        s_nb = lax.dot_general(q, kl_ref[strip, :], nt, preferred_element_type=F32) * C_SCALE + bias_ref[cls]
        s_cx = lax.dot_general(q, kc_ref[...], nt, preferred_element_type=F32) * C_SCALE
        (p_nb, p_cx), l = _softmax_rows([s_nb, s_cx])
        o = jnp.dot(p_nb.astype(BF16), vl_ref[strip, :], preferred_element_type=F32)
        o += jnp.dot(p_cx.astype(BF16), vc_ref[...], preferred_element_type=F32)
        o_ref[rows, :] = (o / l * _silu(g_ref[rows, :].astype(F32))).astype(BF16)
        return carry

    lax.fori_loop(0, n_tiles, tile, 0, unroll=2)


def _natten_ctx_kernel(q_ref, kc_ref, vc_ref, g_ref, o_ref):
    s =lax.dot_general(q_ref[...], kc_ref[...], (((1,), (1,)), ((), ())), preferred_element_type=F32) * C_SCALE
    (p,), l = _softmax_rows([s])
    o = jnp.dot(p.astype(BF16), vc_ref[...], preferred_element_type=F32)
    o_ref[...] = (o / l * _silu(g_ref[...].astype(F32))).astype(BF16)


def _natten(z, bias_tab, with_ctx_queries):
    ctx_tile = N_LAT // CTX_LEN
    hd = HEAD_DIM
    est = (2 * (5 * SEQ * hd * 2 + 2 * CTX_LEN * hd * 2 + 3 * Q_TILE * NA_KEYS * 4)
           + 12 * Q_TILE * (NA_KEYS + CTX_LEN) * 4)
    mc = pl.pallas_call(
        _natten_kernel,
        grid=(BATCH, C_HEADS),
        in_specs=[pl.BlockSpec((SEQ, hd), lambda b, h: (b, OFF_QC // hd + h)),
                  pl.BlockSpec((SEQ, hd), lambda b, h: (b, OFF_KC // hd + h)),
                  pl.BlockSpec((SEQ, hd), lambda b, h: (b, OFF_VC // hd + h)),
                  pl.BlockSpec((CTX_LEN, hd), lambda b, h: (ctx_tile + b, OFF_KC // hd + h)),
                  pl.BlockSpec((CTX_LEN, hd), lambda b, h: (ctx_tile + b, OFF_VC // hd + h)),
                  pl.BlockSpec((1, 2 * MAX_KH - 1, GRID_W, 2 * GRID_W), lambda b, h: (h, 0, 0, 0)),
                  pl.BlockSpec((SEQ, hd), lambda b, h: (b, OFF_GC // hd + h))],
        out_specs=pl.BlockSpec((SEQ, hd), lambda b, h: (b, h)),
        out_shape=jax.ShapeDtypeStruct((N_LAT, W_C), BF16),
        scratch_shapes=[pltpu.VMEM((3, Q_TILE, NA_KEYS), F32)],
        compiler_params=pltpu.CompilerParams(dimension_semantics=("parallel", "parallel"),
                                             vmem_limit_bytes=_vmem_limit(est)),
    )(z, z, z, z, z, bias_tab, z)
    if not with_ctx_queries:
        return mc, None
    mc_ctx = pl.pallas_call(
        _natten_ctx_kernel,
        grid=(BATCH, C_HEADS),
        in_specs=[pl.BlockSpec((CTX_LEN, hd), lambda b, h: (ctx_tile + b, OFF_QC // hd + h)),
                  pl.BlockSpec((CTX_LEN, hd), lambda b, h: (ctx_tile + b, OFF_KC // hd + h)),
                  pl.BlockSpec((CTX_LEN, hd), lambda b, h: (ctx_tile + b, OFF_VC // hd + h)),
                  pl.BlockSpec((CTX_LEN, hd), lambda b, h: (ctx_tile + b, OFF_GC // hd + h))],
        out_specs=pl.BlockSpec((CTX_LEN, hd), lambda b, h: (b, h)),
        out_shape=jax.ShapeDtypeStruct((N_CTX, W_C), BF16),
        compiler_params=pltpu.CompilerParams(dimension_semantics=("parallel", "parallel")),
    )(z, z, z, z)
    return mc, mc_ctx


OUT_CHUNK = 512


def _outproj_body(ma, mb, mc, x, w_ref, mod_ref, o_ref):
    gate = mod_ref[0, :, 2 * D_MODEL:3 * D_MODEL]
    for c0 in range(0, D_MODEL, OUT_CHUNK):
        c1 = c0 + OUT_CHUNK
        y = jnp.dot(ma, w_ref[0:W_A, c0:c1], preferred_element_type=F32)
        y += jnp.dot(mb, w_ref[W_A:W_A + W_B, c0:c1], preferred_element_type=F32)
        y += jnp.dot(mc, w_ref[W_A + W_B:D_MODEL, c0:c1], preferred_element_type=F32)
        o_ref[:, c0:c1] = x[:, c0:c1] + gate[:, c0:c1] * y


def _outproj_mid_kernel(ma_ref, mbl_ref, mbc_ref, mcl_ref, mcc_ref, xl_ref, xc_ref, w_ref, mod_ref, o_ref):
    _outproj_body(ma_ref[...], _pick_rows(mbl_ref, mbc_ref), _pick_rows(mcl_ref, mcc_ref),
                  _pick_rows(xl_ref, xc_ref), w_ref, mod_ref, o_ref)


def _outproj_final_kernel(ma_ref, mb_ref, mc_ref, x_ref, w_ref, mod_ref, fg_ref, o_ref):
    _outproj_body(ma_ref[...], mb_ref[...], mc_ref[...], x_ref[...], w_ref, mod_ref, o_ref)
    o_ref[...] = _rms(o_ref[...], fg_ref[...])


def _outproj_mid(ma, mb, mb_ctx, mc, mc_ctx, x_lat, x_ctx, ctx_block, w_out_b, mods_l):
    tm = ROW_TILE
    est = (3 * tm * D_MODEL * 2 + D_MODEL * D_MODEL * 2 + 5 * tm * D_MODEL * 4 + 3 * tm * OUT_CHUNK * 4)
    lat = lambda w: pl.BlockSpec((tm, w), lambda t: (_lat_tile(t), 0))
    ctx = lambda w, blk: pl.BlockSpec((tm, w), lambda t: (blk, 0), pipeline_mode=pl.Buffered(1))
    return pl.pallas_call(
        _outproj_mid_kernel,
        grid=(ROW_TILES,),
        in_specs=[pl.BlockSpec((tm, W_A), lambda t: (t, 0)),
                  lat(W_B), ctx(W_B, 0), lat(W_C), ctx(W_C, 0), lat(D_MODEL), ctx(D_MODEL, ctx_block),
                  pl.BlockSpec((D_MODEL, D_MODEL), lambda t: (0, 0), pipeline_mode=pl.Buffered(1)),
                  pl.BlockSpec((1, 1, 3 * D_MODEL), lambda t: (_mod_row(t), 0, 0))],
        out_specs=pl.BlockSpec((tm, D_MODEL), lambda t: (t, 0)),
        out_shape=jax.ShapeDtypeStruct((N_TOK, D_MODEL), F32),
        compiler_params=pltpu.CompilerParams(dimension_semantics=("parallel",),
                                             vmem_limit_bytes=_vmem_limit(est)),
    )(ma, mb, mb_ctx, mc, mc_ctx, x_lat, x_ctx, w_out_b, mods_l.reshape(MOD_ROWS, 1, 3 * D_MODEL))


def _outproj_final(ma, mb, mc, x_lat, w_out_b, mods_l, final_g):
    tm = ROW_TILE
    est = (2 * tm * D_MODEL * 2 + D_MODEL * D_MODEL * 2 + 4 * tm * D_MODEL * 4 + 3 * tm * OUT_CHUNK * 4)
    return pl.pallas_call(
        _outproj_final_kernel,
        grid=(LAT_TILES,),
        in_specs=[pl.BlockSpec((tm, W_A), lambda t: (t, 0)),
                  pl.BlockSpec((tm, W_B), lambda t: (t, 0)),
                  pl.BlockSpec((tm, W_C), lambda t: (t, 0)),
                  pl.BlockSpec((tm, D_MODEL), lambda t: (t, 0)),
                  pl.BlockSpec((D_MODEL, D_MODEL), lambda t: (0, 0), pipeline_mode=pl.Buffered(1)),
                  pl.BlockSpec((1, 1, 3 * D_MODEL), lambda t: (_mod_row(t), 0, 0)),
                  pl.BlockSpec((1, D_MODEL), lambda t: (0, 0))],
        out_specs=pl.BlockSpec((tm, D_MODEL), lambda t: (t, 0)),
        out_shape=jax.ShapeDtypeStruct((N_LAT, D_MODEL), F32),
        compiler_params=pltpu.CompilerParams(dimension_semantics=("parallel",),
                                             vmem_limit_bytes=_vmem_limit(est)),
    )(ma, mb, mc, x_lat, w_out_b, mods_l.reshape(MOD_ROWS, 1, 3 * D_MODEL), final_g.reshape(1, D_MODEL))


def _cols(w, name):
    a, b = _SRC[name]
    return w[:, a:b]


W_PREP_ROWS = 256


def _prep_w_in_kernel(w_ref, kr_ref, o_ref):
    h0, h1 = _SRC["u"][0], _SRC["ckv"][1]
    t0, t1 = _SRC["gb"][0], _SRC["gc"][1]
    o_ref[0:h1 - h0, :] = w_ref[h0:h1, :].astype(BF16)
    o_ref[OFF_GB:OFF_KR, :] = w_ref[t0:t1, :].astype(BF16)
    o_ref[OFF_KR:Z_W, :] = kr_ref[...]


def _prep_w_in(w_in, l):
    assert _SRC["ckv"][1] == OFF_GB and OFF_GB + (_SRC["gc"][1] - _SRC["gb"][0]) == OFF_KR
    w_t = jnp.swapaxes(w_in, 1, 2)
    kr = w_t[l, _SRC["kr"][0]:_SRC["kr"][1], :]
    kr_sw = jnp.concatenate([kr[_Q4:2 * _Q4], kr[0:_Q4], kr[3 * _Q4:4 * _Q4], kr[2 * _Q4:3 * _Q4]], axis=0)
    kr4 = jnp.concatenate([kr, kr, kr_sw, kr_sw], axis=0).astype(BF16)
    cols = W_PREP_ROWS
    est = 2 * cols * (IN_W * 4 + Z_W * 2 + 4 * ROPE_DIM * 2) + 2 * cols * IN_W * 4
    return pl.pallas_call(
        _prep_w_in_kernel,
        grid=(D_MODEL // cols,),
        in_specs=[pl.BlockSpec((IN_W, cols), lambda i: (l, i)),
                  pl.BlockSpec((4 * ROPE_DIM, cols), lambda i: (0, i))],
        out_specs=pl.BlockSpec((Z_W, cols), lambda i: (0, i)),
        out_shape=jax.ShapeDtypeStruct((Z_W, D_MODEL), BF16),
        compiler_params=pltpu.CompilerParams(dimension_semantics=("parallel",),
                                             vmem_limit_bytes=_vmem_limit(est)),
    )(w_t.reshape(DEPTH * IN_W, D_MODEL), kr4)


def _cast_kernel(w_ref, o_ref):
    o_ref[...] = w_ref[...].astype(BF16)


def _prep_w_out(w_out, l):
    rows = W_PREP_ROWS
    return pl.pallas_call(
        _cast_kernel,
        grid=(MIX_W // rows,),
        in_specs=[pl.BlockSpec((rows, D_MODEL), lambda i: (l * (MIX_W // rows) + i, 0))],
        out_specs=pl.BlockSpec((rows, D_MODEL), lambda i: (i, 0)),
        out_shape=jax.ShapeDtypeStruct((MIX_W, D_MODEL), BF16),
        compiler_params=pltpu.CompilerParams(dimension_semantics=("parallel",)),
    )(w_out.reshape(DEPTH * MIX_W, D_MODEL))


def _prep_w_uq(w):
    w3 = w.reshape(Q_LORA, B_HEADS, NOPE_DIM + ROPE_DIM)
    nope = w3[:, :, :NOPE_DIM].reshape(Q_LORA, B_HEADS * NOPE_DIM)
    rope = w3[:, :, NOPE_DIM:]
    rope_sw = jnp.concatenate([rope[..., _Q4:2 * _Q4], rope[..., 0:_Q4],
                               rope[..., 3 * _Q4:4 * _Q4], rope[..., 2 * _Q4:3 * _Q4]], axis=-1)
    return jnp.concatenate([nope, rope.reshape(Q_LORA, -1), rope_sw.reshape(Q_LORA, -1)], axis=1).astype(BF16)


def _prep_w_ukv(w):
    w3 = w.reshape(KV_LORA, B_HEADS, NOPE_DIM + V_DIM)
    return jnp.concatenate([w3[:, :, :NOPE_DIM].reshape(KV_LORA, -1),
                            w3[:, :, NOPE_DIM:].reshape(KV_LORA, -1)], axis=1).astype(BF16)


def _rope_tables():
    t = np.arange(SEQ)
    row = (t // GRID_W).astype(np.float32)
    col = (t % GRID_W).astype(np.float32)
    inv = (np.float32(ROPE_THETA) ** (-np.arange(_Q4, dtype=np.float32) / np.float32(_Q4))).astype(np.float32)
    ar, ac = row[:, None] * inv, col[:, None] * inv
    cos = np.concatenate([np.cos(ar), np.cos(ar), np.cos(ac), np.cos(ac)], axis=1)
    sin = np.concatenate([-np.sin(ar), np.sin(ar), -np.sin(ac), np.sin(ac)], axis=1)
    cos = np.concatenate([cos, np.ones((ROW_TILE, ROPE_DIM))], axis=0).astype(np.float32)
    sin = np.concatenate([sin, np.zeros((ROW_TILE, ROPE_DIM))], axis=0).astype(np.float32)
    return np.concatenate([cos, cos], axis=1), np.concatenate([sin, sin], axis=1)


_COS2, _SIN2 = _rope_tables()


def _rope_tile(t):
    return jnp.where(t < LAT_TILES, t % TILES_PER_SAMPLE, TILES_PER_SAMPLE)


def kernel(x, c, ctx, c_ctx, w_ada, b_ada, norm_g, w_in, qa_g, kva_g, w_uq, w_ukv, sgu_g, w_s, b_s, rpb,
           w_out, final_g):
    assert x.shape == (BATCH, SEQ, D_MODEL) and ctx.shape == (BATCH, CTX_LEN, D_MODEL)
    assert w_in.shape == (DEPTH, D_MODEL, IN_W)
    cc = jnp.concatenate([c, c_ctx[None, :], jnp.zeros((MOD_ROWS - BATCH - 1, D_MODEL), F32)], axis=0)
    mods = _modulation(cc, w_ada, b_ada)
    cos2, sin2 = jnp.asarray(_COS2), jnp.asarray(_SIN2)

    x_lat, x_ctx, ctx_block = x.reshape(N_LAT, D_MODEL), ctx.reshape(N_CTX, D_MODEL), 0
    for l in range(DEPTH):
        last = l == DEPTH - 1
        z = _inproj(x_lat, x_ctx, ctx_block, mods[l], norm_g[l], _prep_w_in(w_in, l))
        q, k, vt = _upproj(z, cos2, sin2, qa_g[l], kva_g[l], _prep_w_uq(w_uq[l]), _prep_w_ukv(w_ukv[l]))
        b_s_exp = jnp.repeat(b_s[l].T, HEAD_DIM, axis=1)
        ma = _gmlp(z, sgu_g[l], w_s[l].astype(BF16), b_s_exp, LAT_TILES if last else ROW_TILES)
        mb, mb_ctx = _mla(q, k, vt, z, with_ctx_queries=not last)
        mc, mc_ctx = _natten(z, _na_bias_tables(rpb[l]), with_ctx_queries=not last)
        w_out_b = _prep_w_out(w_out, l)
        if last:
            out = _outproj_final(ma, mb, mc, x_lat, w_out_b, mods[l], final_g)
            return out.reshape(BATCH, SEQ, D_MODEL)
        xf = _outproj_mid(ma, mb, mb_ctx, mc, mc_ctx, x_lat, x_ctx, ctx_block, w_out_b, mods[l])
        x_lat, x_ctx, ctx_block = xf, xf, LAT_TILES
```

```python
import functools
import math

import numpy as np
import jax
import jax.numpy as jnp
from jax import lax
from jax.experimental import pallas as pl
from jax.experimental.pallas import tpu as pltpu

D_MODEL = 2048
BATCH = 2
SEQ = 4096
DEPTH = 2
GRID_W = 64
CTX_LEN = 256
EPS = 1e-6
NEG = -1e30
HEAD_DIM = 128
W_A = D_MODEL // 4
W_B = D_MODEL // 2
W_C = D_MODEL // 4
MIX_W = W_A + W_B + W_C
CHUNK = 128
A_GROUPS = W_A // HEAD_DIM
B_HEADS = W_B // HEAD_DIM
Q_LORA = D_MODEL // 4
KV_LORA = 512
NOPE_DIM = 128
ROPE_DIM = 64
V_DIM = 128
MLA_SCALE = (NOPE_DIM + ROPE_DIM) ** -0.5
ROPE_THETA = 10000.0
C_HEADS = W_C // HEAD_DIM
MAX_KH = 8
KW = 16
C_SCALE = HEAD_DIM ** -0.5
ROWS = SEQ // GRID_W

LANES = 128
MXU_DIM = 256
VMEM_BYTES_V7X = 64 * 1024 * 1024
VMEM_LIMIT_CAP = 56 * 1024 * 1024

F32 = jnp.float32
BF16 = jnp.bfloat16

N_LAT = BATCH * SEQ
N_CTX = BATCH * CTX_LEN
N_TOK = N_LAT + N_CTX
ROW_TILE = N_CTX
LAT_TILES = N_LAT // ROW_TILE
ROW_TILES = N_TOK // ROW_TILE
TILES_PER_SAMPLE = SEQ // ROW_TILE
Q_TILE = CTX_LEN
Q_TILES_PER_SAMPLE = SEQ // Q_TILE
KV_CHUNK = 256
assert ROW_TILE % CHUNK == 0 and SEQ % ROW_TILE == 0 and SEQ % Q_TILE == 0

GM_W = 3 * W_A
OFF_CQ = 0
OFF_CKV = OFF_CQ + Q_LORA
OFF_GB = OFF_CKV + KV_LORA
OFF_QC = OFF_GB + W_B
OFF_KC = OFF_QC + W_C
OFF_VC = OFF_KC + W_C
OFF_GC = OFF_VC + W_C
OFF_KR = OFF_GC + W_C
Z_W = OFF_KR + 4 * ROPE_DIM
W_ROWS = GM_W + Z_W
Z_CHUNK = 512
assert W_ROWS % MXU_DIM == 0 and OFF_KR % (4 * ROPE_DIM) == 0 and Z_CHUNK == W_A

_SRC = {}
_acc = 0
for _name, _w in (("u", W_A), ("va", W_A), ("ga", W_A), ("cq", Q_LORA), ("ckv", KV_LORA), ("kr", ROPE_DIM),
                  ("gb", W_B), ("qc", W_C), ("kc", W_C), ("vc", W_C), ("gc", W_C)):
    _SRC[_name] = (_acc, _acc + _w)
    _acc += _w
IN_W = _acc

_Q4 = ROPE_DIM // 4
_SWAP = np.concatenate([np.arange(_Q4, 2 * _Q4), np.arange(0, _Q4),
                        np.arange(3 * _Q4, 4 * _Q4), np.arange(2 * _Q4, 3 * _Q4)])


def _vmem_limit(nbytes):
    return int(min(VMEM_LIMIT_CAP, max(16 * 1024 * 1024, nbytes * 5 // 4)))


def _silu(x):
    return x * jax.nn.sigmoid(x)


def _rms(x, g):
    return x * lax.rsqrt(jnp.mean(x * x, axis=-1, keepdims=True) + EPS) * g


MOD_ROWS = 8
MOD_TN = 768


def _mod_kernel(c_ref, w_ref, b_ref, o_ref):
    a = _silu(c_ref[...])
    a_hi = a.astype(BF16)
    a_lo = (a - a_hi.astype(F32)).astype(BF16)
    w = w_ref[0]
    w_hi = w.astype(BF16)
    w_lo = (w - w_hi.astype(F32)).astype(BF16)
    r = jnp.dot(jnp.concatenate([a_hi, a_lo], axis=0), w_hi, preferred_element_type=F32)
    r_lo = jnp.dot(a_hi, w_lo, preferred_element_type=F32)
    o_ref[0] = r[0:MOD_ROWS] + r[MOD_ROWS:2 * MOD_ROWS] + r_lo + b_ref[0]


def _modulation(cc, w_ada, b_ada):
    n = 3 * D_MODEL
    est = 2 * (MOD_ROWS * D_MODEL * 4 + D_MODEL * MOD_TN * 4 + 2 * MOD_ROWS * MOD_TN * 4)
    return pl.pallas_call(
        _mod_kernel,
        grid=(DEPTH, n // MOD_TN),
        in_specs=[pl.BlockSpec((MOD_ROWS, D_MODEL), lambda l, j: (0, 0)),
                  pl.BlockSpec((1, D_MODEL, MOD_TN), lambda l, j: (l, 0, j)),
                  pl.BlockSpec((1, 1, MOD_TN), lambda l, j: (l, 0, j))],
        out_specs=pl.BlockSpec((1, MOD_ROWS, MOD_TN), lambda l, j: (l, 0, j)),
        out_shape=jax.ShapeDtypeStruct((DEPTH, MOD_ROWS, n), F32),
        compiler_params=pltpu.CompilerParams(dimension_semantics=("parallel", "parallel"),
                                             vmem_limit_bytes=_vmem_limit(est)),
    )(cc, w_ada, b_ada.reshape(DEPTH, 1, n))


def _mod_row(t):
    return jnp.where(t < LAT_TILES, t // TILES_PER_SAMPLE, BATCH)


def _lat_tile(t):
    return jnp.minimum(t, LAT_TILES - 1)


def _pick_rows(lat_ref, ctx_ref):
    return jnp.where(pl.program_id(0) < LAT_TILES, lat_ref[...], ctx_ref[...])


def _inproj_kernel(xl_ref, xc_ref, mod_ref, g_ref, w_ref, sg_ref, ws_ref, bs_ref, z_ref, ma_ref, h_ref):
    nt = (((1,), (1,)), ((), ()))
    x = _pick_rows(xl_ref, xc_ref)
    shift = mod_ref[0, :, 0:D_MODEL]
    scale = mod_ref[0, :, D_MODEL:2 * D_MODEL]
    h_ref[...] = (_rms(x, g_ref[...]) * (1.0 + scale) + shift).astype(BF16)

    def proj(r0, r1):
        return lax.dot_general(h_ref[...], w_ref[r0:r1, :], nt, preferred_element_type=F32)

    vn = _rms(jax.nn.gelu(proj(W_A, 2 * W_A)), sg_ref[...]).astype(BF16)
    front = jax.nn.gelu(proj(0, W_A)) * _silu(proj(2 * W_A, 3 * W_A))
    for c in range(ROW_TILE // CHUNK):
        r0, r1 = c * CHUNK, (c + 1) * CHUNK
        for g in range(A_GROUPS):
            c0, c1 = g * HEAD_DIM, (g + 1) * HEAD_DIM
            s = jnp.dot(ws_ref[g], vn[r0:r1, c0:c1], preferred_element_type=F32) + bs_ref[:, c0:c1]
            ma_ref[r0:r1, c0:c1] = (front[r0:r1, c0:c1] * s).astype(BF16)

    for c0 in range(0, Z_W, Z_CHUNK):
        c1 = min(c0 + Z_CHUNK, Z_W)
        z_ref[:, c0:c1] = proj(GM_W + c0, GM_W + c1).astype(BF16)


def _inproj(x_lat, x_ctx, ctx_block, mods_l, norm_g, w_in_p, sgu_g, w_s_b, b_s_exp):
    tm = ROW_TILE
    est = (3 * tm * D_MODEL * 4 + D_MODEL * W_ROWS * 2 + 2 * tm * (Z_W + W_A) * 2 + tm * D_MODEL * 2
           + 6 * tm * Z_CHUNK * 4)
    return pl.pallas_call(
        _inproj_kernel,
        grid=(ROW_TILES,),
        in_specs=[pl.BlockSpec((tm, D_MODEL), lambda t: (_lat_tile(t), 0)),
                  pl.BlockSpec((tm, D_MODEL), lambda t: (ctx_block, 0), pipeline_mode=pl.Buffered(1)),
                  pl.BlockSpec((1, 1, 3 * D_MODEL), lambda t: (_mod_row(t), 0, 0)),
                  pl.BlockSpec((1, D_MODEL), lambda t: (0, 0)),
                  pl.BlockSpec((W_ROWS, D_MODEL), lambda t: (0, 0), pipeline_mode=pl.Buffered(1)),
                  pl.BlockSpec((1, W_A), lambda t: (0, 0)),
                  pl.BlockSpec((A_GROUPS, CHUNK, CHUNK), lambda t: (0, 0, 0)),
                  pl.BlockSpec((CHUNK, W_A), lambda t: (0, 0))],
        out_specs=[pl.BlockSpec((tm, Z_W), lambda t: (t, 0)),
                   pl.BlockSpec((tm, W_A), lambda t: (t, 0))],
        out_shape=[jax.ShapeDtypeStruct((N_TOK, Z_W), BF16),
                   jax.ShapeDtypeStruct((N_TOK, W_A), BF16)],
        scratch_shapes=[pltpu.VMEM((tm, D_MODEL), BF16)],
        compiler_params=pltpu.CompilerParams(dimension_semantics=("parallel",),
                                             vmem_limit_bytes=_vmem_limit(est)),
    )(x_lat, x_ctx, mods_l.reshape(MOD_ROWS, 1, 3 * D_MODEL), norm_g.reshape(1, D_MODEL), w_in_p,
      sgu_g.reshape(1, W_A), w_s_b, b_s_exp)


QK_W = 2 * HEAD_DIM
UQ_W = B_HEADS * (NOPE_DIM + 2 * ROPE_DIM)
UKV_W = B_HEADS * (NOPE_DIM + V_DIM)
_Q_PRESCALE = MLA_SCALE * math.log2(math.e)
assert B_HEADS % 2 == 0 and 2 * ROPE_DIM == LANES


def _upproj_kernel(cq_ref, ckv_ref, kr_ref, cos_ref, sin_ref, qg_ref, kg_ref, wq_ref, wkv_ref,
                   q_ref, k_ref, vt_ref):
    cos2 = cos_ref[...]
    sin2 = sin_ref[...]
    n_rope = B_HEADS * ROPE_DIM
    cqn = _rms(cq_ref[...].astype(F32), qg_ref[...]).astype(BF16)
    q_all = jnp.dot(cqn, wq_ref[...], preferred_element_type=F32)
    rope0 = B_HEADS * NOPE_DIM
    for h in range(B_HEADS):
        q_ref[h, :, 0:NOPE_DIM] = (q_all[:, h * NOPE_DIM:(h + 1) * NOPE_DIM] * _Q_PRESCALE).astype(BF16)
    for j in range(B_HEADS // 2):
        a = q_all[:, rope0 + j * LANES: rope0 + (j + 1) * LANES]
        a_sw = q_all[:, rope0 + n_rope + j * LANES: rope0 + n_rope + (j + 1) * LANES]
        rot = ((a * cos2 + a_sw * sin2) * _Q_PRESCALE).astype(BF16)
        q_ref[2 * j, :, NOPE_DIM:QK_W] = rot
        q_ref[2 * j + 1, :, NOPE_DIM:QK_W] = rot

    ckvn = _rms(ckv_ref[...].astype(F32), kg_ref[...]).astype(BF16)
    kv_all = jnp.dot(ckvn, wkv_ref[...], preferred_element_type=F32)
    kr = kr_ref[...].astype(F32)
    krot = kr[:, 0:LANES] * cos2 + kr[:, LANES:2 * LANES] * sin2
    lane = lax.broadcasted_iota(jnp.int32, krot.shape, 1)
    k_lo = jnp.where(lane < ROPE_DIM, krot, 0.0).astype(BF16)
    k_hi = jnp.where(lane >= ROPE_DIM, krot, 0.0).astype(BF16)
    v0 = B_HEADS * NOPE_DIM
    for h in range(B_HEADS):
        k_ref[h, :, 0:NOPE_DIM] = kv_all[:, h * NOPE_DIM:(h + 1) * NOPE_DIM].astype(BF16)
        k_ref[h, :, NOPE_DIM:QK_W] = k_lo if h % 2 == 0 else k_hi
        v_h = kv_all[:, v0 + h * V_DIM: v0 + (h + 1) * V_DIM]
        for c in range(ROW_TILE // KV_CHUNK):
            vt_ref[h, c] = v_h[c * KV_CHUNK:(c + 1) * KV_CHUNK, :].T.astype(BF16)


def _upproj(z, cos2, sin2, qa_g, kva_g, w_uq_p, w_ukv_p):
    tm = ROW_TILE
    est = 2 * (2 * tm * Q_LORA * 2 + tm * 4 * ROPE_DIM * 2 + 2 * tm * LANES * 4 + Q_LORA * UQ_W * 2
               + KV_LORA * UKV_W * 2 + B_HEADS * tm * (2 * QK_W + V_DIM) * 2) + 4 * tm * UQ_W * 4
    return pl.pallas_call(
        _upproj_kernel,
        grid=(ROW_TILES,),
        in_specs=[pl.BlockSpec((tm, Q_LORA), lambda t: (t, OFF_CQ // Q_LORA)),
                  pl.BlockSpec((tm, KV_LORA), lambda t: (t, OFF_CKV // KV_LORA)),
                  pl.BlockSpec((tm, 4 * ROPE_DIM), lambda t: (t, OFF_KR // (4 * ROPE_DIM))),
                  pl.BlockSpec((tm, LANES), lambda t: (_rope_tile(t), 0)),
                  pl.BlockSpec((tm, LANES), lambda t: (_rope_tile(t), 0)),
                  pl.BlockSpec((1, Q_LORA), lambda t: (0, 0)),
                  pl.BlockSpec((1, KV_LORA), lambda t: (0, 0)),
                  pl.BlockSpec((Q_LORA, UQ_W), lambda t: (0, 0)),
                  pl.BlockSpec((KV_LORA, UKV_W), lambda t: (0, 0))],
        out_specs=[pl.BlockSpec((B_HEADS, tm, QK_W), lambda t: (0, t, 0)),
                   pl.BlockSpec((B_HEADS, tm, QK_W), lambda t: (0, t, 0)),
                   pl.BlockSpec((B_HEADS, tm // KV_CHUNK, V_DIM, KV_CHUNK), lambda t: (0, t, 0, 0))],
        out_shape=[jax.ShapeDtypeStruct((B_HEADS, N_TOK, QK_W), BF16),
                   jax.ShapeDtypeStruct((B_HEADS, N_TOK, QK_W), BF16),
                   jax.ShapeDtypeStruct((B_HEADS, N_TOK // KV_CHUNK, V_DIM, KV_CHUNK), BF16)],
        compiler_params=pltpu.CompilerParams(dimension_semantics=("parallel",),
                                             vmem_limit_bytes=_vmem_limit(est)),
    )(z, z, z, cos2, sin2, qa_g.reshape(1, Q_LORA), kva_g.reshape(1, KV_LORA), w_uq_p, w_ukv_p)


MLA_TK = 512
MLA_TQ = 1024
MLA_TQG = 256
assert SEQ % MLA_TK == 0 and MLA_TK % KV_CHUNK == 0 and SEQ % MLA_TQ == 0


def _attend_t(qs, chunks):
    def scores(k, q):
        return lax.dot_general(k, q, (((1,), (1,)), ((), ())), preferred_element_type=F32)

    state = [None] * len(qs)
    s_next = [scores(chunks[0][0], q) for q in qs]
    for j, (_, vts) in enumerate(chunks):
        for g, q in enumerate(qs):
            s = s_next[g]
            if j + 1 < len(chunks):
                s_next[g] = scores(chunks[j + 1][0], q)
            s_max = jnp.max(s, axis=0, keepdims=True)
            m_new = s_max if state[g] is None else jnp.maximum(state[g][0], s_max)
            p_sum = pv = None
            for n, vt in enumerate(vts):
                p = jnp.exp2(s[n * KV_CHUNK:(n + 1) * KV_CHUNK] - m_new)
                ps = jnp.sum(p, axis=0, keepdims=True)
                d = jnp.dot(vt, p.astype(BF16), preferred_element_type=F32)
                p_sum, pv = (ps, d) if n == 0 else (p_sum + ps, pv + d)
            if state[g] is None:
                state[g] = (m_new, p_sum, pv)
            else:
                m, l, acc = state[g]
                alpha = jnp.exp2(m - m_new)
                state[g] = (m_new, alpha * l + p_sum, alpha * acc + pv)
    return [acc / l for _, l, acc in state]


def _mla_lat_kernel(q_ref, kl_ref, vtl_ref, kc_ref, vtc_ref, gb_ref, o_ref):
    per = MLA_TK // KV_CHUNK
    chunks = [(kl_ref[0, j * MLA_TK:(j + 1) * MLA_TK, :], [vtl_ref[0, j * per + n] for n in range(per)])
              for j in range(SEQ // MLA_TK)]
    chunks.append((kc_ref[0], [vtc_ref[0, n] for n in range(CTX_LEN // KV_CHUNK)]))
    groups = [(g * MLA_TQG, (g + 1) * MLA_TQG) for g in range(MLA_TQ // MLA_TQG)]
    outs = _attend_t([q_ref[0, lo:hi, :] for lo, hi in groups], chunks)
    for (lo, hi), o_t in zip(groups, outs):
        o_ref[lo:hi, :] = (o_t.T * _silu(gb_ref[lo:hi, :].astype(F32))).astype(BF16)


def _mla_ctx_kernel(q_ref, kc_ref, vtc_ref, gb_ref, o_ref):
    (o_t,) = _attend_t([q_ref[0]], [(kc_ref[0], [vtc_ref[0, n] for n in range(CTX_LEN // KV_CHUNK)])])
    o = o_t.T
    o_ref[...] = (o * _silu(gb_ref[...].astype(F32))).astype(BF16)


def _mla(q, k, vt, z, with_ctx_queries):
    ctx_tile = N_LAT // CTX_LEN
    tq = MLA_TQ
    nq = SEQ // tq
    est = 2 * (tq * QK_W * 2 + SEQ * QK_W * 2 + SEQ * V_DIM * 2 + CTX_LEN * (QK_W + V_DIM) * 2
               + 2 * tq * HEAD_DIM * 2) + 8 * MLA_TK * tq * 4
    mb = pl.pallas_call(
        _mla_lat_kernel,
        grid=(BATCH, B_HEADS, nq),
        in_specs=[pl.BlockSpec((1, tq, QK_W), lambda b, h, i: (h, b * nq + i, 0)),
                  pl.BlockSpec((1, SEQ, QK_W), lambda b, h, i: (h, b, 0)),
                  pl.BlockSpec((1, SEQ // KV_CHUNK, V_DIM, KV_CHUNK), lambda b, h, i: (h, b, 0, 0)),
                  pl.BlockSpec((1, CTX_LEN, QK_W), lambda b, h, i: (h, ctx_tile + b, 0)),
                  pl.BlockSpec((1, CTX_LEN // KV_CHUNK, V_DIM, KV_CHUNK), lambda b, h, i: (h, ctx_tile + b, 0, 0)),
                  pl.BlockSpec((tq, HEAD_DIM), lambda b, h, i: (b * nq + i, OFF_GB // HEAD_DIM + h))],
        out_specs=pl.BlockSpec((tq, HEAD_DIM), lambda b, h, i: (b * nq + i, h)),
        out_shape=jax.ShapeDtypeStruct((N_LAT, W_B), BF16),
        compiler_params=pltpu.CompilerParams(dimension_semantics=("parallel", "parallel", "parallel"),
                                             vmem_limit_bytes=_vmem_limit(est)),
    )(q, k, vt, k, vt, z)
    if not with_ctx_queries:
        return mb, None
    mb_ctx = pl.pallas_call(
        _mla_ctx_kernel,
        grid=(BATCH, B_HEADS),
        in_specs=[pl.BlockSpec((1, CTX_LEN, QK_W), lambda b, h: (h, ctx_tile + b, 0)),
                  pl.BlockSpec((1, CTX_LEN, QK_W), lambda b, h: (h, ctx_tile + b, 0)),
                  pl.BlockSpec((1, CTX_LEN // KV_CHUNK, V_DIM, KV_CHUNK), lambda b, h: (h, ctx_tile + b, 0, 0)),
                  pl.BlockSpec((CTX_LEN, HEAD_DIM), lambda b, h: (ctx_tile + b, OFF_GB // HEAD_DIM + h))],
        out_specs=pl.BlockSpec((CTX_LEN, HEAD_DIM), lambda b, h: (b, h)),
        out_shape=jax.ShapeDtypeStruct((N_CTX, W_B), BF16),
        compiler_params=pltpu.CompilerParams(dimension_semantics=("parallel", "parallel")),
    )(q, k, vt, z)
    return mb, mb_ctx


NA_QROWS = Q_TILE // GRID_W
NA_KROWS = 12
NA_KEYS = NA_KROWS * GRID_W
assert NA_QROWS + MAX_KH <= NA_KROWS + 1 and ROWS >= NA_KROWS


def _na_strip_row(j):
    return int(np.clip(j * NA_QROWS - MAX_KH // 2, 0, ROWS - NA_KROWS))


def _na_tables():
    def one(j):
        s = _na_strip_row(j)
        valid = np.zeros((NA_QROWS, NA_KROWS), bool)
        d = np.zeros((NA_QROWS, NA_KROWS), np.int32)
        for a in range(NA_QROWS):
            r = j * NA_QROWS + a
            r0 = int(np.clip(r - MAX_KH // 2, 0, ROWS - MAX_KH))
            for i in range(NA_KROWS):
                valid[a, i] = 0 <= s + i - r0 < MAX_KH
                d[a, i] = s + i - r + (MAX_KH - 1) if valid[a, i] else 0
        return valid, d
    n_tiles = ROWS // NA_QROWS
    first, mid, last = one(0), one(1), one(n_tiles - 1)
    for j in range(1, n_tiles - 1):
        v, d = one(j)
        assert (v == mid[0]).all() and (d == mid[1]).all()
    return np.stack([first[0], mid[0], last[0]]), np.stack([first[1], mid[1], last[1]])


_NA_VALID, _NA_DROW = _na_tables()


def _na_bias_tables(rpb_l):
    col = np.arange(GRID_W)
    c0 = np.clip(col - KW // 2, 0, GRID_W - KW)
    col_ok = (col[None, :] >= c0[:, None]) & (col[None, :] < c0[:, None] + KW)
    dc = np.clip(col[None, :] - col[:, None], -(KW - 1), KW - 1) + (KW - 1)
    onehot = (dc.reshape(-1)[None, :] == np.arange(2 * KW - 1)[:, None]).astype(np.float32)
    t1 = jnp.dot(rpb_l.reshape(C_HEADS * (2 * MAX_KH - 1), 2 * KW - 1), jnp.asarray(onehot),
                 precision=lax.Precision.HIGHEST).reshape(C_HEADS, 2 * MAX_KH - 1, GRID_W, GRID_W)
    t1 = jnp.where(jnp.asarray(col_ok)[None, None], t1, NEG)
    return jnp.concatenate([t1, t1], axis=-1)


def _softmax_rows(blocks):
    m = functools.reduce(jnp.maximum, [jnp.max(s, axis=-1, keepdims=True) for s in blocks])
    ps = [jnp.exp(s - m) for s in blocks]
    return ps, functools.reduce(jnp.add, [jnp.sum(p, axis=-1, keepdims=True) for p in ps])


def _assemble_bias(t_ref, bias_ref):
    lane = lax.broadcasted_iota(jnp.int32, (GRID_W, 2 * GRID_W), 1)
    neg = jnp.full((GRID_W, 2 * GRID_W), NEG, F32)
    for cls in range(3):
        for a in range(NA_QROWS):
            for pair in range(NA_KROWS // 2):
                halves = [t_ref[0, int(_NA_DROW[cls, a, i])] if _NA_VALID[cls, a, i] else neg
                          for i in (2 * pair, 2 * pair + 1)]
                tile = halves[0] if halves[0] is halves[1] else jnp.where(lane < GRID_W, halves[0], halves[1])
                bias_ref[cls, a * GRID_W:(a + 1) * GRID_W, pair * 2 * GRID_W:(pair + 1) * 2 * GRID_W] = tile


def _natten_kernel(q_ref, kl_ref, vl_ref, kc_ref, vc_ref, t_ref, g_ref, o_ref, bias_ref):
    nt = (((1,), (1,)), ((), ()))
    n_tiles = ROWS // NA_QROWS
    _assemble_bias(t_ref, bias_ref)

    def tile(j, carry):
        rows = pl.ds(pl.multiple_of(j * Q_TILE, Q_TILE), Q_TILE)
        strip_row = jnp.clip(j * NA_QROWS - MAX_KH // 2, 0, ROWS - NA_KROWS)
        strip = pl.ds(pl.multiple_of(strip_row * GRID_W, GRID_W), NA_KEYS)
        cls = jnp.where(j == 0, 0, jnp.where(j == n_tiles - 1, 2, 1))
        q = q_ref[rows, :]
        s_nb = lax.dot_general(q, kl_ref[strip, :], nt, preferred_element_type=F32) * C_SCALE + bias_ref[cls]
        s_cx = lax.dot_general(q, kc_ref[...], nt, preferred_element_type=F32) * C_SCALE
        (p_nb, p_cx), l = _softmax_rows([s_nb, s_cx])
        o = jnp.dot(p_nb.astype(BF16), vl_ref[strip, :], preferred_element_type=F32)
        o += jnp.dot(p_cx.astype(BF16), vc_ref[...], preferred_element_type=F32)
        o_ref[rows, :] = (o / l * _silu(g_ref[rows, :].astype(F32))).astype(BF16)
        return carry

    lax.fori_loop(0, n_tiles, tile, 0, unroll=2)


def _natten_ctx_kernel(q_ref, kc_ref, vc_ref, g_ref, o_ref):
    s =lax.dot_general(q_ref[...], kc_ref[...], (((1,), (1,)), ((), ())), preferred_element_type=F32) * C_SCALE
    (p,), l = _softmax_rows([s])
    o = jnp.dot(p.astype(BF16), vc_ref[...], preferred_element_type=F32)
    o_ref[...] = (o / l * _silu(g_ref[...].astype(F32))).astype(BF16)


def _natten(z, bias_tab, with_ctx_queries):
    ctx_tile = N_LAT // CTX_LEN
    hd = HEAD_DIM
    est = (2 * (5 * SEQ * hd * 2 + 2 * CTX_LEN * hd * 2 + 3 * Q_TILE * NA_KEYS * 4)
           + 12 * Q_TILE * (NA_KEYS + CTX_LEN) * 4)
    mc = pl.pallas_call(
        _natten_kernel,
        grid=(BATCH, C_HEADS),
        in_specs=[pl.BlockSpec((SEQ, hd), lambda b, h: (b, OFF_QC // hd + h)),
                  pl.BlockSpec((SEQ, hd), lambda b, h: (b, OFF_KC // hd + h)),
                  pl.BlockSpec((SEQ, hd), lambda b, h: (b, OFF_VC // hd + h)),
                  pl.BlockSpec((CTX_LEN, hd), lambda b, h: (ctx_tile + b, OFF_KC // hd + h)),
                  pl.BlockSpec((CTX_LEN, hd), lambda b, h: (ctx_tile + b, OFF_VC // hd + h)),
                  pl.BlockSpec((1, 2 * MAX_KH - 1, GRID_W, 2 * GRID_W), lambda b, h: (h, 0, 0, 0)),
                  pl.BlockSpec((SEQ, hd), lambda b, h: (b, OFF_GC // hd + h))],
        out_specs=pl.BlockSpec((SEQ, hd), lambda b, h: (b, h)),
        out_shape=jax.ShapeDtypeStruct((N_LAT, W_C), BF16),
        scratch_shapes=[pltpu.VMEM((3, Q_TILE, NA_KEYS), F32)],
        compiler_params=pltpu.CompilerParams(dimension_semantics=("parallel", "parallel"),
                                             vmem_limit_bytes=_vmem_limit(est)),
    )(z, z, z, z, z, bias_tab, z)
    if not with_ctx_queries:
        return mc, None
    mc_ctx = pl.pallas_call(
        _natten_ctx_kernel,
        grid=(BATCH, C_HEADS),
        in_specs=[pl.BlockSpec((CTX_LEN, hd), lambda b, h: (ctx_tile + b, OFF_QC // hd + h)),
                  pl.BlockSpec((CTX_LEN, hd), lambda b, h: (ctx_tile + b, OFF_KC // hd + h)),
                  pl.BlockSpec((CTX_LEN, hd), lambda b, h: (ctx_tile + b, OFF_VC // hd + h)),
                  pl.BlockSpec((CTX_LEN, hd), lambda b, h: (ctx_tile + b, OFF_GC // hd + h))],
        out_specs=pl.BlockSpec((CTX_LEN, hd), lambda b, h: (b, h)),
        out_shape=jax.ShapeDtypeStruct((N_CTX, W_C), BF16),
        compiler_params=pltpu.CompilerParams(dimension_semantics=("parallel", "parallel")),
    )(z, z, z, z)
    return mc, mc_ctx


OUT_CHUNK = 512


def _outproj_body(ma, mb, mc, x, w_ref, mod_ref, o_ref):
    gate = mod_ref[0, :, 2 * D_MODEL:3 * D_MODEL]
    for c0 in range(0, D_MODEL, OUT_CHUNK):
        c1 = c0 + OUT_CHUNK
        y = jnp.dot(ma, w_ref[0:W_A, c0:c1], preferred_element_type=F32)
        y += jnp.dot(mb, w_ref[W_A:W_A + W_B, c0:c1], preferred_element_type=F32)
        y += jnp.dot(mc, w_ref[W_A + W_B:D_MODEL, c0:c1], preferred_element_type=F32)
        o_ref[:, c0:c1] = x[:, c0:c1] + gate[:, c0:c1] * y


def _outproj_mid_kernel(ma_ref, mbl_ref, mbc_ref, mcl_ref, mcc_ref, xl_ref, xc_ref, w_ref, mod_ref, o_ref):
    _outproj_body(ma_ref[...], _pick_rows(mbl_ref, mbc_ref), _pick_rows(mcl_ref, mcc_ref),
                  _pick_rows(xl_ref, xc_ref), w_ref, mod_ref, o_ref)


def _outproj_final_kernel(ma_ref, mb_ref, mc_ref, x_ref, w_ref, mod_ref, fg_ref, o_ref):
    _outproj_body(ma_ref[...], mb_ref[...], mc_ref[...], x_ref[...], w_ref, mod_ref, o_ref)
    o_ref[...] = _rms(o_ref[...], fg_ref[...])


def _outproj_mid(ma, mb, mb_ctx, mc, mc_ctx, x_lat, x_ctx, ctx_block, w_out_b, mods_l):
    tm = ROW_TILE
    est = (3 * tm * D_MODEL * 2 + D_MODEL * D_MODEL * 2 + 5 * tm * D_MODEL * 4 + 3 * tm * OUT_CHUNK * 4)
    lat = lambda w: pl.BlockSpec((tm, w), lambda t: (_lat_tile(t), 0))
    ctx = lambda w, blk: pl.BlockSpec((tm, w), lambda t: (blk, 0), pipeline_mode=pl.Buffered(1))
    return pl.pallas_call(
        _outproj_mid_kernel,
        grid=(ROW_TILES,),
        in_specs=[pl.BlockSpec((tm, W_A), lambda t: (t, 0)),
                  lat(W_B), ctx(W_B, 0), lat(W_C), ctx(W_C, 0), lat(D_MODEL), ctx(D_MODEL, ctx_block),
                  pl.BlockSpec((D_MODEL, D_MODEL), lambda t: (0, 0), pipeline_mode=pl.Buffered(1)),
                  pl.BlockSpec((1, 1, 3 * D_MODEL), lambda t: (_mod_row(t), 0, 0))],
        out_specs=pl.BlockSpec((tm, D_MODEL), lambda t: (t, 0)),
        out_shape=jax.ShapeDtypeStruct((N_TOK, D_MODEL), F32),
        compiler_params=pltpu.CompilerParams(dimension_semantics=("parallel",),
                                             vmem_limit_bytes=_vmem_limit(est)),
    )(ma, mb, mb_ctx, mc, mc_ctx, x_lat, x_ctx, w_out_b, mods_l.reshape(MOD_ROWS, 1, 3 * D_MODEL))


def _outproj_final(ma, mb, mc, x_lat, w_out_b, mods_l, final_g):
    tm = ROW_TILE
    est = (2 * tm * D_MODEL * 2 + D_MODEL * D_MODEL * 2 + 4 * tm * D_MODEL * 4 + 3 * tm * OUT_CHUNK * 4)
    return pl.pallas_call(
        _outproj_final_kernel,
        grid=(LAT_TILES,),
        in_specs=[pl.BlockSpec((tm, W_A), lambda t: (t, 0)),
                  pl.BlockSpec((tm, W_B), lambda t: (t, 0)),
                  pl.BlockSpec((tm, W_C), lambda t: (t, 0)),
                  pl.BlockSpec((tm, D_MODEL), lambda t: (t, 0)),
                  pl.BlockSpec((D_MODEL, D_MODEL), lambda t: (0, 0), pipeline_mode=pl.Buffered(1)),
                  pl.BlockSpec((1, 1, 3 * D_MODEL), lambda t: (_mod_row(t), 0, 0)),
                  pl.BlockSpec((1, D_MODEL), lambda t: (0, 0))],
        out_specs=pl.BlockSpec((tm, D_MODEL), lambda t: (t, 0)),
        out_shape=jax.ShapeDtypeStruct((N_LAT, D_MODEL), F32),
        compiler_params=pltpu.CompilerParams(dimension_semantics=("parallel",),
                                             vmem_limit_bytes=_vmem_limit(est)),
    )(ma, mb, mc, x_lat, w_out_b, mods_l.reshape(MOD_ROWS, 1, 3 * D_MODEL), final_g.reshape(1, D_MODEL))


def _cols(w, name):
    a, b = _SRC[name]
    return w[:, a:b]


W_PREP_ROWS = 256


def _prep_w_in_kernel(w_ref, kr_ref, o_ref):
    h0, h1 = _SRC["u"][0], _SRC["ckv"][1]
    t0, t1 = _SRC["gb"][0], _SRC["gc"][1]
    o_ref[0:h1 - h0, :] = w_ref[h0:h1, :].astype(BF16)
    o_ref[GM_W + OFF_GB:GM_W + OFF_KR, :] = w_ref[t0:t1, :].astype(BF16)
    o_ref[GM_W + OFF_KR:W_ROWS, :] = kr_ref[...]


def _prep_w_in(w_in, l):
    assert _SRC["ckv"][1] == GM_W + OFF_GB and OFF_GB + (_SRC["gc"][1] - _SRC["gb"][0]) == OFF_KR
    w_t = jnp.swapaxes(w_in, 1, 2)
    kr = w_t[l, _SRC["kr"][0]:_SRC["kr"][1], :]
    kr_sw = jnp.concatenate([kr[_Q4:2 * _Q4], kr[0:_Q4], kr[3 * _Q4:4 * _Q4], kr[2 * _Q4:3 * _Q4]], axis=0)
    kr4 = jnp.concatenate([kr, kr, kr_sw, kr_sw], axis=0).astype(BF16)
    cols = W_PREP_ROWS
    est = 2 * cols * (IN_W * 4 + W_ROWS * 2 + 4 * ROPE_DIM * 2) + 2 * cols * IN_W * 4
    return pl.pallas_call(
        _prep_w_in_kernel,
        grid=(D_MODEL // cols,),
        in_specs=[pl.BlockSpec((IN_W, cols), lambda i: (l, i)),
                  pl.BlockSpec((4 * ROPE_DIM, cols), lambda i: (0, i))],
        out_specs=pl.BlockSpec((W_ROWS, cols), lambda i: (0, i)),
        out_shape=jax.ShapeDtypeStruct((W_ROWS, D_MODEL), BF16),
        compiler_params=pltpu.CompilerParams(dimension_semantics=("parallel",),
                                             vmem_limit_bytes=_vmem_limit(est)),
    )(w_t.reshape(DEPTH * IN_W, D_MODEL), kr4)


def _cast_kernel(w_ref, o_ref):
    o_ref[...] = w_ref[...].astype(BF16)


def _prep_w_out(w_out, l):
    rows = W_PREP_ROWS
    return pl.pallas_call(
        _cast_kernel,
        grid=(MIX_W // rows,),
        in_specs=[pl.BlockSpec((rows, D_MODEL), lambda i: (l * (MIX_W // rows) + i, 0))],
        out_specs=pl.BlockSpec((rows, D_MODEL), lambda i: (i, 0)),
        out_shape=jax.ShapeDtypeStruct((MIX_W, D_MODEL), BF16),
        compiler_params=pltpu.CompilerParams(dimension_semantics=("parallel",)),
    )(w_out.reshape(DEPTH * MIX_W, D_MODEL))


def _prep_w_uq(w):
    w3 = w.reshape(Q_LORA, B_HEADS, NOPE_DIM + ROPE_DIM)
    nope = w3[:, :, :NOPE_DIM].reshape(Q_LORA, B_HEADS * NOPE_DIM)
    rope = w3[:, :, NOPE_DIM:]
    rope_sw = jnp.concatenate([rope[..., _Q4:2 * _Q4], rope[..., 0:_Q4],
                               rope[..., 3 * _Q4:4 * _Q4], rope[..., 2 * _Q4:3 * _Q4]], axis=-1)
    return jnp.concatenate([nope, rope.reshape(Q_LORA, -1), rope_sw.reshape(Q_LORA, -1)], axis=1).astype(BF16)


def _prep_w_ukv(w):
    w3 = w.reshape(KV_LORA, B_HEADS, NOPE_DIM + V_DIM)
    return jnp.concatenate([w3[:, :, :NOPE_DIM].reshape(KV_LORA, -1),
                            w3[:, :, NOPE_DIM:].reshape(KV_LORA, -1)], axis=1).astype(BF16)


def _rope_tables():
    t = np.arange(SEQ)
    row = (t // GRID_W).astype(np.float32)
    col = (t % GRID_W).astype(np.float32)
    inv = (np.float32(ROPE_THETA) ** (-np.arange(_Q4, dtype=np.float32) / np.float32(_Q4))).astype(np.float32)
    ar, ac = row[:, None] * inv, col[:, None] * inv
    cos = np.concatenate([np.cos(ar), np.cos(ar), np.cos(ac), np.cos(ac)], axis=1)
    sin = np.concatenate([-np.sin(ar), np.sin(ar), -np.sin(ac), np.sin(ac)], axis=1)
    cos = np.concatenate([cos, np.ones((ROW_TILE, ROPE_DIM))], axis=0).astype(np.float32)
    sin = np.concatenate([sin, np.zeros((ROW_TILE, ROPE_DIM))], axis=0).astype(np.float32)
    return np.concatenate([cos, cos], axis=1), np.concatenate([sin, sin], axis=1)


_COS2, _SIN2 = _rope_tables()


def _rope_tile(t):
    return jnp.where(t < LAT_TILES, t % TILES_PER_SAMPLE, TILES_PER_SAMPLE)


def kernel(x, c, ctx, c_ctx, w_ada, b_ada, norm_g, w_in, qa_g, kva_g, w_uq, w_ukv, sgu_g, w_s, b_s, rpb,
           w_out, final_g):
    assert x.shape == (BATCH, SEQ, D_MODEL) and ctx.shape == (BATCH, CTX_LEN, D_MODEL)
    assert w_in.shape == (DEPTH, D_MODEL, IN_W)
    cc = jnp.concatenate([c, c_ctx[None, :], jnp.zeros((MOD_ROWS - BATCH - 1, D_MODEL), F32)], axis=0)
    mods = _modulation(cc, w_ada, b_ada)
    cos2, sin2 = jnp.asarray(_COS2), jnp.asarray(_SIN2)

    x_lat, x_ctx, ctx_block = x.reshape(N_LAT, D_MODEL), ctx.reshape(N_CTX, D_MODEL), 0
    for l in range(DEPTH):
        last = l == DEPTH - 1
        b_s_exp = jnp.repeat(b_s[l].T, HEAD_DIM, axis=1)
        z, ma = _inproj(x_lat, x_ctx, ctx_block, mods[l], norm_g[l], _prep_w_in(w_in, l),
                        sgu_g[l], w_s[l].astype(BF16), b_s_exp)
        q, k, vt = _upproj(z, cos2, sin2, qa_g[l], kva_g[l], _prep_w_uq(w_uq[l]), _prep_w_ukv(w_ukv[l]))
        mb, mb_ctx = _mla(q, k, vt, z, with_ctx_queries=not last)
        mc, mc_ctx = _natten(z, _na_bias_tables(rpb[l]), with_ctx_queries=not last)
        w_out_b = _prep_w_out(w_out, l)
        if last:
            out = _outproj_final(ma, mb, mc, x_lat, w_out_b, mods[l], final_g)
            return out.reshape(BATCH, SEQ, D_MODEL)
        xf = _outproj_mid(ma, mb, mb_ctx, mc, mc_ctx, x_lat, x_ctx, ctx_block, w_out_b, mods[l])
        x_lat, x_ctx, ctx_block = xf, xf, LAT_TILES
```

```python
import functools
import math

import numpy as np
import jax
import jax.numpy as jnp
from jax import lax
from jax.experimental import pallas as pl
from jax.experimental.pallas import tpu as pltpu

D_MODEL = 2048
BATCH = 2
SEQ = 4096
DEPTH = 2
GRID_W = 64
CTX_LEN = 256
EPS = 1e-6
NEG = -1e30
HEAD_DIM = 128
W_A = D_MODEL // 4
W_B = D_MODEL // 2
W_C = D_MODEL // 4
MIX_W = W_A + W_B + W_C
CHUNK = 128
A_GROUPS = W_A // HEAD_DIM
B_HEADS = W_B // HEAD_DIM
Q_LORA = D_MODEL // 4
KV_LORA = 512
NOPE_DIM = 128
ROPE_DIM = 64
V_DIM = 128
MLA_SCALE = (NOPE_DIM + ROPE_DIM) ** -0.5
ROPE_THETA = 10000.0
C_HEADS = W_C // HEAD_DIM
MAX_KH = 8
KW = 16
C_SCALE = HEAD_DIM ** -0.5
ROWS = SEQ // GRID_W

LANES = 128
MXU_DIM = 256
VMEM_BYTES_V7X = 64 * 1024 * 1024
VMEM_LIMIT_CAP = 56 * 1024 * 1024

F32 = jnp.float32
BF16 = jnp.bfloat16

N_LAT = BATCH * SEQ
N_CTX = BATCH * CTX_LEN
N_TOK = N_LAT + N_CTX
ROW_TILE = N_CTX
LAT_TILES = N_LAT // ROW_TILE
ROW_TILES = N_TOK // ROW_TILE
TILES_PER_SAMPLE = SEQ // ROW_TILE
Q_TILE = CTX_LEN
Q_TILES_PER_SAMPLE = SEQ // Q_TILE
KV_CHUNK = 256
assert ROW_TILE % CHUNK == 0 and SEQ % ROW_TILE == 0 and SEQ % Q_TILE == 0

GM_W = 3 * W_A
OFF_CQ = 0
OFF_CKV = OFF_CQ + Q_LORA
OFF_GB = OFF_CKV + KV_LORA
OFF_QC = OFF_GB + W_B
OFF_KC = OFF_QC + W_C
OFF_VC = OFF_KC + W_C
OFF_GC = OFF_VC + W_C
OFF_KR = OFF_GC + W_C
Z_W = OFF_KR + 4 * ROPE_DIM
W_ROWS = GM_W + Z_W
Z_CHUNK = 512
assert W_ROWS % MXU_DIM == 0 and OFF_KR % (4 * ROPE_DIM) == 0 and Z_CHUNK == W_A

_SRC = {}
_acc = 0
for _name, _w in (("u", W_A), ("va", W_A), ("ga", W_A), ("cq", Q_LORA), ("ckv", KV_LORA), ("kr", ROPE_DIM),
                  ("gb", W_B), ("qc", W_C), ("kc", W_C), ("vc", W_C), ("gc", W_C)):
    _SRC[_name] = (_acc, _acc + _w)
    _acc += _w
IN_W = _acc

_Q4 = ROPE_DIM // 4
_SWAP = np.concatenate([np.arange(_Q4, 2 * _Q4), np.arange(0, _Q4),
                        np.arange(3 * _Q4, 4 * _Q4), np.arange(2 * _Q4, 3 * _Q4)])


def _vmem_limit(nbytes):
    return int(min(VMEM_LIMIT_CAP, max(16 * 1024 * 1024, nbytes * 5 // 4)))


def _silu(x):
    return x * jax.nn.sigmoid(x)


def _rms(x, g):
    return x * lax.rsqrt(jnp.mean(x * x, axis=-1, keepdims=True) + EPS) * g


MOD_ROWS = 8
MOD_TN = 768


def _mod_kernel(c_ref, w_ref, b_ref, o_ref):
    a = _silu(c_ref[...])
    a_hi = a.astype(BF16)
    a_lo = (a - a_hi.astype(F32)).astype(BF16)
    w = w_ref[0]
    w_hi = w.astype(BF16)
    w_lo = (w - w_hi.astype(F32)).astype(BF16)
    r = jnp.dot(jnp.concatenate([a_hi, a_lo], axis=0), w_hi, preferred_element_type=F32)
    r_lo = jnp.dot(a_hi, w_lo, preferred_element_type=F32)
    o_ref[0] = r[0:MOD_ROWS] + r[MOD_ROWS:2 * MOD_ROWS] + r_lo + b_ref[0]


def _modulation(cc, w_ada, b_ada):
    n = 3 * D_MODEL
    est = 2 * (MOD_ROWS * D_MODEL * 4 + D_MODEL * MOD_TN * 4 + 2 * MOD_ROWS * MOD_TN * 4)
    return pl.pallas_call(
        _mod_kernel,
        grid=(DEPTH, n // MOD_TN),
        in_specs=[pl.BlockSpec((MOD_ROWS, D_MODEL), lambda l, j: (0, 0)),
                  pl.BlockSpec((1, D_MODEL, MOD_TN), lambda l, j: (l, 0, j)),
                  pl.BlockSpec((1, 1, MOD_TN), lambda l, j: (l, 0, j))],
        out_specs=pl.BlockSpec((1, MOD_ROWS, MOD_TN), lambda l, j: (l, 0, j)),
        out_shape=jax.ShapeDtypeStruct((DEPTH, MOD_ROWS, n), F32),
        compiler_params=pltpu.CompilerParams(dimension_semantics=("parallel", "parallel"),
                                             vmem_limit_bytes=_vmem_limit(est)),
    )(cc, w_ada, b_ada.reshape(DEPTH, 1, n))


def _mod_row(t):
    return jnp.where(t < LAT_TILES, t // TILES_PER_SAMPLE, BATCH)


def _lat_tile(t):
    return jnp.minimum(t, LAT_TILES - 1)


def _pick_rows(lat_ref, ctx_ref):
    return jnp.where(pl.program_id(0) < LAT_TILES, lat_ref[...], ctx_ref[...])


def _inproj_kernel(xl_ref, xc_ref, mod_ref, g_ref, w_ref, sg_ref, ws_ref, bs_ref, z_ref, ma_ref, h_ref):
    nt = (((1,), (1,)), ((), ()))
    x = _pick_rows(xl_ref, xc_ref)
    shift = mod_ref[0, :, 0:D_MODEL]
    scale = mod_ref[0, :, D_MODEL:2 * D_MODEL]
    h_ref[...] = (_rms(x, g_ref[...]) * (1.0 + scale) + shift).astype(BF16)

    def proj(r0, r1):
        return lax.dot_general(h_ref[...], w_ref[r0:r1, :], nt, preferred_element_type=F32)

    vn = _rms(jax.nn.gelu(proj(W_A, 2 * W_A)), sg_ref[...]).astype(BF16)
    front = jax.nn.gelu(proj(0, W_A)) * _silu(proj(2 * W_A, 3 * W_A))
    for c in range(ROW_TILE // CHUNK):
        r0, r1 = c * CHUNK, (c + 1) * CHUNK
        for g in range(A_GROUPS):
            c0, c1 = g * HEAD_DIM, (g + 1) * HEAD_DIM
            s = jnp.dot(ws_ref[g], vn[r0:r1, c0:c1], preferred_element_type=F32) + bs_ref[:, c0:c1]
            ma_ref[r0:r1, c0:c1] = (front[r0:r1, c0:c1] * s).astype(BF16)

    for c0 in range(0, Z_W, Z_CHUNK):
        c1 = min(c0 + Z_CHUNK, Z_W)
        z_ref[:, c0:c1] = proj(GM_W + c0, GM_W + c1).astype(BF16)


def _inproj(x_lat, x_ctx, ctx_block, mods_l, norm_g, w_in_p, sgu_g, w_s_b, b_s_exp):
    tm = ROW_TILE
    est = (3 * tm * D_MODEL * 4 + D_MODEL * W_ROWS * 2 + 2 * tm * (Z_W + W_A) * 2 + tm * D_MODEL * 2
           + 6 * tm * Z_CHUNK * 4)
    return pl.pallas_call(
        _inproj_kernel,
        grid=(ROW_TILES,),
        in_specs=[pl.BlockSpec((tm, D_MODEL), lambda t: (_lat_tile(t), 0)),
                  pl.BlockSpec((tm, D_MODEL), lambda t: (ctx_block, 0), pipeline_mode=pl.Buffered(1)),
                  pl.BlockSpec((1, 1, 3 * D_MODEL), lambda t: (_mod_row(t), 0, 0)),
                  pl.BlockSpec((1, D_MODEL), lambda t: (0, 0)),
                  pl.BlockSpec((W_ROWS, D_MODEL), lambda t: (0, 0), pipeline_mode=pl.Buffered(1)),
                  pl.BlockSpec((1, W_A), lambda t: (0, 0)),
                  pl.BlockSpec((A_GROUPS, CHUNK, CHUNK), lambda t: (0, 0, 0)),
                  pl.BlockSpec((CHUNK, W_A), lambda t: (0, 0))],
        out_specs=[pl.BlockSpec((tm, Z_W), lambda t: (t, 0)),
                   pl.BlockSpec((tm, W_A), lambda t: (t, 0))],
        out_shape=[jax.ShapeDtypeStruct((N_TOK, Z_W), BF16),
                   jax.ShapeDtypeStruct((N_TOK, W_A), BF16)],
        scratch_shapes=[pltpu.VMEM((tm, D_MODEL), BF16)],
        compiler_params=pltpu.CompilerParams(dimension_semantics=("parallel",),
                                             vmem_limit_bytes=_vmem_limit(est)),
    )(x_lat, x_ctx, mods_l.reshape(MOD_ROWS, 1, 3 * D_MODEL), norm_g.reshape(1, D_MODEL), w_in_p,
      sgu_g.reshape(1, W_A), w_s_b, b_s_exp)


QK_W = 2 * HEAD_DIM
UQ_W = B_HEADS * (NOPE_DIM + 2 * ROPE_DIM)
UKV_W = B_HEADS * (NOPE_DIM + V_DIM)
BF16_SUBLANES = 16
VT_ROWS = V_DIM + BF16_SUBLANES
_Q_PRESCALE = MLA_SCALE * math.log2(math.e)
assert B_HEADS % 2 == 0 and 2 * ROPE_DIM == LANES


def _upproj_kernel(cq_ref, ckv_ref, kr_ref, cos_ref, sin_ref, qg_ref, kg_ref, wq_ref, wkv_ref,
                   q_ref, k_ref, vt_ref):
    cos2 = cos_ref[...]
    sin2 = sin_ref[...]
    n_rope = B_HEADS * ROPE_DIM
    cqn = _rms(cq_ref[...].astype(F32), qg_ref[...]).astype(BF16)
    q_all = jnp.dot(cqn, wq_ref[...], preferred_element_type=F32)
    rope0 = B_HEADS * NOPE_DIM
    for h in range(B_HEADS):
        q_ref[h, :, 0:NOPE_DIM] = (q_all[:, h * NOPE_DIM:(h + 1) * NOPE_DIM] * _Q_PRESCALE).astype(BF16)
    for j in range(B_HEADS // 2):
        a = q_all[:, rope0 + j * LANES: rope0 + (j + 1) * LANES]
        a_sw = q_all[:, rope0 + n_rope + j * LANES: rope0 + n_rope + (j + 1) * LANES]
        rot = ((a * cos2 + a_sw * sin2) * _Q_PRESCALE).astype(BF16)
        q_ref[2 * j, :, NOPE_DIM:QK_W] = rot
        q_ref[2 * j + 1, :, NOPE_DIM:QK_W] = rot

    ckvn = _rms(ckv_ref[...].astype(F32), kg_ref[...]).astype(BF16)
    kv_all = jnp.dot(ckvn, wkv_ref[...], preferred_element_type=F32)
    kr = kr_ref[...].astype(F32)
    krot = kr[:, 0:LANES] * cos2 + kr[:, LANES:2 * LANES] * sin2
    lane = lax.broadcasted_iota(jnp.int32, krot.shape, 1)
    k_lo = jnp.where(lane < ROPE_DIM, krot, 0.0).astype(BF16)
    k_hi = jnp.where(lane >= ROPE_DIM, krot, 0.0).astype(BF16)
    v0 = B_HEADS * NOPE_DIM
    sub = lax.broadcasted_iota(jnp.int32, (VT_ROWS - V_DIM, KV_CHUNK), 0)
    ones_rows = jnp.where(sub == 0, 1.0, 0.0).astype(BF16)
    for h in range(B_HEADS):
        k_ref[h, :, 0:NOPE_DIM] = kv_all[:, h * NOPE_DIM:(h + 1) * NOPE_DIM].astype(BF16)
        k_ref[h, :, NOPE_DIM:QK_W] = k_lo if h % 2 == 0 else k_hi
        v_h = kv_all[:, v0 + h * V_DIM: v0 + (h + 1) * V_DIM]
        for c in range(ROW_TILE // KV_CHUNK):
            vt_ref[h, c, 0:V_DIM, :] = v_h[c * KV_CHUNK:(c + 1) * KV_CHUNK, :].T.astype(BF16)
            vt_ref[h, c, V_DIM:VT_ROWS, :] = ones_rows


def _upproj(z, cos2, sin2, qa_g, kva_g, w_uq_p, w_ukv_p):
    tm = ROW_TILE
    est = 2 * (2 * tm * Q_LORA * 2 + tm * 4 * ROPE_DIM * 2 + 2 * tm * LANES * 4 + Q_LORA * UQ_W * 2
               + KV_LORA * UKV_W * 2 + B_HEADS * tm * (2 * QK_W + V_DIM) * 2) + 4 * tm * UQ_W * 4
    return pl.pallas_call(
        _upproj_kernel,
        grid=(ROW_TILES,),
        in_specs=[pl.BlockSpec((tm, Q_LORA), lambda t: (t, OFF_CQ // Q_LORA)),
                  pl.BlockSpec((tm, KV_LORA), lambda t: (t, OFF_CKV // KV_LORA)),
                  pl.BlockSpec((tm, 4 * ROPE_DIM), lambda t: (t, OFF_KR // (4 * ROPE_DIM))),
                  pl.BlockSpec((tm, LANES), lambda t: (_rope_tile(t), 0)),
                  pl.BlockSpec((tm, LANES), lambda t: (_rope_tile(t), 0)),
                  pl.BlockSpec((1, Q_LORA), lambda t: (0, 0)),
                  pl.BlockSpec((1, KV_LORA), lambda t: (0, 0)),
                  pl.BlockSpec((Q_LORA, UQ_W), lambda t: (0, 0)),
                  pl.BlockSpec((KV_LORA, UKV_W), lambda t: (0, 0))],
        out_specs=[pl.BlockSpec((B_HEADS, tm, QK_W), lambda t: (0, t, 0)),
                   pl.BlockSpec((B_HEADS, tm, QK_W), lambda t: (0, t, 0)),
                   pl.BlockSpec((B_HEADS, tm // KV_CHUNK, VT_ROWS, KV_CHUNK), lambda t: (0, t, 0, 0))],
        out_shape=[jax.ShapeDtypeStruct((B_HEADS, N_TOK, QK_W), BF16),
                   jax.ShapeDtypeStruct((B_HEADS, N_TOK, QK_W), BF16),
                   jax.ShapeDtypeStruct((B_HEADS, N_TOK // KV_CHUNK, VT_ROWS, KV_CHUNK), BF16)],
        compiler_params=pltpu.CompilerParams(dimension_semantics=("parallel",),
                                             vmem_limit_bytes=_vmem_limit(est)),
    )(z, z, z, cos2, sin2, qa_g.reshape(1, Q_LORA), kva_g.reshape(1, KV_LORA), w_uq_p, w_ukv_p)


MLA_TK = 512
MLA_TQ = 2048
MLA_TQG = 256
assert SEQ % MLA_TK == 0 and MLA_TK % KV_CHUNK == 0 and SEQ % MLA_TQ == 0


def _attend_t(qs, chunks):
    def scores(k, q):
        return lax.dot_general(k, q, (((1,), (1,)), ((), ())), preferred_element_type=F32)

    state = [None] * len(qs)
    s_next = [scores(chunks[0][0], q) for q in qs]
    for j, (_, vts) in enumerate(chunks):
        for g, q in enumerate(qs):
            s = s_next[g]
            if j + 1 < len(chunks):
                s_next[g] = scores(chunks[j + 1][0], q)
            s_max = jnp.max(s, axis=0, keepdims=True)
            m_new = s_max if state[g] is None else jnp.maximum(state[g][0], s_max)
            pv = None
            for n, vt in enumerate(vts):
                p = jnp.exp2(s[n * KV_CHUNK:(n + 1) * KV_CHUNK] - m_new)
                d = jnp.dot(vt, p.astype(BF16), preferred_element_type=F32)
                pv = d if n == 0 else pv + d
            if state[g] is None:
                state[g] = (m_new, pv)
            else:
                m, acc = state[g]
                state[g] = (m_new, jnp.exp2(m - m_new) * acc + pv)
    return [acc[0:V_DIM] / acc[V_DIM:V_DIM + 1] for _, acc in state]


def _mla_lat_kernel(q_ref, kl_ref, vtl_ref, kc_ref, vtc_ref, gb_ref, o_ref):
    per = MLA_TK // KV_CHUNK
    chunks = [(kl_ref[0, j * MLA_TK:(j + 1) * MLA_TK, :], [vtl_ref[0, j * per + n] for n in range(per)])
              for j in range(SEQ // MLA_TK)]
    chunks.append((kc_ref[0], [vtc_ref[0, n] for n in range(CTX_LEN // KV_CHUNK)]))
    groups = [(g * MLA_TQG, (g + 1) * MLA_TQG) for g in range(MLA_TQ // MLA_TQG)]
    outs = _attend_t([q_ref[0, lo:hi, :] for lo, hi in groups], chunks)
    for (lo, hi), o_t in zip(groups, outs):
        o_ref[lo:hi, :] = (o_t.T * _silu(gb_ref[lo:hi, :].astype(F32))).astype(BF16)


def _mla_ctx_kernel(q_ref, kc_ref, vtc_ref, gb_ref, o_ref):
    (o_t,) = _attend_t([q_ref[0]], [(kc_ref[0], [vtc_ref[0, n] for n in range(CTX_LEN // KV_CHUNK)])])
    o = o_t.T
    o_ref[...] = (o * _silu(gb_ref[...].astype(F32))).astype(BF16)


def _mla(q, k, vt, z, with_ctx_queries):
    ctx_tile = N_LAT // CTX_LEN
    tq = MLA_TQ
    nq = SEQ // tq
    est = 2 * (tq * QK_W * 2 + SEQ * QK_W * 2 + SEQ * V_DIM * 2 + CTX_LEN * (QK_W + V_DIM) * 2
               + 2 * tq * HEAD_DIM * 2) + 8 * MLA_TK * tq * 4
    mb = pl.pallas_call(
        _mla_lat_kernel,
        grid=(BATCH, B_HEADS, nq),
        in_specs=[pl.BlockSpec((1, tq, QK_W), lambda b, h, i: (h, b * nq + i, 0)),
                  pl.BlockSpec((1, SEQ, QK_W), lambda b, h, i: (h, b, 0)),
                  pl.BlockSpec((1, SEQ // KV_CHUNK, VT_ROWS, KV_CHUNK), lambda b, h, i: (h, b, 0, 0)),
                  pl.BlockSpec((1, CTX_LEN, QK_W), lambda b, h, i: (h, ctx_tile + b, 0)),
                  pl.BlockSpec((1, CTX_LEN // KV_CHUNK, VT_ROWS, KV_CHUNK), lambda b, h, i: (h, ctx_tile + b, 0, 0)),
                  pl.BlockSpec((tq, HEAD_DIM), lambda b, h, i: (b * nq + i, OFF_GB // HEAD_DIM + h))],
        out_specs=pl.BlockSpec((tq, HEAD_DIM), lambda b, h, i: (b * nq + i, h)),
        out_shape=jax.ShapeDtypeStruct((N_LAT, W_B), BF16),
        compiler_params=pltpu.CompilerParams(dimension_semantics=("parallel", "parallel", "parallel"),
                                             vmem_limit_bytes=_vmem_limit(est)),
    )(q, k, vt, k, vt, z)
    if not with_ctx_queries:
        return mb, None
    mb_ctx = pl.pallas_call(
        _mla_ctx_kernel,
        grid=(BATCH, B_HEADS),
        in_specs=[pl.BlockSpec((1, CTX_LEN, QK_W), lambda b, h: (h, ctx_tile + b, 0)),
                  pl.BlockSpec((1, CTX_LEN, QK_W), lambda b, h: (h, ctx_tile + b, 0)),
                  pl.BlockSpec((1, CTX_LEN // KV_CHUNK, VT_ROWS, KV_CHUNK), lambda b, h: (h, ctx_tile + b, 0, 0)),
                  pl.BlockSpec((CTX_LEN, HEAD_DIM), lambda b, h: (ctx_tile + b, OFF_GB // HEAD_DIM + h))],
        out_specs=pl.BlockSpec((CTX_LEN, HEAD_DIM), lambda b, h: (b, h)),
        out_shape=jax.ShapeDtypeStruct((N_CTX, W_B), BF16),
        compiler_params=pltpu.CompilerParams(dimension_semantics=("parallel", "parallel")),
    )(q, k, vt, z)
    return mb, mb_ctx


NA_QROWS = Q_TILE // GRID_W
NA_KROWS = 12
NA_KEYS = NA_KROWS * GRID_W
assert NA_QROWS + MAX_KH <= NA_KROWS + 1 and ROWS >= NA_KROWS


def _na_strip_row(j):
    return int(np.clip(j * NA_QROWS - MAX_KH // 2, 0, ROWS - NA_KROWS))


def _na_tables():
    def one(j):
        s = _na_strip_row(j)
        valid = np.zeros((NA_QROWS, NA_KROWS), bool)
        d = np.zeros((NA_QROWS, NA_KROWS), np.int32)
        for a in range(NA_QROWS):
            r = j * NA_QROWS + a
            r0 = int(np.clip(r - MAX_KH // 2, 0, ROWS - MAX_KH))
            for i in range(NA_KROWS):
                valid[a, i] = 0 <= s + i - r0 < MAX_KH
                d[a, i] = s + i - r + (MAX_KH - 1) if valid[a, i] else 0
        return valid, d
    n_tiles = ROWS // NA_QROWS
    first, mid, last = one(0), one(1), one(n_tiles - 1)
    for j in range(1, n_tiles - 1):
        v, d = one(j)
        assert (v == mid[0]).all() and (d == mid[1]).all()
    return np.stack([first[0], mid[0], last[0]]), np.stack([first[1], mid[1], last[1]])


_NA_VALID, _NA_DROW = _na_tables()


def _na_bias_tables(rpb_l):
    col = np.arange(GRID_W)
    c0 = np.clip(col - KW // 2, 0, GRID_W - KW)
    col_ok = (col[None, :] >= c0[:, None]) & (col[None, :] < c0[:, None] + KW)
    dc = np.clip(col[None, :] - col[:, None], -(KW - 1), KW - 1) + (KW - 1)
    onehot = (dc.reshape(-1)[None, :] == np.arange(2 * KW - 1)[:, None]).astype(np.float32)
    t1 = jnp.dot(rpb_l.reshape(C_HEADS * (2 * MAX_KH - 1), 2 * KW - 1), jnp.asarray(onehot),
                 precision=lax.Precision.HIGHEST).reshape(C_HEADS, 2 * MAX_KH - 1, GRID_W, GRID_W)
    t1 = jnp.where(jnp.asarray(col_ok)[None, None], t1, NEG)
    return jnp.concatenate([t1, t1], axis=-1)


def _softmax_rows(blocks):
    m = functools.reduce(jnp.maximum, [jnp.max(s, axis=-1, keepdims=True) for s in blocks])
    ps = [jnp.exp(s - m) for s in blocks]
    return ps, functools.reduce(jnp.add, [jnp.sum(p, axis=-1, keepdims=True) for p in ps])


def _assemble_bias(t_ref, bias_ref):
    lane = lax.broadcasted_iota(jnp.int32, (GRID_W, 2 * GRID_W), 1)
    neg = jnp.full((GRID_W, 2 * GRID_W), NEG, F32)
    for cls in range(3):
        for a in range(NA_QROWS):
            for pair in range(NA_KROWS // 2):
                halves = [t_ref[0, int(_NA_DROW[cls, a, i])] if _NA_VALID[cls, a, i] else neg
                          for i in (2 * pair, 2 * pair + 1)]
                tile = halves[0] if halves[0] is halves[1] else jnp.where(lane < GRID_W, halves[0], halves[1])
                bias_ref[cls, a * GRID_W:(a + 1) * GRID_W, pair * 2 * GRID_W:(pair + 1) * 2 * GRID_W] = tile


def _natten_kernel(q_ref, kl_ref, vl_ref, kc_ref, vc_ref, t_ref, g_ref, o_ref, bias_ref):
    nt = (((1,), (1,)), ((), ()))
    n_tiles = ROWS // NA_QROWS
    _assemble_bias(t_ref, bias_ref)

    def tile(j, carry):
        rows = pl.ds(pl.multiple_of(j * Q_TILE, Q_TILE), Q_TILE)
        strip_row = jnp.clip(j * NA_QROWS - MAX_KH // 2, 0, ROWS - NA_KROWS)
        strip = pl.ds(pl.multiple_of(strip_row * GRID_W, GRID_W), NA_KEYS)
        cls = jnp.where(j == 0, 0, jnp.where(j == n_tiles - 1, 2, 1))
        q = q_ref[rows, :]
        s_nb = lax.dot_general(q, kl_ref[strip, :], nt, preferred_element_type=F32) * C_SCALE + bias_ref[cls]
        s_cx = lax.dot_general(q, kc_ref[...], nt, preferred_element_type=F32) * C_SCALE
        (p_nb, p_cx), l = _softmax_rows([s_nb, s_cx])
        o = jnp.dot(p_nb.astype(BF16), vl_ref[strip, :], preferred_element_type=F32)
        o += jnp.dot(p_cx.astype(BF16), vc_ref[...], preferred_element_type=F32)
        o_ref[rows, :] = (o / l * _silu(g_ref[rows, :].astype(F32))).astype(BF16)
        return carry

    lax.fori_loop(0, n_tiles, tile, 0, unroll=2)


def _natten_ctx_kernel(q_ref, kc_ref, vc_ref, g_ref, o_ref):
    s =lax.dot_general(q_ref[...], kc_ref[...], (((1,), (1,)), ((), ())), preferred_element_type=F32) * C_SCALE
    (p,), l = _softmax_rows([s])
    o = jnp.dot(p.astype(BF16), vc_ref[...], preferred_element_type=F32)
    o_ref[...] = (o / l * _silu(g_ref[...].astype(F32))).astype(BF16)


def _natten(z, bias_tab, with_ctx_queries):
    ctx_tile = N_LAT // CTX_LEN
    hd = HEAD_DIM
    est = (2 * (5 * SEQ * hd * 2 + 2 * CTX_LEN * hd * 2 + 3 * Q_TILE * NA_KEYS * 4)
           + 12 * Q_TILE * (NA_KEYS + CTX_LEN) * 4)
    mc = pl.pallas_call(
        _natten_kernel,
        grid=(BATCH, C_HEADS),
        in_specs=[pl.BlockSpec((SEQ, hd), lambda b, h: (b, OFF_QC // hd + h)),
                  pl.BlockSpec((SEQ, hd), lambda b, h: (b, OFF_KC // hd + h)),
                  pl.BlockSpec((SEQ, hd), lambda b, h: (b, OFF_VC // hd + h)),
                  pl.BlockSpec((CTX_LEN, hd), lambda b, h: (ctx_tile + b, OFF_KC // hd + h)),
                  pl.BlockSpec((CTX_LEN, hd), lambda b, h: (ctx_tile + b, OFF_VC // hd + h)),
                  pl.BlockSpec((1, 2 * MAX_KH - 1, GRID_W, 2 * GRID_W), lambda b, h: (h, 0, 0, 0)),
                  pl.BlockSpec((SEQ, hd), lambda b, h: (b, OFF_GC // hd + h))],
        out_specs=pl.BlockSpec((SEQ, hd), lambda b, h: (b, h)),
        out_shape=jax.ShapeDtypeStruct((N_LAT, W_C), BF16),
        scratch_shapes=[pltpu.VMEM((3, Q_TILE, NA_KEYS), F32)],
        compiler_params=pltpu.CompilerParams(dimension_semantics=("parallel", "parallel"),
                                             vmem_limit_bytes=_vmem_limit(est)),
    )(z, z, z, z, z, bias_tab, z)
    if not with_ctx_queries:
        return mc, None
    mc_ctx = pl.pallas_call(
        _natten_ctx_kernel,
        grid=(BATCH, C_HEADS),
        in_specs=[pl.BlockSpec((CTX_LEN, hd), lambda b, h: (ctx_tile + b, OFF_QC // hd + h)),
                  pl.BlockSpec((CTX_LEN, hd), lambda b, h: (ctx_tile + b, OFF_KC // hd + h)),
                  pl.BlockSpec((CTX_LEN, hd), lambda b, h: (ctx_tile + b, OFF_VC // hd + h)),
                  pl.BlockSpec((CTX_LEN, hd), lambda b, h: (ctx_tile + b, OFF_GC // hd + h))],
        out_specs=pl.BlockSpec((CTX_LEN, hd), lambda b, h: (b, h)),
        out_shape=jax.ShapeDtypeStruct((N_CTX, W_C), BF16),
        compiler_params=pltpu.CompilerParams(dimension_semantics=("parallel", "parallel")),
    )(z, z, z, z)
    return mc, mc_ctx


OUT_CHUNK = 512


def _outproj_body(ma, mb, mc, x, w_ref, mod_ref, o_ref):
    gate = mod_ref[0, :, 2 * D_MODEL:3 * D_MODEL]
    for c0 in range(0, D_MODEL, OUT_CHUNK):
        c1 = c0 + OUT_CHUNK
        y = jnp.dot(ma, w_ref[0:W_A, c0:c1], preferred_element_type=F32)
        y += jnp.dot(mb, w_ref[W_A:W_A + W_B, c0:c1], preferred_element_type=F32)
        y += jnp.dot(mc, w_ref[W_A + W_B:D_MODEL, c0:c1], preferred_element_type=F32)
        o_ref[:, c0:c1] = x[:, c0:c1] + gate[:, c0:c1] * y


def _outproj_mid_kernel(ma_ref, mbl_ref, mbc_ref, mcl_ref, mcc_ref, xl_ref, xc_ref, w_ref, mod_ref, o_ref):
    _outproj_body(ma_ref[...], _pick_rows(mbl_ref, mbc_ref), _pick_rows(mcl_ref, mcc_ref),
                  _pick_rows(xl_ref, xc_ref), w_ref, mod_ref, o_ref)


def _outproj_final_kernel(ma_ref, mb_ref, mc_ref, x_ref, w_ref, mod_ref, fg_ref, o_ref):
    _outproj_body(ma_ref[...], mb_ref[...], mc_ref[...], x_ref[...], w_ref, mod_ref, o_ref)
    o_ref[...] = _rms(o_ref[...], fg_ref[...])


def _outproj_mid(ma, mb, mb_ctx, mc, mc_ctx, x_lat, x_ctx, ctx_block, w_out_b, mods_l):
    tm = ROW_TILE
    est = (3 * tm * D_MODEL * 2 + D_MODEL * D_MODEL * 2 + 5 * tm * D_MODEL * 4 + 3 * tm * OUT_CHUNK * 4)
    lat = lambda w: pl.BlockSpec((tm, w), lambda t: (_lat_tile(t), 0))
    ctx = lambda w, blk: pl.BlockSpec((tm, w), lambda t: (blk, 0), pipeline_mode=pl.Buffered(1))
    return pl.pallas_call(
        _outproj_mid_kernel,
        grid=(ROW_TILES,),
        in_specs=[pl.BlockSpec((tm, W_A), lambda t: (t, 0)),
                  lat(W_B), ctx(W_B, 0), lat(W_C), ctx(W_C, 0), lat(D_MODEL), ctx(D_MODEL, ctx_block),
                  pl.BlockSpec((D_MODEL, D_MODEL), lambda t: (0, 0), pipeline_mode=pl.Buffered(1)),
                  pl.BlockSpec((1, 1, 3 * D_MODEL), lambda t: (_mod_row(t), 0, 0))],
        out_specs=pl.BlockSpec((tm, D_MODEL), lambda t: (t, 0)),
        out_shape=jax.ShapeDtypeStruct((N_TOK, D_MODEL), F32),
        compiler_params=pltpu.CompilerParams(dimension_semantics=("parallel",),
                                             vmem_limit_bytes=_vmem_limit(est)),
    )(ma, mb, mb_ctx, mc, mc_ctx, x_lat, x_ctx, w_out_b, mods_l.reshape(MOD_ROWS, 1, 3 * D_MODEL))


def _outproj_final(ma, mb, mc, x_lat, w_out_b, mods_l, final_g):
    tm = ROW_TILE
    est = (2 * tm * D_MODEL * 2 + D_MODEL * D_MODEL * 2 + 4 * tm * D_MODEL * 4 + 3 * tm * OUT_CHUNK * 4)
    return pl.pallas_call(
        _outproj_final_kernel,
        grid=(LAT_TILES,),
        in_specs=[pl.BlockSpec((tm, W_A), lambda t: (t, 0)),
                  pl.BlockSpec((tm, W_B), lambda t: (t, 0)),
                  pl.BlockSpec((tm, W_C), lambda t: (t, 0)),
                  pl.BlockSpec((tm, D_MODEL), lambda t: (t, 0)),
                  pl.BlockSpec((D_MODEL, D_MODEL), lambda t: (0, 0), pipeline_mode=pl.Buffered(1)),
                  pl.BlockSpec((1, 1, 3 * D_MODEL), lambda t: (_mod_row(t), 0, 0)),
                  pl.BlockSpec((1, D_MODEL), lambda t: (0, 0))],
        out_specs=pl.BlockSpec((tm, D_MODEL), lambda t: (t, 0)),
        out_shape=jax.ShapeDtypeStruct((N_LAT, D_MODEL), F32),
        compiler_params=pltpu.CompilerParams(dimension_semantics=("parallel",),
                                             vmem_limit_bytes=_vmem_limit(est)),
    )(ma, mb, mc, x_lat, w_out_b, mods_l.reshape(MOD_ROWS, 1, 3 * D_MODEL), final_g.reshape(1, D_MODEL))


def _cols(w, name):
    a, b = _SRC[name]
    return w[:, a:b]


W_PREP_ROWS = 256


def _prep_w_in_kernel(w_ref, kr_ref, o_ref):
    h0, h1 = _SRC["u"][0], _SRC["ckv"][1]
    t0, t1 = _SRC["gb"][0], _SRC["gc"][1]
    o_ref[0:h1 - h0, :] = w_ref[h0:h1, :].astype(BF16)
    o_ref[GM_W + OFF_GB:GM_W + OFF_KR, :] = w_ref[t0:t1, :].astype(BF16)
    o_ref[GM_W + OFF_KR:W_ROWS, :] = kr_ref[...]


def _prep_w_in(w_in, l):
    assert _SRC["ckv"][1] == GM_W + OFF_GB and OFF_GB + (_SRC["gc"][1] - _SRC["gb"][0]) == OFF_KR
    w_t = jnp.swapaxes(w_in, 1, 2)
    kr = w_t[l, _SRC["kr"][0]:_SRC["kr"][1], :]
    kr_sw = jnp.concatenate([kr[_Q4:2 * _Q4], kr[0:_Q4], kr[3 * _Q4:4 * _Q4], kr[2 * _Q4:3 * _Q4]], axis=0)
    kr4 = jnp.concatenate([kr, kr, kr_sw, kr_sw], axis=0).astype(BF16)
    cols = W_PREP_ROWS
    est = 2 * cols * (IN_W * 4 + W_ROWS * 2 + 4 * ROPE_DIM * 2) + 2 * cols * IN_W * 4
    return pl.pallas_call(
        _prep_w_in_kernel,
        grid=(D_MODEL // cols,),
        in_specs=[pl.BlockSpec((IN_W, cols), lambda i: (l, i)),
                  pl.BlockSpec((4 * ROPE_DIM, cols), lambda i: (0, i))],
        out_specs=pl.BlockSpec((W_ROWS, cols), lambda i: (0, i)),
        out_shape=jax.ShapeDtypeStruct((W_ROWS, D_MODEL), BF16),
        compiler_params=pltpu.CompilerParams(dimension_semantics=("parallel",),
                                             vmem_limit_bytes=_vmem_limit(est)),
    )(w_t.reshape(DEPTH * IN_W, D_MODEL), kr4)


def _cast_kernel(w_ref, o_ref):
    o_ref[...] = w_ref[...].astype(BF16)


def _prep_w_out(w_out, l):
    rows = W_PREP_ROWS
    return pl.pallas_call(
        _cast_kernel,
        grid=(MIX_W // rows,),
        in_specs=[pl.BlockSpec((rows, D_MODEL), lambda i: (l * (MIX_W // rows) + i, 0))],
        out_specs=pl.BlockSpec((rows, D_MODEL), lambda i: (i, 0)),
        out_shape=jax.ShapeDtypeStruct((MIX_W, D_MODEL), BF16),
        compiler_params=pltpu.CompilerParams(dimension_semantics=("parallel",)),
    )(w_out.reshape(DEPTH * MIX_W, D_MODEL))


def _prep_w_uq(w):
    w3 = w.reshape(Q_LORA, B_HEADS, NOPE_DIM + ROPE_DIM)
    nope = w3[:, :, :NOPE_DIM].reshape(Q_LORA, B_HEADS * NOPE_DIM)
    rope = w3[:, :, NOPE_DIM:]
    rope_sw = jnp.concatenate([rope[..., _Q4:2 * _Q4], rope[..., 0:_Q4],
                               rope[..., 3 * _Q4:4 * _Q4], rope[..., 2 * _Q4:3 * _Q4]], axis=-1)
    return jnp.concatenate([nope, rope.reshape(Q_LORA, -1), rope_sw.reshape(Q_LORA, -1)], axis=1).astype(BF16)


def _prep_w_ukv(w):
    w3 = w.reshape(KV_LORA, B_HEADS, NOPE_DIM + V_DIM)
    return jnp.concatenate([w3[:, :, :NOPE_DIM].reshape(KV_LORA, -1),
                            w3[:, :, NOPE_DIM:].reshape(KV_LORA, -1)], axis=1).astype(BF16)


def _rope_tables():
    t = np.arange(SEQ)
    row = (t // GRID_W).astype(np.float32)
    col = (t % GRID_W).astype(np.float32)
    inv = (np.float32(ROPE_THETA) ** (-np.arange(_Q4, dtype=np.float32) / np.float32(_Q4))).astype(np.float32)
    ar, ac = row[:, None] * inv, col[:, None] * inv
    cos = np.concatenate([np.cos(ar), np.cos(ar), np.cos(ac), np.cos(ac)], axis=1)
    sin = np.concatenate([-np.sin(ar), np.sin(ar), -np.sin(ac), np.sin(ac)], axis=1)
    cos = np.concatenate([cos, np.ones((ROW_TILE, ROPE_DIM))], axis=0).astype(np.float32)
    sin = np.concatenate([sin, np.zeros((ROW_TILE, ROPE_DIM))], axis=0).astype(np.float32)
    return np.concatenate([cos, cos], axis=1), np.concatenate([sin, sin], axis=1)


_COS2, _SIN2 = _rope_tables()


def _rope_tile(t):
    return jnp.where(t < LAT_TILES, t % TILES_PER_SAMPLE, TILES_PER_SAMPLE)


def kernel(x, c, ctx, c_ctx, w_ada, b_ada, norm_g, w_in, qa_g, kva_g, w_uq, w_ukv, sgu_g, w_s, b_s, rpb,
           w_out, final_g):
    assert x.shape == (BATCH, SEQ, D_MODEL) and ctx.shape == (BATCH, CTX_LEN, D_MODEL)
    assert w_in.shape == (DEPTH, D_MODEL, IN_W)
    cc = jnp.concatenate([c, c_ctx[None, :], jnp.zeros((MOD_ROWS - BATCH - 1, D_MODEL), F32)], axis=0)
    mods = _modulation(cc, w_ada, b_ada)
    cos2, sin2 = jnp.asarray(_COS2), jnp.asarray(_SIN2)

    x_lat, x_ctx, ctx_block = x.reshape(N_LAT, D_MODEL), ctx.reshape(N_CTX, D_MODEL), 0
    for l in range(DEPTH):
        last = l == DEPTH - 1
        b_s_exp = jnp.repeat(b_s[l].T, HEAD_DIM, axis=1)
        z, ma = _inproj(x_lat, x_ctx, ctx_block, mods[l], norm_g[l], _prep_w_in(w_in, l),
                        sgu_g[l], w_s[l].astype(BF16), b_s_exp)
        q, k, vt = _upproj(z, cos2, sin2, qa_g[l], kva_g[l], _prep_w_uq(w_uq[l]), _prep_w_ukv(w_ukv[l]))
        mb, mb_ctx = _mla(q, k, vt, z, with_ctx_queries=not last)
        mc, mc_ctx = _natten(z, _na_bias_tables(rpb[l]), with_ctx_queries=not last)
        w_out_b = _prep_w_out(w_out, l)
        if last:
            out = _outproj_final(ma, mb, mc, x_lat, w_out_b, mods[l], final_g)
            return out.reshape(BATCH, SEQ, D_MODEL)
        xf = _outproj_mid(ma, mb, mb_ctx, mc, mc_ctx, x_lat, x_ctx, ctx_block, w_out_b, mods[l])
        x_lat, x_ctx, ctx_block = xf, xf, LAT_TILES
```

```python
import functools
import math

import numpy as np
import jax
import jax.numpy as jnp
from jax import lax
from jax.experimental import pallas as pl
from jax.experimental.pallas import tpu as pltpu

D_MODEL = 2048
BATCH = 2
SEQ = 4096
DEPTH = 2
GRID_W = 64
CTX_LEN = 256
EPS = 1e-6
NEG = -1e30
HEAD_DIM = 128
W_A = D_MODEL // 4
W_B = D_MODEL // 2
W_C = D_MODEL // 4
MIX_W = W_A + W_B + W_C
CHUNK = 128
A_GROUPS = W_A // HEAD_DIM
B_HEADS = W_B // HEAD_DIM
Q_LORA = D_MODEL // 4
KV_LORA = 512
NOPE_DIM = 128
ROPE_DIM = 64
V_DIM = 128
MLA_SCALE = (NOPE_DIM + ROPE_DIM) ** -0.5
ROPE_THETA = 10000.0
C_HEADS = W_C // HEAD_DIM
MAX_KH = 8
KW = 16
C_SCALE = HEAD_DIM ** -0.5
ROWS = SEQ // GRID_W

LANES = 128
SUBLANES = 8
BF16_SUBLANES = 16
MXU_DIM = 256
VMEM_BYTES_V7X = 64 * 1024 * 1024
VMEM_LIMIT_CAP = VMEM_BYTES_V7X * 7 // 8

F32 = jnp.float32
BF16 = jnp.bfloat16

N_LAT = BATCH * SEQ
N_CTX = BATCH * CTX_LEN
N_TOK = N_LAT + N_CTX
ROW_TILE = N_CTX
LAT_TILES = N_LAT // ROW_TILE
ROW_TILES = N_TOK // ROW_TILE
TILES_PER_SAMPLE = SEQ // ROW_TILE
Q_TILE = CTX_LEN
KV_CHUNK = MXU_DIM
assert ROW_TILE % CHUNK == 0 and SEQ % ROW_TILE == 0 and SEQ % Q_TILE == 0 and CTX_LEN % KV_CHUNK == 0

GM_W = 3 * W_A
OFF_CQ = 0
OFF_CKV = OFF_CQ + Q_LORA
OFF_GB = OFF_CKV + KV_LORA
OFF_QC = OFF_GB + W_B
OFF_KC = OFF_QC + W_C
OFF_VC = OFF_KC + W_C
OFF_GC = OFF_VC + W_C
OFF_KR = OFF_GC + W_C
Z_W = OFF_KR + 4 * ROPE_DIM
W_ROWS = GM_W + Z_W
Z_CHUNK = 512
assert W_ROWS % MXU_DIM == 0 and OFF_KR % (4 * ROPE_DIM) == 0

_SRC = {}
_acc = 0
for _name, _w in (("u", W_A), ("va", W_A), ("ga", W_A), ("cq", Q_LORA), ("ckv", KV_LORA), ("kr", ROPE_DIM),
                  ("gb", W_B), ("qc", W_C), ("kc", W_C), ("vc", W_C), ("gc", W_C)):
    _SRC[_name] = (_acc, _acc + _w)
    _acc += _w
IN_W = _acc

_Q4 = ROPE_DIM // 4


def _vmem_limit(nbytes):
    return int(min(VMEM_LIMIT_CAP, max(16 * 1024 * 1024, nbytes * 5 // 4)))


def _silu(x):
    return x * jax.nn.sigmoid(x)


def _rms(x, g):
    return x * lax.rsqrt(jnp.mean(x * x, axis=-1, keepdims=True) + EPS) * g


MOD_ROWS = SUBLANES
MOD_TN = 768
assert BATCH + 1 <= MOD_ROWS and (3 * D_MODEL) % MOD_TN == 0 and MOD_TN % LANES == 0


def _mod_kernel(c_ref, w_ref, b_ref, o_ref):
    a = _silu(c_ref[...])
    a_hi = a.astype(BF16)
    a_lo = (a - a_hi.astype(F32)).astype(BF16)
    w = w_ref[0]
    w_hi = w.astype(BF16)
    w_lo = (w - w_hi.astype(F32)).astype(BF16)
    r = jnp.dot(jnp.concatenate([a_hi, a_lo], axis=0), w_hi, preferred_element_type=F32)
    r_lo = jnp.dot(a_hi, w_lo, preferred_element_type=F32)
    o_ref[0] = r[0:MOD_ROWS] + r[MOD_ROWS:2 * MOD_ROWS] + r_lo + b_ref[0]


def _modulation(cc, w_ada, b_ada):
    n = 3 * D_MODEL
    est = 2 * (MOD_ROWS * D_MODEL * 4 + D_MODEL * MOD_TN * 4 + 2 * MOD_ROWS * MOD_TN * 4)
    return pl.pallas_call(
        _mod_kernel,
        grid=(DEPTH, n // MOD_TN),
        in_specs=[pl.BlockSpec((MOD_ROWS, D_MODEL), lambda l, j: (0, 0)),
                  pl.BlockSpec((1, D_MODEL, MOD_TN), lambda l, j: (l, 0, j)),
                  pl.BlockSpec((1, 1, MOD_TN), lambda l, j: (l, 0, j))],
        out_specs=pl.BlockSpec((1, MOD_ROWS, MOD_TN), lambda l, j: (l, 0, j)),
        out_shape=jax.ShapeDtypeStruct((DEPTH, MOD_ROWS, n), F32),
        compiler_params=pltpu.CompilerParams(dimension_semantics=("parallel", "parallel"),
                                             vmem_limit_bytes=_vmem_limit(est)),
    )(cc, w_ada, b_ada.reshape(DEPTH, 1, n))


def _mod_row(t):
    return jnp.where(t < LAT_TILES, t // TILES_PER_SAMPLE, BATCH)


def _lat_tile(t):
    return jnp.minimum(t, LAT_TILES - 1)


def _pick_rows(lat_ref, ctx_ref):
    return jnp.where(pl.program_id(0) < LAT_TILES, lat_ref[...], ctx_ref[...])


def _inproj_kernel(xl_ref, xc_ref, mod_ref, g_ref, w_ref, sg_ref, ws_ref, bs_ref, z_ref, ma_ref, h_ref):
    nt = (((1,), (1,)), ((), ()))
    x = _pick_rows(xl_ref, xc_ref)
    shift = mod_ref[0, :, 0:D_MODEL]
    scale = mod_ref[0, :, D_MODEL:2 * D_MODEL]
    h_ref[...] = (_rms(x, g_ref[...]) * (1.0 + scale) + shift).astype(BF16)

    def proj(r0, r1):
        return lax.dot_general(h_ref[...], w_ref[r0:r1, :], nt, preferred_element_type=F32)

    vn = _rms(jax.nn.gelu(proj(W_A, 2 * W_A)), sg_ref[...]).astype(BF16)
    front = jax.nn.gelu(proj(0, W_A)) * _silu(proj(2 * W_A, 3 * W_A))
    for c in range(ROW_TILE // CHUNK):
        r0, r1 = c * CHUNK, (c + 1) * CHUNK
        for g in range(A_GROUPS):
            c0, c1 = g * HEAD_DIM, (g + 1) * HEAD_DIM
            s = jnp.dot(ws_ref[g], vn[r0:r1, c0:c1], preferred_element_type=F32) + bs_ref[:, c0:c1]
            ma_ref[r0:r1, c0:c1] = (front[r0:r1, c0:c1] * s).astype(BF16)

    for c0 in range(0, Z_W, Z_CHUNK):
        c1 = min(c0 + Z_CHUNK, Z_W)
        z_ref[:, c0:c1] = proj(GM_W + c0, GM_W + c1).astype(BF16)


def _inproj(x_lat, x_ctx, ctx_block, mods_l, norm_g, w_in_p, sgu_g, w_s_b, b_s_exp):
    tm = ROW_TILE
    est = (3 * tm * D_MODEL * 4 + D_MODEL * W_ROWS * 2 + 2 * tm * (Z_W + W_A) * 2 + tm * D_MODEL * 2
           + 6 * tm * Z_CHUNK * 4)
    return pl.pallas_call(
        _inproj_kernel,
        grid=(ROW_TILES,),
        in_specs=[pl.BlockSpec((tm, D_MODEL), lambda t: (_lat_tile(t), 0)),
                  pl.BlockSpec((tm, D_MODEL), lambda t: (ctx_block, 0), pipeline_mode=pl.Buffered(1)),
                  pl.BlockSpec((1, 1, 3 * D_MODEL), lambda t: (_mod_row(t), 0, 0)),
                  pl.BlockSpec((1, D_MODEL), lambda t: (0, 0)),
                  pl.BlockSpec((W_ROWS, D_MODEL), lambda t: (0, 0), pipeline_mode=pl.Buffered(1)),
                  pl.BlockSpec((1, W_A), lambda t: (0, 0)),
                  pl.BlockSpec((A_GROUPS, CHUNK, CHUNK), lambda t: (0, 0, 0)),
                  pl.BlockSpec((CHUNK, W_A), lambda t: (0, 0))],
        out_specs=[pl.BlockSpec((tm, Z_W), lambda t: (t, 0)),
                   pl.BlockSpec((tm, W_A), lambda t: (t, 0))],
        out_shape=[jax.ShapeDtypeStruct((N_TOK, Z_W), BF16),
                   jax.ShapeDtypeStruct((N_TOK, W_A), BF16)],
        scratch_shapes=[pltpu.VMEM((tm, D_MODEL), BF16)],
        compiler_params=pltpu.CompilerParams(dimension_semantics=("parallel",),
                                             vmem_limit_bytes=_vmem_limit(est)),
    )(x_lat, x_ctx, mods_l.reshape(MOD_ROWS, 1, 3 * D_MODEL), norm_g.reshape(1, D_MODEL), w_in_p,
      sgu_g.reshape(1, W_A), w_s_b, b_s_exp)


QK_W = 2 * HEAD_DIM
UQ_W = B_HEADS * (NOPE_DIM + 2 * ROPE_DIM)
UKV_W = B_HEADS * (NOPE_DIM + V_DIM)
VT_ROWS = V_DIM + BF16_SUBLANES
_Q_PRESCALE = MLA_SCALE * math.log2(math.e)
assert B_HEADS % 2 == 0 and 2 * ROPE_DIM == LANES


def _upproj_kernel(cq_ref, ckv_ref, kr_ref, cos_ref, sin_ref, qg_ref, kg_ref, wq_ref, wkv_ref,
                   q_ref, k_ref, vt_ref):
    cos2 = cos_ref[...]
    sin2 = sin_ref[...]
    n_rope = B_HEADS * ROPE_DIM
    cqn = _rms(cq_ref[...].astype(F32), qg_ref[...]).astype(BF16)
    q_all = jnp.dot(cqn, wq_ref[...], preferred_element_type=F32)
    rope0 = B_HEADS * NOPE_DIM
    for h in range(B_HEADS):
        q_ref[h, :, 0:NOPE_DIM] = (q_all[:, h * NOPE_DIM:(h + 1) * NOPE_DIM] * _Q_PRESCALE).astype(BF16)
    for j in range(B_HEADS // 2):
        a = q_all[:, rope0 + j * LANES: rope0 + (j + 1) * LANES]
        a_sw = q_all[:, rope0 + n_rope + j * LANES: rope0 + n_rope + (j + 1) * LANES]
        rot = ((a * cos2 + a_sw * sin2) * _Q_PRESCALE).astype(BF16)
        q_ref[2 * j, :, NOPE_DIM:QK_W] = rot
        q_ref[2 * j + 1, :, NOPE_DIM:QK_W] = rot

    ckvn = _rms(ckv_ref[...].astype(F32), kg_ref[...]).astype(BF16)
    kv_all = jnp.dot(ckvn, wkv_ref[...], preferred_element_type=F32)
    kr = kr_ref[...].astype(F32)
    krot = kr[:, 0:LANES] * cos2 + kr[:, LANES:2 * LANES] * sin2
    lane = lax.broadcasted_iota(jnp.int32, krot.shape, 1)
    k_lo = jnp.where(lane < ROPE_DIM, krot, 0.0).astype(BF16)
    k_hi = jnp.where(lane >= ROPE_DIM, krot, 0.0).astype(BF16)
    v0 = B_HEADS * NOPE_DIM
    sub = lax.broadcasted_iota(jnp.int32, (VT_ROWS - V_DIM, KV_CHUNK), 0)
    ones_rows = jnp.where(sub == 0, 1.0, 0.0).astype(BF16)
    for h in range(B_HEADS):
        k_ref[h, :, 0:NOPE_DIM] = kv_all[:, h * NOPE_DIM:(h + 1) * NOPE_DIM].astype(BF16)
        k_ref[h, :, NOPE_DIM:QK_W] = k_lo if h % 2 == 0 else k_hi
        v_h = kv_all[:, v0 + h * V_DIM: v0 + (h + 1) * V_DIM]
        for c in range(ROW_TILE // KV_CHUNK):
            vt_ref[h, c, 0:V_DIM, :] = v_h[c * KV_CHUNK:(c + 1) * KV_CHUNK, :].T.astype(BF16)
            vt_ref[h, c, V_DIM:VT_ROWS, :] = ones_rows


def _upproj(z, cos2, sin2, qa_g, kva_g, w_uq_p, w_ukv_p):
    tm = ROW_TILE
    est = 2 * (2 * tm * Q_LORA * 2 + tm * 4 * ROPE_DIM * 2 + 2 * tm * LANES * 4 + Q_LORA * UQ_W * 2
               + KV_LORA * UKV_W * 2 + B_HEADS * tm * (2 * QK_W + V_DIM) * 2) + 4 * tm * UQ_W * 4
    return pl.pallas_call(
        _upproj_kernel,
        grid=(ROW_TILES,),
        in_specs=[pl.BlockSpec((tm, Q_LORA), lambda t: (t, OFF_CQ // Q_LORA)),
                  pl.BlockSpec((tm, KV_LORA), lambda t: (t, OFF_CKV // KV_LORA)),
                  pl.BlockSpec((tm, 4 * ROPE_DIM), lambda t: (t, OFF_KR // (4 * ROPE_DIM))),
                  pl.BlockSpec((tm, LANES), lambda t: (_rope_tile(t), 0)),
                  pl.BlockSpec((tm, LANES), lambda t: (_rope_tile(t), 0)),
                  pl.BlockSpec((1, Q_LORA), lambda t: (0, 0)),
                  pl.BlockSpec((1, KV_LORA), lambda t: (0, 0)),
                  pl.BlockSpec((Q_LORA, UQ_W), lambda t: (0, 0)),
                  pl.BlockSpec((KV_LORA, UKV_W), lambda t: (0, 0))],
        out_specs=[pl.BlockSpec((B_HEADS, tm, QK_W), lambda t: (0, t, 0)),
                   pl.BlockSpec((B_HEADS, tm, QK_W), lambda t: (0, t, 0)),
                   pl.BlockSpec((B_HEADS, tm // KV_CHUNK, VT_ROWS, KV_CHUNK), lambda t: (0, t, 0, 0))],
        out_shape=[jax.ShapeDtypeStruct((B_HEADS, N_TOK, QK_W), BF16),
                   jax.ShapeDtypeStruct((B_HEADS, N_TOK, QK_W), BF16),
                   jax.ShapeDtypeStruct((B_HEADS, N_TOK // KV_CHUNK, VT_ROWS, KV_CHUNK), BF16)],
        compiler_params=pltpu.CompilerParams(dimension_semantics=("parallel",),
                                             vmem_limit_bytes=_vmem_limit(est)),
    )(z, z, z, cos2, sin2, qa_g.reshape(1, Q_LORA), kva_g.reshape(1, KV_LORA), w_uq_p, w_ukv_p)


MLA_TK = 512
MLA_TQ = 2048
MLA_TQG = MXU_DIM
assert SEQ % MLA_TK == 0 and MLA_TK % KV_CHUNK == 0 and SEQ % MLA_TQ == 0 and MLA_TQ % MLA_TQG == 0


def _attend_t(qs, chunks):
    def scores(k, q):
        return lax.dot_general(k, q, (((1,), (1,)), ((), ())), preferred_element_type=F32)

    state = [None] * len(qs)
    s_next = [scores(chunks[0][0], q) for q in qs]
    for j, (_, vts) in enumerate(chunks):
        for g, q in enumerate(qs):
            s = s_next[g]
            if j + 1 < len(chunks):
                s_next[g] = scores(chunks[j + 1][0], q)
            s_max = jnp.max(s, axis=0, keepdims=True)
            m_new = s_max if state[g] is None else jnp.maximum(state[g][0], s_max)
            pv = None
            for n, vt in enumerate(vts):
                p = jnp.exp2(s[n * KV_CHUNK:(n + 1) * KV_CHUNK] - m_new)
                d = jnp.dot(vt, p.astype(BF16), preferred_element_type=F32)
                pv = d if n == 0 else pv + d
            if state[g] is None:
                state[g] = (m_new, pv)
            else:
                m, acc = state[g]
                state[g] = (m_new, jnp.exp2(m - m_new) * acc + pv)
    return [acc[0:V_DIM] / acc[V_DIM:V_DIM + 1] for _, acc in state]


def _mla_lat_kernel(q_ref, kl_ref, vtl_ref, kc_ref, vtc_ref, gb_ref, o_ref):
    per = MLA_TK // KV_CHUNK
    chunks = [(kl_ref[0, j * MLA_TK:(j + 1) * MLA_TK, :], [vtl_ref[0, j * per + n] for n in range(per)])
              for j in range(SEQ // MLA_TK)]
    chunks.append((kc_ref[0], [vtc_ref[0, n] for n in range(CTX_LEN // KV_CHUNK)]))
    groups = [(g * MLA_TQG, (g + 1) * MLA_TQG) for g in range(MLA_TQ // MLA_TQG)]
    outs = _attend_t([q_ref[0, lo:hi, :] for lo, hi in groups], chunks)
    for (lo, hi), o_t in zip(groups, outs):
        o_ref[lo:hi, :] = (o_t.T * _silu(gb_ref[lo:hi, :].astype(F32))).astype(BF16)


def _mla_ctx_kernel(q_ref, kc_ref, vtc_ref, gb_ref, o_ref):
    (o_t,) = _attend_t([q_ref[0]], [(kc_ref[0], [vtc_ref[0, n] for n in range(CTX_LEN // KV_CHUNK)])])
    o = o_t.T
    o_ref[...] = (o * _silu(gb_ref[...].astype(F32))).astype(BF16)


def _mla(q, k, vt, z, with_ctx_queries):
    ctx_tile = N_LAT // CTX_LEN
    tq = MLA_TQ
    nq = SEQ // tq
    est = 2 * (tq * QK_W * 2 + SEQ * QK_W * 2 + SEQ * V_DIM * 2 + CTX_LEN * (QK_W + V_DIM) * 2
               + 2 * tq * HEAD_DIM * 2) + 8 * MLA_TK * tq * 4
    mb = pl.pallas_call(
        _mla_lat_kernel,
        grid=(BATCH, B_HEADS, nq),
        in_specs=[pl.BlockSpec((1, tq, QK_W), lambda b, h, i: (h, b * nq + i, 0)),
                  pl.BlockSpec((1, SEQ, QK_W), lambda b, h, i: (h, b, 0)),
                  pl.BlockSpec((1, SEQ // KV_CHUNK, VT_ROWS, KV_CHUNK), lambda b, h, i: (h, b, 0, 0)),
                  pl.BlockSpec((1, CTX_LEN, QK_W), lambda b, h, i: (h, ctx_tile + b, 0)),
                  pl.BlockSpec((1, CTX_LEN // KV_CHUNK, VT_ROWS, KV_CHUNK), lambda b, h, i: (h, ctx_tile + b, 0, 0)),
                  pl.BlockSpec((tq, HEAD_DIM), lambda b, h, i: (b * nq + i, OFF_GB // HEAD_DIM + h))],
        out_specs=pl.BlockSpec((tq, HEAD_DIM), lambda b, h, i: (b * nq + i, h)),
        out_shape=jax.ShapeDtypeStruct((N_LAT, W_B), BF16),
        compiler_params=pltpu.CompilerParams(dimension_semantics=("parallel", "parallel", "parallel"),
                                             vmem_limit_bytes=_vmem_limit(est)),
    )(q, k, vt, k, vt, z)
    if not with_ctx_queries:
        return mb, None
    mb_ctx = pl.pallas_call(
        _mla_ctx_kernel,
        grid=(BATCH, B_HEADS),
        in_specs=[pl.BlockSpec((1, CTX_LEN, QK_W), lambda b, h: (h, ctx_tile + b, 0)),
                  pl.BlockSpec((1, CTX_LEN, QK_W), lambda b, h: (h, ctx_tile + b, 0)),
                  pl.BlockSpec((1, CTX_LEN // KV_CHUNK, VT_ROWS, KV_CHUNK), lambda b, h: (h, ctx_tile + b, 0, 0)),
                  pl.BlockSpec((CTX_LEN, HEAD_DIM), lambda b, h: (ctx_tile + b, OFF_GB // HEAD_DIM + h))],
        out_specs=pl.BlockSpec((CTX_LEN, HEAD_DIM), lambda b, h: (b, h)),
        out_shape=jax.ShapeDtypeStruct((N_CTX, W_B), BF16),
        compiler_params=pltpu.CompilerParams(dimension_semantics=("parallel", "parallel")),
    )(q, k, vt, z)
    return mb, mb_ctx


NA_QROWS = Q_TILE // GRID_W
NA_KROWS = NA_QROWS + MAX_KH
NA_KEYS = NA_KROWS * GRID_W
assert ROWS >= NA_KROWS and NA_KEYS % MXU_DIM == 0 and NA_KROWS % 2 == 0 and 2 * GRID_W == LANES


def _na_strip_row(j):
    return int(np.clip(j * NA_QROWS - MAX_KH // 2, 0, ROWS - NA_KROWS))


def _na_tables():
    def one(j):
        s = _na_strip_row(j)
        valid = np.zeros((NA_QROWS, NA_KROWS), bool)
        d = np.zeros((NA_QROWS, NA_KROWS), np.int32)
        for a in range(NA_QROWS):
            r = j * NA_QROWS + a
            r0 = int(np.clip(r - MAX_KH // 2, 0, ROWS - MAX_KH))
            for i in range(NA_KROWS):
                valid[a, i] = 0 <= s + i - r0 < MAX_KH
                d[a, i] = s + i - r + (MAX_KH - 1) if valid[a, i] else 0
        return valid, d
    n_tiles = ROWS // NA_QROWS
    first, mid, last = one(0), one(1), one(n_tiles - 1)
    for j in range(1, n_tiles - 1):
        v, d = one(j)
        assert (v == mid[0]).all() and (d == mid[1]).all()
    return np.stack([first[0], mid[0], last[0]]), np.stack([first[1], mid[1], last[1]])


_NA_VALID, _NA_DROW = _na_tables()


def _na_bias_tables(rpb_l):
    col = np.arange(GRID_W)
    c0 = np.clip(col - KW // 2, 0, GRID_W - KW)
    col_ok = (col[None, :] >= c0[:, None]) & (col[None, :] < c0[:, None] + KW)
    dc = np.clip(col[None, :] - col[:, None], -(KW - 1), KW - 1) + (KW - 1)
    onehot = (dc.reshape(-1)[None, :] == np.arange(2 * KW - 1)[:, None]).astype(np.float32)
    t1 = jnp.dot(rpb_l.reshape(C_HEADS * (2 * MAX_KH - 1), 2 * KW - 1), jnp.asarray(onehot),
                 precision=lax.Precision.HIGHEST).reshape(C_HEADS, 2 * MAX_KH - 1, GRID_W, GRID_W)
    t1 = jnp.where(jnp.asarray(col_ok)[None, None], t1, NEG)
    return jnp.concatenate([t1, t1], axis=-1)


def _softmax_rows(blocks):
    m = functools.reduce(jnp.maximum, [jnp.max(s, axis=-1, keepdims=True) for s in blocks])
    ps = [jnp.exp(s - m) for s in blocks]
    return ps, functools.reduce(jnp.add, [jnp.sum(p, axis=-1, keepdims=True) for p in ps])


def _assemble_bias(t_ref, bias_ref):
    lane = lax.broadcasted_iota(jnp.int32, (GRID_W, 2 * GRID_W), 1)
    neg = jnp.full((GRID_W, 2 * GRID_W), NEG, F32)
    for cls in range(3):
        for a in range(NA_QROWS):
            for pair in range(NA_KROWS // 2):
                halves = [t_ref[0, int(_NA_DROW[cls, a, i])] if _NA_VALID[cls, a, i] else neg
                          for i in (2 * pair, 2 * pair + 1)]
                tile = halves[0] if halves[0] is halves[1] else jnp.where(lane < GRID_W, halves[0], halves[1])
                bias_ref[cls, a * GRID_W:(a + 1) * GRID_W, pair * 2 * GRID_W:(pair + 1) * 2 * GRID_W] = tile


def _natten_kernel(q_ref, kl_ref, vl_ref, kc_ref, vc_ref, t_ref, g_ref, o_ref, bias_ref):
    nt = (((1,), (1,)), ((), ()))
    n_tiles = ROWS // NA_QROWS
    _assemble_bias(t_ref, bias_ref)

    def tile(j, carry):
        rows = pl.ds(pl.multiple_of(j * Q_TILE, Q_TILE), Q_TILE)
        strip_row = jnp.clip(j * NA_QROWS - MAX_KH // 2, 0, ROWS - NA_KROWS)
        strip = pl.ds(pl.multiple_of(strip_row * GRID_W, GRID_W), NA_KEYS)
        cls = jnp.where(j == 0, 0, jnp.where(j == n_tiles - 1, 2, 1))
        q = q_ref[rows, :]
        s_nb = lax.dot_general(q, kl_ref[strip, :], nt, preferred_element_type=F32) * C_SCALE + bias_ref[cls]
        s_cx = lax.dot_general(q, kc_ref[...], nt, preferred_element_type=F32) * C_SCALE
        (p_nb, p_cx), l = _softmax_rows([s_nb, s_cx])
        o = jnp.dot(p_nb.astype(BF16), vl_ref[strip, :], preferred_element_type=F32)
        o += jnp.dot(p_cx.astype(BF16), vc_ref[...], preferred_element_type=F32)
        o_ref[rows, :] = (o / l * _silu(g_ref[rows, :].astype(F32))).astype(BF16)
        return carry

    lax.fori_loop(0, n_tiles, tile, 0, unroll=2)


def _natten_ctx_kernel(q_ref, kc_ref, vc_ref, g_ref, o_ref):
    s =lax.dot_general(q_ref[...], kc_ref[...], (((1,), (1,)), ((), ())), preferred_element_type=F32) * C_SCALE
    (p,), l = _softmax_rows([s])
    o = jnp.dot(p.astype(BF16), vc_ref[...], preferred_element_type=F32)
    o_ref[...] = (o / l * _silu(g_ref[...].astype(F32))).astype(BF16)


def _natten(z, bias_tab, with_ctx_queries):
    ctx_tile = N_LAT // CTX_LEN
    hd = HEAD_DIM
    est = (2 * (5 * SEQ * hd * 2 + 2 * CTX_LEN * hd * 2 + 3 * Q_TILE * NA_KEYS * 4)
           + 12 * Q_TILE * (NA_KEYS + CTX_LEN) * 4)
    mc = pl.pallas_call(
        _natten_kernel,
        grid=(BATCH, C_HEADS),
        in_specs=[pl.BlockSpec((SEQ, hd), lambda b, h: (b, OFF_QC // hd + h)),
                  pl.BlockSpec((SEQ, hd), lambda b, h: (b, OFF_KC // hd + h)),
                  pl.BlockSpec((SEQ, hd), lambda b, h: (b, OFF_VC // hd + h)),
                  pl.BlockSpec((CTX_LEN, hd), lambda b, h: (ctx_tile + b, OFF_KC // hd + h)),
                  pl.BlockSpec((CTX_LEN, hd), lambda b, h: (ctx_tile + b, OFF_VC // hd + h)),
                  pl.BlockSpec((1, 2 * MAX_KH - 1, GRID_W, 2 * GRID_W), lambda b, h: (h, 0, 0, 0)),
                  pl.BlockSpec((SEQ, hd), lambda b, h: (b, OFF_GC // hd + h))],
        out_specs=pl.BlockSpec((SEQ, hd), lambda b, h: (b, h)),
        out_shape=jax.ShapeDtypeStruct((N_LAT, W_C), BF16),
        scratch_shapes=[pltpu.VMEM((3, Q_TILE, NA_KEYS), F32)],
        compiler_params=pltpu.CompilerParams(dimension_semantics=("parallel", "parallel"),
                                             vmem_limit_bytes=_vmem_limit(est)),
    )(z, z, z, z, z, bias_tab, z)
    if not with_ctx_queries:
        return mc, None
    mc_ctx = pl.pallas_call(
        _natten_ctx_kernel,
        grid=(BATCH, C_HEADS),
        in_specs=[pl.BlockSpec((CTX_LEN, hd), lambda b, h: (ctx_tile + b, OFF_QC // hd + h)),
                  pl.BlockSpec((CTX_LEN, hd), lambda b, h: (ctx_tile + b, OFF_KC // hd + h)),
                  pl.BlockSpec((CTX_LEN, hd), lambda b, h: (ctx_tile + b, OFF_VC // hd + h)),
                  pl.BlockSpec((CTX_LEN, hd), lambda b, h: (ctx_tile + b, OFF_GC // hd + h))],
        out_specs=pl.BlockSpec((CTX_LEN, hd), lambda b, h: (b, h)),
        out_shape=jax.ShapeDtypeStruct((N_CTX, W_C), BF16),
        compiler_params=pltpu.CompilerParams(dimension_semantics=("parallel", "parallel")),
    )(z, z, z, z)
    return mc, mc_ctx


OUT_CHUNK = 512


def _outproj_body(ma, mb, mc, x, w_ref, mod_ref, o_ref):
    gate = mod_ref[0, :, 2 * D_MODEL:3 * D_MODEL]
    for c0 in range(0, D_MODEL, OUT_CHUNK):
        c1 = c0 + OUT_CHUNK
        y = jnp.dot(ma, w_ref[0:W_A, c0:c1], preferred_element_type=F32)
        y += jnp.dot(mb, w_ref[W_A:W_A + W_B, c0:c1], preferred_element_type=F32)
        y += jnp.dot(mc, w_ref[W_A + W_B:D_MODEL, c0:c1], preferred_element_type=F32)
        o_ref[:, c0:c1] = x[:, c0:c1] + gate[:, c0:c1] * y


def _outproj_mid_kernel(ma_ref, mbl_ref, mbc_ref, mcl_ref, mcc_ref, xl_ref, xc_ref, w_ref, mod_ref, o_ref):
    _outproj_body(ma_ref[...], _pick_rows(mbl_ref, mbc_ref), _pick_rows(mcl_ref, mcc_ref),
                  _pick_rows(xl_ref, xc_ref), w_ref, mod_ref, o_ref)


def _outproj_final_kernel(ma_ref, mb_ref, mc_ref, x_ref, w_ref, mod_ref, fg_ref, o_ref):
    _outproj_body(ma_ref[...], mb_ref[...], mc_ref[...], x_ref[...], w_ref, mod_ref, o_ref)
    o_ref[...] = _rms(o_ref[...], fg_ref[...])


def _outproj_mid(ma, mb, mb_ctx, mc, mc_ctx, x_lat, x_ctx, ctx_block, w_out_b, mods_l):
    tm = ROW_TILE
    est = (3 * tm * D_MODEL * 2 + D_MODEL * D_MODEL * 2 + 5 * tm * D_MODEL * 4 + 3 * tm * OUT_CHUNK * 4)
    lat = lambda w: pl.BlockSpec((tm, w), lambda t: (_lat_tile(t), 0))
    ctx = lambda w, blk: pl.BlockSpec((tm, w), lambda t: (blk, 0), pipeline_mode=pl.Buffered(1))
    return pl.pallas_call(
        _outproj_mid_kernel,
        grid=(ROW_TILES,),
        in_specs=[pl.BlockSpec((tm, W_A), lambda t: (t, 0)),
                  lat(W_B), ctx(W_B, 0), lat(W_C), ctx(W_C, 0), lat(D_MODEL), ctx(D_MODEL, ctx_block),
                  pl.BlockSpec((D_MODEL, D_MODEL), lambda t: (0, 0), pipeline_mode=pl.Buffered(1)),
                  pl.BlockSpec((1, 1, 3 * D_MODEL), lambda t: (_mod_row(t), 0, 0))],
        out_specs=pl.BlockSpec((tm, D_MODEL), lambda t: (t, 0)),
        out_shape=jax.ShapeDtypeStruct((N_TOK, D_MODEL), F32),
        compiler_params=pltpu.CompilerParams(dimension_semantics=("parallel",),
                                             vmem_limit_bytes=_vmem_limit(est)),
    )(ma, mb, mb_ctx, mc, mc_ctx, x_lat, x_ctx, w_out_b, mods_l.reshape(MOD_ROWS, 1, 3 * D_MODEL))


def _outproj_final(ma, mb, mc, x_lat, w_out_b, mods_l, final_g):
    tm = ROW_TILE
    est = (2 * tm * D_MODEL * 2 + D_MODEL * D_MODEL * 2 + 4 * tm * D_MODEL * 4 + 3 * tm * OUT_CHUNK * 4)
    return pl.pallas_call(
        _outproj_final_kernel,
        grid=(LAT_TILES,),
        in_specs=[pl.BlockSpec((tm, W_A), lambda t: (t, 0)),
                  pl.BlockSpec((tm, W_B), lambda t: (t, 0)),
                  pl.BlockSpec((tm, W_C), lambda t: (t, 0)),
                  pl.BlockSpec((tm, D_MODEL), lambda t: (t, 0)),
                  pl.BlockSpec((D_MODEL, D_MODEL), lambda t: (0, 0), pipeline_mode=pl.Buffered(1)),
                  pl.BlockSpec((1, 1, 3 * D_MODEL), lambda t: (_mod_row(t), 0, 0)),
                  pl.BlockSpec((1, D_MODEL), lambda t: (0, 0))],
        out_specs=pl.BlockSpec((tm, D_MODEL), lambda t: (t, 0)),
        out_shape=jax.ShapeDtypeStruct((N_LAT, D_MODEL), F32),
        compiler_params=pltpu.CompilerParams(dimension_semantics=("parallel",),
                                             vmem_limit_bytes=_vmem_limit(est)),
    )(ma, mb, mc, x_lat, w_out_b, mods_l.reshape(MOD_ROWS, 1, 3 * D_MODEL), final_g.reshape(1, D_MODEL))


W_PREP_ROWS = MXU_DIM


def _prep_w_in_kernel(w_ref, kr_ref, o_ref):
    h0, h1 = _SRC["u"][0], _SRC["ckv"][1]
    t0, t1 = _SRC["gb"][0], _SRC["gc"][1]
    o_ref[0:h1 - h0, :] = w_ref[h0:h1, :].astype(BF16)
    o_ref[GM_W + OFF_GB:GM_W + OFF_KR, :] = w_ref[t0:t1, :].astype(BF16)
    o_ref[GM_W + OFF_KR:W_ROWS, :] = kr_ref[...]


def _prep_w_in(w_in, l):
    assert _SRC["ckv"][1] == GM_W + OFF_GB and OFF_GB + (_SRC["gc"][1] - _SRC["gb"][0]) == OFF_KR
    w_t = jnp.swapaxes(w_in, 1, 2)
    kr = w_t[l, _SRC["kr"][0]:_SRC["kr"][1], :]
    kr_sw = jnp.concatenate([kr[_Q4:2 * _Q4], kr[0:_Q4], kr[3 * _Q4:4 * _Q4], kr[2 * _Q4:3 * _Q4]], axis=0)
    kr4 = jnp.concatenate([kr, kr, kr_sw, kr_sw], axis=0).astype(BF16)
    cols = W_PREP_ROWS
    est = 2 * cols * (IN_W * 4 + W_ROWS * 2 + 4 * ROPE_DIM * 2) + 2 * cols * IN_W * 4
    return pl.pallas_call(
        _prep_w_in_kernel,
        grid=(D_MODEL // cols,),
        in_specs=[pl.BlockSpec((IN_W, cols), lambda i: (l, i)),
                  pl.BlockSpec((4 * ROPE_DIM, cols), lambda i: (0, i))],
        out_specs=pl.BlockSpec((W_ROWS, cols), lambda i: (0, i)),
        out_shape=jax.ShapeDtypeStruct((W_ROWS, D_MODEL), BF16),
        compiler_params=pltpu.CompilerParams(dimension_semantics=("parallel",),
                                             vmem_limit_bytes=_vmem_limit(est)),
    )(w_t.reshape(DEPTH * IN_W, D_MODEL), kr4)


def _cast_kernel(w_ref, o_ref):
    o_ref[...] = w_ref[...].astype(BF16)


def _prep_w_out(w_out, l):
    rows = W_PREP_ROWS
    return pl.pallas_call(
        _cast_kernel,
        grid=(MIX_W // rows,),
        in_specs=[pl.BlockSpec((rows, D_MODEL), lambda i: (l * (MIX_W // rows) + i, 0))],
        out_specs=pl.BlockSpec((rows, D_MODEL), lambda i: (i, 0)),
        out_shape=jax.ShapeDtypeStruct((MIX_W, D_MODEL), BF16),
        compiler_params=pltpu.CompilerParams(dimension_semantics=("parallel",)),
    )(w_out.reshape(DEPTH * MIX_W, D_MODEL))


def _prep_w_uq(w):
    w3 = w.reshape(Q_LORA, B_HEADS, NOPE_DIM + ROPE_DIM)
    nope = w3[:, :, :NOPE_DIM].reshape(Q_LORA, B_HEADS * NOPE_DIM)
    rope = w3[:, :, NOPE_DIM:]
    rope_sw = jnp.concatenate([rope[..., _Q4:2 * _Q4], rope[..., 0:_Q4],
                               rope[..., 3 * _Q4:4 * _Q4], rope[..., 2 * _Q4:3 * _Q4]], axis=-1)
    return jnp.concatenate([nope, rope.reshape(Q_LORA, -1), rope_sw.reshape(Q_LORA, -1)], axis=1).astype(BF16)


def _prep_w_ukv(w):
    w3 = w.reshape(KV_LORA, B_HEADS, NOPE_DIM + V_DIM)
    return jnp.concatenate([w3[:, :, :NOPE_DIM].reshape(KV_LORA, -1),
                            w3[:, :, NOPE_DIM:].reshape(KV_LORA, -1)], axis=1).astype(BF16)


def _rope_tables():
    t = np.arange(SEQ)
    row = (t // GRID_W).astype(np.float32)
    col = (t % GRID_W).astype(np.float32)
    inv = (np.float32(ROPE_THETA) ** (-np.arange(_Q4, dtype=np.float32) / np.float32(_Q4))).astype(np.float32)
    ar, ac = row[:, None] * inv, col[:, None] * inv
    cos = np.concatenate([np.cos(ar), np.cos(ar), np.cos(ac), np.cos(ac)], axis=1)
    sin = np.concatenate([-np.sin(ar), np.sin(ar), -np.sin(ac), np.sin(ac)], axis=1)
    cos = np.concatenate([cos, np.ones((ROW_TILE, ROPE_DIM))], axis=0).astype(np.float32)
    sin = np.concatenate([sin, np.zeros((ROW_TILE, ROPE_DIM))], axis=0).astype(np.float32)
    return np.concatenate([cos, cos], axis=1), np.concatenate([sin, sin], axis=1)


_COS2, _SIN2 = _rope_tables()


def _rope_tile(t):
    return jnp.where(t < LAT_TILES, t % TILES_PER_SAMPLE, TILES_PER_SAMPLE)


def kernel(x, c, ctx, c_ctx, w_ada, b_ada, norm_g, w_in, qa_g, kva_g, w_uq, w_ukv, sgu_g, w_s, b_s, rpb,
           w_out, final_g):
    assert x.shape == (BATCH, SEQ, D_MODEL) and ctx.shape == (BATCH, CTX_LEN, D_MODEL)
    assert w_in.shape == (DEPTH, D_MODEL, IN_W)
    cc = jnp.concatenate([c, c_ctx[None, :], jnp.zeros((MOD_ROWS - BATCH - 1, D_MODEL), F32)], axis=0)
    mods = _modulation(cc, w_ada, b_ada)
    cos2, sin2 = jnp.asarray(_COS2), jnp.asarray(_SIN2)

    x_lat, x_ctx, ctx_block = x.reshape(N_LAT, D_MODEL), ctx.reshape(N_CTX, D_MODEL), 0
    for l in range(DEPTH):
        last = l == DEPTH - 1
        b_s_exp = jnp.repeat(b_s[l].T, HEAD_DIM, axis=1)
        z, ma = _inproj(x_lat, x_ctx, ctx_block, mods[l], norm_g[l], _prep_w_in(w_in, l),
                        sgu_g[l], w_s[l].astype(BF16), b_s_exp)
        q, k, vt = _upproj(z, cos2, sin2, qa_g[l], kva_g[l], _prep_w_uq(w_uq[l]), _prep_w_ukv(w_ukv[l]))
        mb, mb_ctx = _mla(q, k, vt, z, with_ctx_queries=not last)
        mc, mc_ctx = _natten(z, _na_bias_tables(rpb[l]), with_ctx_queries=not last)
        w_out_b = _prep_w_out(w_out, l)
        if last:
            out = _outproj_final(ma, mb, mc, x_lat, w_out_b, mods[l], final_g)
            return out.reshape(BATCH, SEQ, D_MODEL)
        xf = _outproj_mid(ma, mb, mb_ctx, mc, mc_ctx, x_lat, x_ctx, ctx_block, w_out_b, mods[l])
        x_lat, x_ctx, ctx_block = xf, xf, LAT_TILES
```

```python
import functools
import math

import numpy as np
import jax
import jax.numpy as jnp
from jax import lax
from jax.experimental import pallas as pl
from jax.experimental.pallas import tpu as pltpu

D_MODEL = 2048
BATCH = 2
SEQ = 4096
DEPTH = 2
GRID_W = 64
CTX_LEN = 256
EPS = 1e-6
NEG = -1e30
HEAD_DIM = 128
W_A = D_MODEL // 4
W_B = D_MODEL // 2
W_C = D_MODEL // 4
MIX_W = W_A + W_B + W_C
CHUNK = 128
A_GROUPS = W_A // HEAD_DIM
B_HEADS = W_B // HEAD_DIM
Q_LORA = D_MODEL // 4
KV_LORA = 512
NOPE_DIM = 128
ROPE_DIM = 64
V_DIM = 128
MLA_SCALE = (NOPE_DIM + ROPE_DIM) ** -0.5
ROPE_THETA = 10000.0
C_HEADS = W_C // HEAD_DIM
MAX_KH = 8
KW = 16
C_SCALE = HEAD_DIM ** -0.5
ROWS = SEQ // GRID_W

LANES = 128
SUBLANES = 8
BF16_SUBLANES = 16
MXU_DIM = 256
VMEM_BYTES_V7X = 64 * 1024 * 1024
VMEM_LIMIT_CAP = VMEM_BYTES_V7X * 7 // 8

F32 = jnp.float32
BF16 = jnp.bfloat16

N_LAT = BATCH * SEQ
N_CTX = BATCH * CTX_LEN
N_TOK = N_LAT + N_CTX
ROW_TILE = N_CTX
LAT_TILES = N_LAT // ROW_TILE
ROW_TILES = N_TOK // ROW_TILE
TILES_PER_SAMPLE = SEQ // ROW_TILE
Q_TILE = CTX_LEN
KV_CHUNK = MXU_DIM
assert ROW_TILE % CHUNK == 0 and SEQ % ROW_TILE == 0 and SEQ % Q_TILE == 0 and CTX_LEN % KV_CHUNK == 0

GM_W = 3 * W_A
OFF_CQ = 0
OFF_CKV = OFF_CQ + Q_LORA
OFF_GB = OFF_CKV + KV_LORA
OFF_QC = OFF_GB + W_B
OFF_KC = OFF_QC + W_C
OFF_VC = OFF_KC + W_C
OFF_GC = OFF_VC + W_C
OFF_KR = OFF_GC + W_C
Z_W = OFF_KR + 4 * ROPE_DIM
W_ROWS = GM_W + Z_W
Z_CHUNK = 512
assert W_ROWS % MXU_DIM == 0 and OFF_KR % (4 * ROPE_DIM) == 0

_SRC = {}
_acc = 0
for _name, _w in (("u", W_A), ("va", W_A), ("ga", W_A), ("cq", Q_LORA), ("ckv", KV_LORA), ("kr", ROPE_DIM),
                  ("gb", W_B), ("qc", W_C), ("kc", W_C), ("vc", W_C), ("gc", W_C)):
    _SRC[_name] = (_acc, _acc + _w)
    _acc += _w
IN_W = _acc

_Q4 = ROPE_DIM // 4


def _vmem_limit(nbytes):
    return int(min(VMEM_LIMIT_CAP, max(16 * 1024 * 1024, nbytes * 5 // 4)))


def _silu(x):
    return x * jax.nn.sigmoid(x)


def _rms(x, g):
    return x * lax.rsqrt(jnp.mean(x * x, axis=-1, keepdims=True) + EPS) * g


MOD_ROWS = SUBLANES
MOD_TN = 768
assert BATCH + 1 <= MOD_ROWS and (3 * D_MODEL) % MOD_TN == 0 and MOD_TN % LANES == 0


def _mod_kernel(c_ref, w_ref, b_ref, o_ref):
    a = _silu(c_ref[...])
    a_hi = a.astype(BF16)
    a_lo = (a - a_hi.astype(F32)).astype(BF16)
    w = w_ref[0]
    w_hi = w.astype(BF16)
    w_lo = (w - w_hi.astype(F32)).astype(BF16)
    r = jnp.dot(jnp.concatenate([a_hi, a_lo], axis=0), w_hi, preferred_element_type=F32)
    r_lo = jnp.dot(a_hi, w_lo, preferred_element_type=F32)
    o_ref[0] = r[0:MOD_ROWS] + r[MOD_ROWS:2 * MOD_ROWS] + r_lo + b_ref[0]


def _modulation(cc, w_ada, b_ada):
    n = 3 * D_MODEL
    est = 2 * (MOD_ROWS * D_MODEL * 4 + D_MODEL * MOD_TN * 4 + 2 * MOD_ROWS * MOD_TN * 4)
    return pl.pallas_call(
        _mod_kernel,
        grid=(DEPTH, n // MOD_TN),
        in_specs=[pl.BlockSpec((MOD_ROWS, D_MODEL), lambda l, j: (0, 0)),
                  pl.BlockSpec((1, D_MODEL, MOD_TN), lambda l, j: (l, 0, j)),
                  pl.BlockSpec((1, 1, MOD_TN), lambda l, j: (l, 0, j))],
        out_specs=pl.BlockSpec((1, MOD_ROWS, MOD_TN), lambda l, j: (l, 0, j)),
        out_shape=jax.ShapeDtypeStruct((DEPTH, MOD_ROWS, n), F32),
        compiler_params=pltpu.CompilerParams(dimension_semantics=("parallel", "parallel"),
                                             vmem_limit_bytes=_vmem_limit(est)),
    )(cc, w_ada, b_ada.reshape(DEPTH, 1, n))


def _mod_row(t):
    return jnp.where(t < LAT_TILES, t // TILES_PER_SAMPLE, BATCH)


def _lat_tile(t):
    return jnp.minimum(t, LAT_TILES - 1)


def _pick_rows(lat_ref, ctx_ref):
    return jnp.where(pl.program_id(0) < LAT_TILES, lat_ref[...], ctx_ref[...])


def _inproj_kernel(xl_ref, xc_ref, mod_ref, g_ref, w_ref, sg_ref, ws_ref, bs_ref, z_ref, ma_ref, h_ref):
    nt = (((1,), (1,)), ((), ()))
    x = _pick_rows(xl_ref, xc_ref)
    shift = mod_ref[0, :, 0:D_MODEL]
    scale = mod_ref[0, :, D_MODEL:2 * D_MODEL]
    h_ref[...] = (_rms(x, g_ref[...]) * (1.0 + scale) + shift).astype(BF16)

    def proj(r0, r1):
        return lax.dot_general(h_ref[...], w_ref[r0:r1, :], nt, preferred_element_type=F32)

    vn = _rms(jax.nn.gelu(proj(W_A, 2 * W_A)), sg_ref[...]).astype(BF16)
    front = jax.nn.gelu(proj(0, W_A)) * _silu(proj(2 * W_A, 3 * W_A))
    for c in range(ROW_TILE // CHUNK):
        r0, r1 = c * CHUNK, (c + 1) * CHUNK
        for g in range(A_GROUPS):
            c0, c1 = g * HEAD_DIM, (g + 1) * HEAD_DIM
            s = jnp.dot(ws_ref[g], vn[r0:r1, c0:c1], preferred_element_type=F32) + bs_ref[:, c0:c1]
            ma_ref[r0:r1, c0:c1] = (front[r0:r1, c0:c1] * s).astype(BF16)

    for c0 in range(0, Z_W, Z_CHUNK):
        c1 = min(c0 + Z_CHUNK, Z_W)
        z_ref[:, c0:c1] = proj(GM_W + c0, GM_W + c1).astype(BF16)


def _inproj(x_lat, x_ctx, ctx_block, mods_l, norm_g, w_in_p, sgu_g, w_s_b, b_s_exp):
    tm = ROW_TILE
    est = (3 * tm * D_MODEL * 4 + D_MODEL * W_ROWS * 2 + 2 * tm * (Z_W + W_A) * 2 + tm * D_MODEL * 2
           + 6 * tm * Z_CHUNK * 4)
    return pl.pallas_call(
        _inproj_kernel,
        grid=(ROW_TILES,),
        in_specs=[pl.BlockSpec((tm, D_MODEL), lambda t: (_lat_tile(t), 0)),
                  pl.BlockSpec((tm, D_MODEL), lambda t: (ctx_block, 0), pipeline_mode=pl.Buffered(1)),
                  pl.BlockSpec((1, 1, 3 * D_MODEL), lambda t: (_mod_row(t), 0, 0)),
                  pl.BlockSpec((1, D_MODEL), lambda t: (0, 0)),
                  pl.BlockSpec((W_ROWS, D_MODEL), lambda t: (0, 0), pipeline_mode=pl.Buffered(1)),
                  pl.BlockSpec((1, W_A), lambda t: (0, 0)),
                  pl.BlockSpec((A_GROUPS, CHUNK, CHUNK), lambda t: (0, 0, 0)),
                  pl.BlockSpec((CHUNK, W_A), lambda t: (0, 0))],
        out_specs=[pl.BlockSpec((tm, Z_W), lambda t: (t, 0)),
                   pl.BlockSpec((tm, W_A), lambda t: (t, 0))],
        out_shape=[jax.ShapeDtypeStruct((N_TOK, Z_W), BF16),
                   jax.ShapeDtypeStruct((N_TOK, W_A), BF16)],
        scratch_shapes=[pltpu.VMEM((tm, D_MODEL), BF16)],
        compiler_params=pltpu.CompilerParams(dimension_semantics=("parallel",),
                                             vmem_limit_bytes=_vmem_limit(est)),
    )(x_lat, x_ctx, mods_l.reshape(MOD_ROWS, 1, 3 * D_MODEL), norm_g.reshape(1, D_MODEL), w_in_p,
      sgu_g.reshape(1, W_A), w_s_b, b_s_exp)


QK_W = 2 * HEAD_DIM
UQ_W = B_HEADS * (NOPE_DIM + 2 * ROPE_DIM)
UKV_W = B_HEADS * (NOPE_DIM + V_DIM)
VT_ROWS = V_DIM + BF16_SUBLANES
_Q_PRESCALE = MLA_SCALE * math.log2(math.e)
assert B_HEADS % 2 == 0 and 2 * ROPE_DIM == LANES


def _upproj_kernel(cq_ref, ckv_ref, kr_ref, cos_ref, sin_ref, qg_ref, kg_ref, wq_ref, wkv_ref,
                   q_ref, k_ref, vt_ref):
    cos2 = cos_ref[...]
    sin2 = sin_ref[...]
    n_rope = B_HEADS * ROPE_DIM
    cqn = _rms(cq_ref[...].astype(F32), qg_ref[...]).astype(BF16)
    q_all = jnp.dot(cqn, wq_ref[...], preferred_element_type=F32)
    rope0 = B_HEADS * NOPE_DIM
    for h in range(B_HEADS):
        q_ref[h, :, 0:NOPE_DIM] = (q_all[:, h * NOPE_DIM:(h + 1) * NOPE_DIM] * _Q_PRESCALE).astype(BF16)
    for j in range(B_HEADS // 2):
        a = q_all[:, rope0 + j * LANES: rope0 + (j + 1) * LANES]
        a_sw = q_all[:, rope0 + n_rope + j * LANES: rope0 + n_rope + (j + 1) * LANES]
        rot = ((a * cos2 + a_sw * sin2) * _Q_PRESCALE).astype(BF16)
        q_ref[2 * j, :, NOPE_DIM:QK_W] = rot
        q_ref[2 * j + 1, :, NOPE_DIM:QK_W] = rot

    ckvn = _rms(ckv_ref[...].astype(F32), kg_ref[...]).astype(BF16)
    kv_all = jnp.dot(ckvn, wkv_ref[...], preferred_element_type=F32)
    kr = kr_ref[...].astype(F32)
    krot = kr[:, 0:LANES] * cos2 + kr[:, LANES:2 * LANES] * sin2
    lane = lax.broadcasted_iota(jnp.int32, krot.shape, 1)
    k_lo = jnp.where(lane < ROPE_DIM, krot, 0.0).astype(BF16)
    k_hi = jnp.where(lane >= ROPE_DIM, krot, 0.0).astype(BF16)
    v0 = B_HEADS * NOPE_DIM
    sub = lax.broadcasted_iota(jnp.int32, (VT_ROWS - V_DIM, KV_CHUNK), 0)
    ones_rows = jnp.where(sub == 0, 1.0, 0.0).astype(BF16)
    for h in range(B_HEADS):
        k_ref[h, :, 0:NOPE_DIM] = kv_all[:, h * NOPE_DIM:(h + 1) * NOPE_DIM].astype(BF16)
        k_ref[h, :, NOPE_DIM:QK_W] = k_lo if h % 2 == 0 else k_hi
        v_h = kv_all[:, v0 + h * V_DIM: v0 + (h + 1) * V_DIM]
        for c in range(ROW_TILE // KV_CHUNK):
            vt_ref[h, c, 0:V_DIM, :] = v_h[c * KV_CHUNK:(c + 1) * KV_CHUNK, :].T.astype(BF16)
            vt_ref[h, c, V_DIM:VT_ROWS, :] = ones_rows


def _upproj(z, cos2, sin2, qa_g, kva_g, w_uq_p, w_ukv_p):
    tm = ROW_TILE
    est = 2 * (2 * tm * Q_LORA * 2 + tm * 4 * ROPE_DIM * 2 + 2 * tm * LANES * 4 + Q_LORA * UQ_W * 2
               + KV_LORA * UKV_W * 2 + B_HEADS * tm * (2 * QK_W + V_DIM) * 2) + 4 * tm * UQ_W * 4
    return pl.pallas_call(
        _upproj_kernel,
        grid=(ROW_TILES,),
        in_specs=[pl.BlockSpec((tm, Q_LORA), lambda t: (t, OFF_CQ // Q_LORA)),
                  pl.BlockSpec((tm, KV_LORA), lambda t: (t, OFF_CKV // KV_LORA)),
                  pl.BlockSpec((tm, 4 * ROPE_DIM), lambda t: (t, OFF_KR // (4 * ROPE_DIM))),
                  pl.BlockSpec((tm, LANES), lambda t: (_rope_tile(t), 0)),
                  pl.BlockSpec((tm, LANES), lambda t: (_rope_tile(t), 0)),
                  pl.BlockSpec((1, Q_LORA), lambda t: (0, 0)),
                  pl.BlockSpec((1, KV_LORA), lambda t: (0, 0)),
                  pl.BlockSpec((Q_LORA, UQ_W), lambda t: (0, 0)),
                  pl.BlockSpec((KV_LORA, UKV_W), lambda t: (0, 0))],
        out_specs=[pl.BlockSpec((B_HEADS, tm, QK_W), lambda t: (0, t, 0)),
                   pl.BlockSpec((B_HEADS, tm, QK_W), lambda t: (0, t, 0)),
                   pl.BlockSpec((B_HEADS, tm // KV_CHUNK, VT_ROWS, KV_CHUNK), lambda t: (0, t, 0, 0))],
        out_shape=[jax.ShapeDtypeStruct((B_HEADS, N_TOK, QK_W), BF16),
                   jax.ShapeDtypeStruct((B_HEADS, N_TOK, QK_W), BF16),
                   jax.ShapeDtypeStruct((B_HEADS, N_TOK // KV_CHUNK, VT_ROWS, KV_CHUNK), BF16)],
        compiler_params=pltpu.CompilerParams(dimension_semantics=("parallel",),
                                             vmem_limit_bytes=_vmem_limit(est)),
    )(z, z, z, cos2, sin2, qa_g.reshape(1, Q_LORA), kva_g.reshape(1, KV_LORA), w_uq_p, w_ukv_p)


MLA_TK = 512
MLA_TQ = 2048
MLA_TQG = MXU_DIM
assert SEQ % MLA_TK == 0 and MLA_TK % KV_CHUNK == 0 and SEQ % MLA_TQ == 0 and MLA_TQ % MLA_TQG == 0


def _attend_t(qs, chunks):
    def scores(k, q):
        return lax.dot_general(k, q, (((1,), (1,)), ((), ())), preferred_element_type=F32)

    state = [None] * len(qs)
    s_next = [scores(chunks[0][0], q) for q in qs]
    for j, (_, vts) in enumerate(chunks):
        for g, q in enumerate(qs):
            s = s_next[g]
            if j + 1 < len(chunks):
                s_next[g] = scores(chunks[j + 1][0], q)
            s_max = jnp.max(s, axis=0, keepdims=True)
            m_new = s_max if state[g] is None else jnp.maximum(state[g][0], s_max)
            pv = None
            for n, vt in enumerate(vts):
                p = jnp.exp2(s[n * KV_CHUNK:(n + 1) * KV_CHUNK] - m_new)
                d = jnp.dot(vt, p.astype(BF16), preferred_element_type=F32)
                pv = d if n == 0 else pv + d
            if state[g] is None:
                state[g] = (m_new, pv)
            else:
                m, acc = state[g]
                state[g] = (m_new, jnp.exp2(m - m_new) * acc + pv)
    return [acc[0:V_DIM] / acc[V_DIM:V_DIM + 1] for _, acc in state]


def _mla_lat_kernel(q_ref, kl_ref, vtl_ref, kc_ref, vtc_ref, gb_ref, o_ref):
    per = MLA_TK // KV_CHUNK
    chunks = [(kl_ref[0, j * MLA_TK:(j + 1) * MLA_TK, :], [vtl_ref[0, j * per + n] for n in range(per)])
              for j in range(SEQ // MLA_TK)]
    chunks.append((kc_ref[0], [vtc_ref[0, n] for n in range(CTX_LEN // KV_CHUNK)]))
    groups = [(g * MLA_TQG, (g + 1) * MLA_TQG) for g in range(MLA_TQ // MLA_TQG)]
    outs = _attend_t([q_ref[0, lo:hi, :] for lo, hi in groups], chunks)
    for (lo, hi), o_t in zip(groups, outs):
        o_ref[lo:hi, :] = (o_t.T * _silu(gb_ref[lo:hi, :].astype(F32))).astype(BF16)


def _mla_ctx_kernel(q_ref, kc_ref, vtc_ref, gb_ref, o_ref):
    (o_t,) = _attend_t([q_ref[0]], [(kc_ref[0], [vtc_ref[0, n] for n in range(CTX_LEN // KV_CHUNK)])])
    o = o_t.T
    o_ref[...] = (o * _silu(gb_ref[...].astype(F32))).astype(BF16)


def _mla(q, k, vt, z, with_ctx_queries):
    ctx_tile = N_LAT // CTX_LEN
    tq = MLA_TQ
    nq = SEQ // tq
    est = 2 * (tq * QK_W * 2 + SEQ * QK_W * 2 + SEQ * V_DIM * 2 + CTX_LEN * (QK_W + V_DIM) * 2
               + 2 * tq * HEAD_DIM * 2) + 8 * MLA_TK * tq * 4
    mb = pl.pallas_call(
        _mla_lat_kernel,
        grid=(BATCH, B_HEADS, nq),
        in_specs=[pl.BlockSpec((1, tq, QK_W), lambda b, h, i: (h, b * nq + i, 0)),
                  pl.BlockSpec((1, SEQ, QK_W), lambda b, h, i: (h, b, 0)),
                  pl.BlockSpec((1, SEQ // KV_CHUNK, VT_ROWS, KV_CHUNK), lambda b, h, i: (h, b, 0, 0)),
                  pl.BlockSpec((1, CTX_LEN, QK_W), lambda b, h, i: (h, ctx_tile + b, 0)),
                  pl.BlockSpec((1, CTX_LEN // KV_CHUNK, VT_ROWS, KV_CHUNK), lambda b, h, i: (h, ctx_tile + b, 0, 0)),
                  pl.BlockSpec((tq, HEAD_DIM), lambda b, h, i: (b * nq + i, OFF_GB // HEAD_DIM + h))],
        out_specs=pl.BlockSpec((tq, HEAD_DIM), lambda b, h, i: (b * nq + i, h)),
        out_shape=jax.ShapeDtypeStruct((N_LAT, W_B), BF16),
        compiler_params=pltpu.CompilerParams(dimension_semantics=("parallel", "parallel", "parallel"),
                                             vmem_limit_bytes=_vmem_limit(est)),
    )(q, k, vt, k, vt, z)
    if not with_ctx_queries:
        return mb, None
    mb_ctx = pl.pallas_call(
        _mla_ctx_kernel,
        grid=(BATCH, B_HEADS),
        in_specs=[pl.BlockSpec((1, CTX_LEN, QK_W), lambda b, h: (h, ctx_tile + b, 0)),
                  pl.BlockSpec((1, CTX_LEN, QK_W), lambda b, h: (h, ctx_tile + b, 0)),
                  pl.BlockSpec((1, CTX_LEN // KV_CHUNK, VT_ROWS, KV_CHUNK), lambda b, h: (h, ctx_tile + b, 0, 0)),
                  pl.BlockSpec((CTX_LEN, HEAD_DIM), lambda b, h: (ctx_tile + b, OFF_GB // HEAD_DIM + h))],
        out_specs=pl.BlockSpec((CTX_LEN, HEAD_DIM), lambda b, h: (b, h)),
        out_shape=jax.ShapeDtypeStruct((N_CTX, W_B), BF16),
        compiler_params=pltpu.CompilerParams(dimension_semantics=("parallel", "parallel")),
    )(q, k, vt, z)
    return mb, mb_ctx


NA_QROWS = Q_TILE // GRID_W
NA_KROWS = NA_QROWS + MAX_KH
NA_KEYS = NA_KROWS * GRID_W
assert ROWS >= NA_KROWS and NA_KEYS % MXU_DIM == 0 and NA_KROWS % 2 == 0 and 2 * GRID_W == LANES


def _na_strip_row(j):
    return int(np.clip(j * NA_QROWS - MAX_KH // 2, 0, ROWS - NA_KROWS))


def _na_tables():
    def one(j):
        s = _na_strip_row(j)
        valid = np.zeros((NA_QROWS, NA_KROWS), bool)
        d = np.zeros((NA_QROWS, NA_KROWS), np.int32)
        for a in range(NA_QROWS):
            r = j * NA_QROWS + a
            r0 = int(np.clip(r - MAX_KH // 2, 0, ROWS - MAX_KH))
            for i in range(NA_KROWS):
                valid[a, i] = 0 <= s + i - r0 < MAX_KH
                d[a, i] = s + i - r + (MAX_KH - 1) if valid[a, i] else 0
        return valid, d
    n_tiles = ROWS // NA_QROWS
    first, mid, last = one(0), one(1), one(n_tiles - 1)
    for j in range(1, n_tiles - 1):
        v, d = one(j)
        assert (v == mid[0]).all() and (d == mid[1]).all()
    return np.stack([first[0], mid[0], last[0]]), np.stack([first[1], mid[1], last[1]])


_NA_VALID, _NA_DROW = _na_tables()


def _na_bias_tables(rpb_l):
    col = np.arange(GRID_W)
    c0 = np.clip(col - KW // 2, 0, GRID_W - KW)
    col_ok = (col[None, :] >= c0[:, None]) & (col[None, :] < c0[:, None] + KW)
    dc = np.clip(col[None, :] - col[:, None], -(KW - 1), KW - 1) + (KW - 1)
    onehot = (dc.reshape(-1)[None, :] == np.arange(2 * KW - 1)[:, None]).astype(np.float32)
    t1 = jnp.dot(rpb_l.reshape(C_HEADS * (2 * MAX_KH - 1), 2 * KW - 1), jnp.asarray(onehot),
                 precision=lax.Precision.HIGHEST).reshape(C_HEADS, 2 * MAX_KH - 1, GRID_W, GRID_W)
    t1 = jnp.where(jnp.asarray(col_ok)[None, None], t1, NEG) / C_SCALE
    return jnp.concatenate([t1, t1], axis=-1)


_NA_EXP2_SCALE = C_SCALE * math.log2(math.e)


def _softmax_rows(blocks):
    m = functools.reduce(jnp.maximum, [jnp.max(s, axis=-1, keepdims=True) for s in blocks])
    ps = [jnp.exp2((s - m) * _NA_EXP2_SCALE) for s in blocks]
    return ps, functools.reduce(jnp.add, [jnp.sum(p, axis=-1, keepdims=True) for p in ps])


def _assemble_bias(t_ref, bias_ref):
    lane = lax.broadcasted_iota(jnp.int32, (GRID_W, 2 * GRID_W), 1)
    neg = jnp.full((GRID_W, 2 * GRID_W), NEG / C_SCALE, F32)
    for cls in range(3):
        for a in range(NA_QROWS):
            for pair in range(NA_KROWS // 2):
                halves = [t_ref[0, int(_NA_DROW[cls, a, i])] if _NA_VALID[cls, a, i] else neg
                          for i in (2 * pair, 2 * pair + 1)]
                tile = halves[0] if halves[0] is halves[1] else jnp.where(lane < GRID_W, halves[0], halves[1])
                bias_ref[cls, a * GRID_W:(a + 1) * GRID_W, pair * 2 * GRID_W:(pair + 1) * 2 * GRID_W] = tile


def _natten_kernel(q_ref, kl_ref, vl_ref, kc_ref, vc_ref, t_ref, g_ref, o_ref, bias_ref):
    nt = (((1,), (1,)), ((), ()))
    n_tiles = ROWS // NA_QROWS
    _assemble_bias(t_ref, bias_ref)

    def tile(j, carry):
        rows = pl.ds(pl.multiple_of(j * Q_TILE, Q_TILE), Q_TILE)
        strip_row = jnp.clip(j * NA_QROWS - MAX_KH // 2, 0, ROWS - NA_KROWS)
        strip = pl.ds(pl.multiple_of(strip_row * GRID_W, GRID_W), NA_KEYS)
        cls = jnp.where(j == 0, 0, jnp.where(j == n_tiles - 1, 2, 1))
        q = q_ref[rows, :]
        s_nb = lax.dot_general(q, kl_ref[strip, :], nt, preferred_element_type=F32) + bias_ref[cls]
        s_cx = lax.dot_general(q, kc_ref[...], nt, preferred_element_type=F32)
        (p_nb, p_cx), l = _softmax_rows([s_nb, s_cx])
        o = jnp.dot(p_nb.astype(BF16), vl_ref[strip, :], preferred_element_type=F32)
        o += jnp.dot(p_cx.astype(BF16), vc_ref[...], preferred_element_type=F32)
        o_ref[rows, :] = (o / l * _silu(g_ref[rows, :].astype(F32))).astype(BF16)
        return carry

    lax.fori_loop(0, n_tiles, tile, 0, unroll=4)


def _natten_ctx_kernel(q_ref, kc_ref, vc_ref, g_ref, o_ref):
    s =lax.dot_general(q_ref[...], kc_ref[...], (((1,), (1,)), ((), ())), preferred_element_type=F32)
    (p,), l = _softmax_rows([s])
    o = jnp.dot(p.astype(BF16), vc_ref[...], preferred_element_type=F32)
    o_ref[...] = (o / l * _silu(g_ref[...].astype(F32))).astype(BF16)


def _natten(z, bias_tab, with_ctx_queries):
    ctx_tile = N_LAT // CTX_LEN
    hd = HEAD_DIM
    est = (2 * (5 * SEQ * hd * 2 + 2 * CTX_LEN * hd * 2 + 3 * Q_TILE * NA_KEYS * 4)
           + 12 * Q_TILE * (NA_KEYS + CTX_LEN) * 4)
    mc = pl.pallas_call(
        _natten_kernel,
        grid=(BATCH, C_HEADS),
        in_specs=[pl.BlockSpec((SEQ, hd), lambda b, h: (b, OFF_QC // hd + h)),
                  pl.BlockSpec((SEQ, hd), lambda b, h: (b, OFF_KC // hd + h)),
                  pl.BlockSpec((SEQ, hd), lambda b, h: (b, OFF_VC // hd + h)),
                  pl.BlockSpec((CTX_LEN, hd), lambda b, h: (ctx_tile + b, OFF_KC // hd + h)),
                  pl.BlockSpec((CTX_LEN, hd), lambda b, h: (ctx_tile + b, OFF_VC // hd + h)),
                  pl.BlockSpec((1, 2 * MAX_KH - 1, GRID_W, 2 * GRID_W), lambda b, h: (h, 0, 0, 0)),
                  pl.BlockSpec((SEQ, hd), lambda b, h: (b, OFF_GC // hd + h))],
        out_specs=pl.BlockSpec((SEQ, hd), lambda b, h: (b, h)),
        out_shape=jax.ShapeDtypeStruct((N_LAT, W_C), BF16),
        scratch_shapes=[pltpu.VMEM((3, Q_TILE, NA_KEYS), F32)],
        compiler_params=pltpu.CompilerParams(dimension_semantics=("parallel", "parallel"),
                                             vmem_limit_bytes=_vmem_limit(est)),
    )(z, z, z, z, z, bias_tab, z)
    if not with_ctx_queries:
        return mc, None
    mc_ctx = pl.pallas_call(
        _natten_ctx_kernel,
        grid=(BATCH, C_HEADS),
        in_specs=[pl.BlockSpec((CTX_LEN, hd), lambda b, h: (ctx_tile + b, OFF_QC // hd + h)),
                  pl.BlockSpec((CTX_LEN, hd), lambda b, h: (ctx_tile + b, OFF_KC // hd + h)),
                  pl.BlockSpec((CTX_LEN, hd), lambda b, h: (ctx_tile + b, OFF_VC // hd + h)),
                  pl.BlockSpec((CTX_LEN, hd), lambda b, h: (ctx_tile + b, OFF_GC // hd + h))],
        out_specs=pl.BlockSpec((CTX_LEN, hd), lambda b, h: (b, h)),
        out_shape=jax.ShapeDtypeStruct((N_CTX, W_C), BF16),
        compiler_params=pltpu.CompilerParams(dimension_semantics=("parallel", "parallel")),
    )(z, z, z, z)
    return mc, mc_ctx


OUT_CHUNK = 512


def _outproj_body(ma, mb, mc, x, w_ref, mod_ref, o_ref):
    gate = mod_ref[0, :, 2 * D_MODEL:3 * D_MODEL]
    for c0 in range(0, D_MODEL, OUT_CHUNK):
        c1 = c0 + OUT_CHUNK
        y = jnp.dot(ma, w_ref[0:W_A, c0:c1], preferred_element_type=F32)
        y += jnp.dot(mb, w_ref[W_A:W_A + W_B, c0:c1], preferred_element_type=F32)
        y += jnp.dot(mc, w_ref[W_A + W_B:D_MODEL, c0:c1], preferred_element_type=F32)
        o_ref[:, c0:c1] = x[:, c0:c1] + gate[:, c0:c1] * y


def _outproj_mid_kernel(ma_ref, mbl_ref, mbc_ref, mcl_ref, mcc_ref, xl_ref, xc_ref, w_ref, mod_ref, o_ref):
    _outproj_body(ma_ref[...], _pick_rows(mbl_ref, mbc_ref), _pick_rows(mcl_ref, mcc_ref),
                  _pick_rows(xl_ref, xc_ref), w_ref, mod_ref, o_ref)


def _outproj_final_kernel(ma_ref, mb_ref, mc_ref, x_ref, w_ref, mod_ref, fg_ref, o_ref):
    _outproj_body(ma_ref[...], mb_ref[...], mc_ref[...], x_ref[...], w_ref, mod_ref, o_ref)
    o_ref[...] = _rms(o_ref[...], fg_ref[...])


def _outproj_mid(ma, mb, mb_ctx, mc, mc_ctx, x_lat, x_ctx, ctx_block, w_out_b, mods_l):
    tm = ROW_TILE
    est = (3 * tm * D_MODEL * 2 + D_MODEL * D_MODEL * 2 + 5 * tm * D_MODEL * 4 + 3 * tm * OUT_CHUNK * 4)
    lat = lambda w: pl.BlockSpec((tm, w), lambda t: (_lat_tile(t), 0))
    ctx = lambda w, blk: pl.BlockSpec((tm, w), lambda t: (blk, 0), pipeline_mode=pl.Buffered(1))
    return pl.pallas_call(
        _outproj_mid_kernel,
        grid=(ROW_TILES,),
        in_specs=[pl.BlockSpec((tm, W_A), lambda t: (t, 0)),
                  lat(W_B), ctx(W_B, 0), lat(W_C), ctx(W_C, 0), lat(D_MODEL), ctx(D_MODEL, ctx_block),
                  pl.BlockSpec((D_MODEL, D_MODEL), lambda t: (0, 0), pipeline_mode=pl.Buffered(1)),
                  pl.BlockSpec((1, 1, 3 * D_MODEL), lambda t: (_mod_row(t), 0, 0))],
        out_specs=pl.BlockSpec((tm, D_MODEL), lambda t: (t, 0)),
        out_shape=jax.ShapeDtypeStruct((N_TOK, D_MODEL), F32),
        compiler_params=pltpu.CompilerParams(dimension_semantics=("parallel",),
                                             vmem_limit_bytes=_vmem_limit(est)),
    )(ma, mb, mb_ctx, mc, mc_ctx, x_lat, x_ctx, w_out_b, mods_l.reshape(MOD_ROWS, 1, 3 * D_MODEL))


def _outproj_final(ma, mb, mc, x_lat, w_out_b, mods_l, final_g):
    tm = ROW_TILE
    est = (2 * tm * D_MODEL * 2 + D_MODEL * D_MODEL * 2 + 4 * tm * D_MODEL * 4 + 3 * tm * OUT_CHUNK * 4)
    return pl.pallas_call(
        _outproj_final_kernel,
        grid=(LAT_TILES,),
        in_specs=[pl.BlockSpec((tm, W_A), lambda t: (t, 0)),
                  pl.BlockSpec((tm, W_B), lambda t: (t, 0)),
                  pl.BlockSpec((tm, W_C), lambda t: (t, 0)),
                  pl.BlockSpec((tm, D_MODEL), lambda t: (t, 0)),
                  pl.BlockSpec((D_MODEL, D_MODEL), lambda t: (0, 0), pipeline_mode=pl.Buffered(1)),
                  pl.BlockSpec((1, 1, 3 * D_MODEL), lambda t: (_mod_row(t), 0, 0)),
                  pl.BlockSpec((1, D_MODEL), lambda t: (0, 0))],
        out_specs=pl.BlockSpec((tm, D_MODEL), lambda t: (t, 0)),
        out_shape=jax.ShapeDtypeStruct((N_LAT, D_MODEL), F32),
        compiler_params=pltpu.CompilerParams(dimension_semantics=("parallel",),
                                             vmem_limit_bytes=_vmem_limit(est)),
    )(ma, mb, mc, x_lat, w_out_b, mods_l.reshape(MOD_ROWS, 1, 3 * D_MODEL), final_g.reshape(1, D_MODEL))


W_PREP_ROWS = MXU_DIM


def _prep_w_in_kernel(w_ref, kr_ref, o_ref):
    h0, h1 = _SRC["u"][0], _SRC["ckv"][1]
    t0, t1 = _SRC["gb"][0], _SRC["gc"][1]
    o_ref[0:h1 - h0, :] = w_ref[h0:h1, :].astype(BF16)
    o_ref[GM_W + OFF_GB:GM_W + OFF_KR, :] = w_ref[t0:t1, :].astype(BF16)
    o_ref[GM_W + OFF_KR:W_ROWS, :] = kr_ref[...]


def _prep_w_in(w_in, l):
    assert _SRC["ckv"][1] == GM_W + OFF_GB and OFF_GB + (_SRC["gc"][1] - _SRC["gb"][0]) == OFF_KR
    w_t = jnp.swapaxes(w_in, 1, 2)
    kr = w_t[l, _SRC["kr"][0]:_SRC["kr"][1], :]
    kr_sw = jnp.concatenate([kr[_Q4:2 * _Q4], kr[0:_Q4], kr[3 * _Q4:4 * _Q4], kr[2 * _Q4:3 * _Q4]], axis=0)
    kr4 = jnp.concatenate([kr, kr, kr_sw, kr_sw], axis=0).astype(BF16)
    cols = W_PREP_ROWS
    est = 2 * cols * (IN_W * 4 + W_ROWS * 2 + 4 * ROPE_DIM * 2) + 2 * cols * IN_W * 4
    return pl.pallas_call(
        _prep_w_in_kernel,
        grid=(D_MODEL // cols,),
        in_specs=[pl.BlockSpec((IN_W, cols), lambda i: (l, i)),
                  pl.BlockSpec((4 * ROPE_DIM, cols), lambda i: (0, i))],
        out_specs=pl.BlockSpec((W_ROWS, cols), lambda i: (0, i)),
        out_shape=jax.ShapeDtypeStruct((W_ROWS, D_MODEL), BF16),
        compiler_params=pltpu.CompilerParams(dimension_semantics=("parallel",),
                                             vmem_limit_bytes=_vmem_limit(est)),
    )(w_t.reshape(DEPTH * IN_W, D_MODEL), kr4)


def _cast_kernel(w_ref, o_ref):
    o_ref[...] = w_ref[...].astype(BF16)


def _prep_w_out(w_out, l):
    rows = W_PREP_ROWS
    return pl.pallas_call(
        _cast_kernel,
        grid=(MIX_W // rows,),
        in_specs=[pl.BlockSpec((rows, D_MODEL), lambda i: (l * (MIX_W // rows) + i, 0))],
        out_specs=pl.BlockSpec((rows, D_MODEL), lambda i: (i, 0)),
        out_shape=jax.ShapeDtypeStruct((MIX_W, D_MODEL), BF16),
        compiler_params=pltpu.CompilerParams(dimension_semantics=("parallel",)),
    )(w_out.reshape(DEPTH * MIX_W, D_MODEL))


def _prep_w_uq(w):
    w3 = w.reshape(Q_LORA, B_HEADS, NOPE_DIM + ROPE_DIM)
    nope = w3[:, :, :NOPE_DIM].reshape(Q_LORA, B_HEADS * NOPE_DIM)
    rope = w3[:, :, NOPE_DIM:]
    rope_sw = jnp.concatenate([rope[..., _Q4:2 * _Q4], rope[..., 0:_Q4],
                               rope[..., 3 * _Q4:4 * _Q4], rope[..., 2 * _Q4:3 * _Q4]], axis=-1)
    return jnp.concatenate([nope, rope.reshape(Q_LORA, -1), rope_sw.reshape(Q_LORA, -1)], axis=1).astype(BF16)


def _prep_w_ukv(w):
    w3 = w.reshape(KV_LORA, B_HEADS, NOPE_DIM + V_DIM)
    return jnp.concatenate([w3[:, :, :NOPE_DIM].reshape(KV_LORA, -1),
                            w3[:, :, NOPE_DIM:].reshape(KV_LORA, -1)], axis=1).astype(BF16)


def _rope_tables():
    t = np.arange(SEQ)
    row = (t // GRID_W).astype(np.float32)
    col = (t % GRID_W).astype(np.float32)
    inv = (np.float32(ROPE_THETA) ** (-np.arange(_Q4, dtype=np.float32) / np.float32(_Q4))).astype(np.float32)
    ar, ac = row[:, None] * inv, col[:, None] * inv
    cos = np.concatenate([np.cos(ar), np.cos(ar), np.cos(ac), np.cos(ac)], axis=1)
    sin = np.concatenate([-np.sin(ar), np.sin(ar), -np.sin(ac), np.sin(ac)], axis=1)
    cos = np.concatenate([cos, np.ones((ROW_TILE, ROPE_DIM))], axis=0).astype(np.float32)
    sin = np.concatenate([sin, np.zeros((ROW_TILE, ROPE_DIM))], axis=0).astype(np.float32)
    return np.concatenate([cos, cos], axis=1), np.concatenate([sin, sin], axis=1)


_COS2, _SIN2 = _rope_tables()


def _rope_tile(t):
    return jnp.where(t < LAT_TILES, t % TILES_PER_SAMPLE, TILES_PER_SAMPLE)


def kernel(x, c, ctx, c_ctx, w_ada, b_ada, norm_g, w_in, qa_g, kva_g, w_uq, w_ukv, sgu_g, w_s, b_s, rpb,
           w_out, final_g):
    assert x.shape == (BATCH, SEQ, D_MODEL) and ctx.shape == (BATCH, CTX_LEN, D_MODEL)
    assert w_in.shape == (DEPTH, D_MODEL, IN_W)
    cc = jnp.concatenate([c, c_ctx[None, :], jnp.zeros((MOD_ROWS - BATCH - 1, D_MODEL), F32)], axis=0)
    mods = _modulation(cc, w_ada, b_ada)
    cos2, sin2 = jnp.asarray(_COS2), jnp.asarray(_SIN2)

    x_lat, x_ctx, ctx_block = x.reshape(N_LAT, D_MODEL), ctx.reshape(N_CTX, D_MODEL), 0
    for l in range(DEPTH):
        last = l == DEPTH - 1
        b_s_exp = jnp.repeat(b_s[l].T, HEAD_DIM, axis=1)
        z, ma = _inproj(x_lat, x_ctx, ctx_block, mods[l], norm_g[l], _prep_w_in(w_in, l),
                        sgu_g[l], w_s[l].astype(BF16), b_s_exp)
        q, k, vt = _upproj(z, cos2, sin2, qa_g[l], kva_g[l], _prep_w_uq(w_uq[l]), _prep_w_ukv(w_ukv[l]))
        mb, mb_ctx = _mla(q, k, vt, z, with_ctx_queries=not last)
        mc, mc_ctx = _natten(z, _na_bias_tables(rpb[l]), with_ctx_queries=not last)
        w_out_b = _prep_w_out(w_out, l)
        if last:
            out = _outproj_final(ma, mb, mc, x_lat, w_out_b, mods[l], final_g)
            return out.reshape(BATCH, SEQ, D_MODEL)
        xf = _outproj_mid(ma, mb, mb_ctx, mc, mc_ctx, x_lat, x_ctx, ctx_block, w_out_b, mods[l])
        x_lat, x_ctx, ctx_block = xf, xf, LAT_TILES
```

```python
import functools
import math

import numpy as np
import jax
import jax.numpy as jnp
from jax import lax
from jax.experimental import pallas as pl
from jax.experimental.pallas import tpu as pltpu

D_MODEL = 2048
BATCH = 2
SEQ = 4096
DEPTH = 2
GRID_W = 64
CTX_LEN = 256
EPS = 1e-6
NEG = -1e30
HEAD_DIM = 128
W_A = D_MODEL // 4
W_B = D_MODEL // 2
W_C = D_MODEL // 4
MIX_W = W_A + W_B + W_C
CHUNK = 128
A_GROUPS = W_A // HEAD_DIM
B_HEADS = W_B // HEAD_DIM
Q_LORA = D_MODEL // 4
KV_LORA = 512
NOPE_DIM = 128
ROPE_DIM = 64
V_DIM = 128
MLA_SCALE = (NOPE_DIM + ROPE_DIM) ** -0.5
ROPE_THETA = 10000.0
C_HEADS = W_C // HEAD_DIM
MAX_KH = 8
KW = 16
C_SCALE = HEAD_DIM ** -0.5
ROWS = SEQ // GRID_W

LANES = 128
SUBLANES = 8
BF16_SUBLANES = 16
MXU_DIM = 256
VMEM_BYTES_V7X = 64 * 1024 * 1024
VMEM_LIMIT_CAP = VMEM_BYTES_V7X * 7 // 8

F32 = jnp.float32
BF16 = jnp.bfloat16

N_LAT = BATCH * SEQ
N_CTX = BATCH * CTX_LEN
N_TOK = N_LAT + N_CTX
ROW_TILE = N_CTX
LAT_TILES = N_LAT // ROW_TILE
ROW_TILES = N_TOK // ROW_TILE
TILES_PER_SAMPLE = SEQ // ROW_TILE
Q_TILE = CTX_LEN
KV_CHUNK = MXU_DIM
assert ROW_TILE % CHUNK == 0 and SEQ % ROW_TILE == 0 and SEQ % Q_TILE == 0 and CTX_LEN % KV_CHUNK == 0

GM_W = 3 * W_A
OFF_CQ = 0
OFF_CKV = OFF_CQ + Q_LORA
OFF_GB = OFF_CKV + KV_LORA
OFF_QC = OFF_GB + W_B
OFF_KC = OFF_QC + W_C
OFF_VC = OFF_KC + W_C
OFF_GC = OFF_VC + W_C
OFF_KR = OFF_GC + W_C
Z_W = OFF_KR + 4 * ROPE_DIM
W_ROWS = GM_W + Z_W
Z_CHUNK = 512
assert W_ROWS % MXU_DIM == 0 and OFF_KR % (4 * ROPE_DIM) == 0

_SRC = {}
_acc = 0
for _name, _w in (("u", W_A), ("va", W_A), ("ga", W_A), ("cq", Q_LORA), ("ckv", KV_LORA), ("kr", ROPE_DIM),
                  ("gb", W_B), ("qc", W_C), ("kc", W_C), ("vc", W_C), ("gc", W_C)):
    _SRC[_name] = (_acc, _acc + _w)
    _acc += _w
IN_W = _acc

_Q4 = ROPE_DIM // 4


def _vmem_limit(nbytes):
    return int(min(VMEM_LIMIT_CAP, max(16 * 1024 * 1024, nbytes * 5 // 4)))


def _silu(x):
    return x * jax.nn.sigmoid(x)


def _rms(x, g):
    return x * lax.rsqrt(jnp.mean(x * x, axis=-1, keepdims=True) + EPS) * g


MOD_ROWS = SUBLANES
MOD_TN = 768
assert BATCH + 1 <= MOD_ROWS and (3 * D_MODEL) % MOD_TN == 0 and MOD_TN % LANES == 0


def _mod_kernel(c_ref, w_ref, b_ref, o_ref):
    a = _silu(c_ref[...])
    a_hi = a.astype(BF16)
    a_lo = (a - a_hi.astype(F32)).astype(BF16)
    w = w_ref[0]
    w_hi = w.astype(BF16)
    w_lo = (w - w_hi.astype(F32)).astype(BF16)
    r = jnp.dot(jnp.concatenate([a_hi, a_lo], axis=0), w_hi, preferred_element_type=F32)
    r_lo = jnp.dot(a_hi, w_lo, preferred_element_type=F32)
    o_ref[0] = r[0:MOD_ROWS] + r[MOD_ROWS:2 * MOD_ROWS] + r_lo + b_ref[0]


def _modulation(cc, w_ada, b_ada):
    n = 3 * D_MODEL
    est = 2 * (MOD_ROWS * D_MODEL * 4 + D_MODEL * MOD_TN * 4 + 2 * MOD_ROWS * MOD_TN * 4)
    return pl.pallas_call(
        _mod_kernel,
        grid=(DEPTH, n // MOD_TN),
        in_specs=[pl.BlockSpec((MOD_ROWS, D_MODEL), lambda l, j: (0, 0)),
                  pl.BlockSpec((1, D_MODEL, MOD_TN), lambda l, j: (l, 0, j)),
                  pl.BlockSpec((1, 1, MOD_TN), lambda l, j: (l, 0, j))],
        out_specs=pl.BlockSpec((1, MOD_ROWS, MOD_TN), lambda l, j: (l, 0, j)),
        out_shape=jax.ShapeDtypeStruct((DEPTH, MOD_ROWS, n), F32),
        compiler_params=pltpu.CompilerParams(dimension_semantics=("parallel", "parallel"),
                                             vmem_limit_bytes=_vmem_limit(est)),
    )(cc, w_ada, b_ada.reshape(DEPTH, 1, n))


def _mod_row(t):
    return jnp.where(t < LAT_TILES, t // TILES_PER_SAMPLE, BATCH)


def _lat_tile(t):
    return jnp.minimum(t, LAT_TILES - 1)


def _pick_rows(lat_ref, ctx_ref):
    return jnp.where(pl.program_id(0) < LAT_TILES, lat_ref[...], ctx_ref[...])


def _inproj_kernel(xl_ref, xc_ref, mod_ref, g_ref, w_ref, sg_ref, ws_ref, bs_ref, z_ref, ma_ref, h_ref):
    nt = (((1,), (1,)), ((), ()))
    x = _pick_rows(xl_ref, xc_ref)
    shift = mod_ref[0, :, 0:D_MODEL]
    scale = mod_ref[0, :, D_MODEL:2 * D_MODEL]
    h_ref[...] = (_rms(x, g_ref[...]) * (1.0 + scale) + shift).astype(BF16)

    def proj(r0, r1):
        return lax.dot_general(h_ref[...], w_ref[r0:r1, :], nt, preferred_element_type=F32)

    vn = _rms(jax.nn.gelu(proj(W_A, 2 * W_A)), sg_ref[...]).astype(BF16)
    front = jax.nn.gelu(proj(0, W_A)) * _silu(proj(2 * W_A, 3 * W_A))
    for c in range(ROW_TILE // CHUNK):
        r0, r1 = c * CHUNK, (c + 1) * CHUNK
        for g in range(A_GROUPS):
            c0, c1 = g * HEAD_DIM, (g + 1) * HEAD_DIM
            s = jnp.dot(ws_ref[g], vn[r0:r1, c0:c1], preferred_element_type=F32) + bs_ref[:, c0:c1]
            ma_ref[r0:r1, c0:c1] = (front[r0:r1, c0:c1] * s).astype(BF16)

    for c0 in range(0, Z_W, Z_CHUNK):
        c1 = min(c0 + Z_CHUNK, Z_W)
        z_ref[:, c0:c1] = proj(GM_W + c0, GM_W + c1).astype(BF16)


def _inproj(x_lat, x_ctx, ctx_block, mods_l, norm_g, w_in_p, sgu_g, w_s_b, b_s_exp):
    tm = ROW_TILE
    est = (3 * tm * D_MODEL * 4 + D_MODEL * W_ROWS * 2 + 2 * tm * (Z_W + W_A) * 2 + tm * D_MODEL * 2
           + 6 * tm * Z_CHUNK * 4)
    return pl.pallas_call(
        _inproj_kernel,
        grid=(ROW_TILES,),
        in_specs=[pl.BlockSpec((tm, D_MODEL), lambda t: (_lat_tile(t), 0)),
                  pl.BlockSpec((tm, D_MODEL), lambda t: (ctx_block, 0), pipeline_mode=pl.Buffered(1)),
                  pl.BlockSpec((1, 1, 3 * D_MODEL), lambda t: (_mod_row(t), 0, 0)),
                  pl.BlockSpec((1, D_MODEL), lambda t: (0, 0)),
                  pl.BlockSpec((W_ROWS, D_MODEL), lambda t: (0, 0), pipeline_mode=pl.Buffered(1)),
                  pl.BlockSpec((1, W_A), lambda t: (0, 0)),
                  pl.BlockSpec((A_GROUPS, CHUNK, CHUNK), lambda t: (0, 0, 0)),
                  pl.BlockSpec((CHUNK, W_A), lambda t: (0, 0))],
        out_specs=[pl.BlockSpec((tm, Z_W), lambda t: (t, 0)),
                   pl.BlockSpec((tm, W_A), lambda t: (t, 0))],
        out_shape=[jax.ShapeDtypeStruct((N_TOK, Z_W), BF16),
                   jax.ShapeDtypeStruct((N_TOK, W_A), BF16)],
        scratch_shapes=[pltpu.VMEM((tm, D_MODEL), BF16)],
        compiler_params=pltpu.CompilerParams(dimension_semantics=("parallel",),
                                             vmem_limit_bytes=_vmem_limit(est)),
    )(x_lat, x_ctx, mods_l.reshape(MOD_ROWS, 1, 3 * D_MODEL), norm_g.reshape(1, D_MODEL), w_in_p,
      sgu_g.reshape(1, W_A), w_s_b, b_s_exp)


QK_W = 2 * HEAD_DIM
UQ_W = B_HEADS * (NOPE_DIM + 2 * ROPE_DIM)
UKV_W = B_HEADS * (NOPE_DIM + V_DIM)
VT_ROWS = V_DIM + BF16_SUBLANES
_Q_PRESCALE = MLA_SCALE * math.log2(math.e)
assert B_HEADS % 2 == 0 and 2 * ROPE_DIM == LANES


def _upproj_kernel(cq_ref, ckv_ref, kr_ref, cos_ref, sin_ref, qg_ref, kg_ref, wq_ref, wkv_ref,
                   q_ref, k_ref, vt_ref):
    cos2 = cos_ref[...]
    sin2 = sin_ref[...]
    n_rope = B_HEADS * ROPE_DIM
    cqn = _rms(cq_ref[...].astype(F32), qg_ref[...]).astype(BF16)
    q_all = jnp.dot(cqn, wq_ref[...], preferred_element_type=F32)
    rope0 = B_HEADS * NOPE_DIM
    for h in range(B_HEADS):
        q_ref[h, :, 0:NOPE_DIM] = (q_all[:, h * NOPE_DIM:(h + 1) * NOPE_DIM] * _Q_PRESCALE).astype(BF16)
    for j in range(B_HEADS // 2):
        a = q_all[:, rope0 + j * LANES: rope0 + (j + 1) * LANES]
        a_sw = q_all[:, rope0 + n_rope + j * LANES: rope0 + n_rope + (j + 1) * LANES]
        rot = ((a * cos2 + a_sw * sin2) * _Q_PRESCALE).astype(BF16)
        q_ref[2 * j, :, NOPE_DIM:QK_W] = rot
        q_ref[2 * j + 1, :, NOPE_DIM:QK_W] = rot

    ckvn = _rms(ckv_ref[...].astype(F32), kg_ref[...]).astype(BF16)
    kv_all = jnp.dot(ckvn, wkv_ref[...], preferred_element_type=F32)
    kr = kr_ref[...].astype(F32)
    krot = kr[:, 0:LANES] * cos2 + kr[:, LANES:2 * LANES] * sin2
    lane = lax.broadcasted_iota(jnp.int32, krot.shape, 1)
    k_lo = jnp.where(lane < ROPE_DIM, krot, 0.0).astype(BF16)
    k_hi = jnp.where(lane >= ROPE_DIM, krot, 0.0).astype(BF16)
    v0 = B_HEADS * NOPE_DIM
    sub = lax.broadcasted_iota(jnp.int32, (VT_ROWS - V_DIM, KV_CHUNK), 0)
    ones_rows = jnp.where(sub == 0, 1.0, 0.0).astype(BF16)
    for h in range(B_HEADS):
        k_ref[h, :, 0:NOPE_DIM] = kv_all[:, h * NOPE_DIM:(h + 1) * NOPE_DIM].astype(BF16)
        k_ref[h, :, NOPE_DIM:QK_W] = k_lo if h % 2 == 0 else k_hi
        v_h = kv_all[:, v0 + h * V_DIM: v0 + (h + 1) * V_DIM]
        for c in range(ROW_TILE // KV_CHUNK):
            vt_ref[h, c, 0:V_DIM, :] = v_h[c * KV_CHUNK:(c + 1) * KV_CHUNK, :].T.astype(BF16)
            vt_ref[h, c, V_DIM:VT_ROWS, :] = ones_rows


def _upproj(z, cos2, sin2, qa_g, kva_g, w_uq_p, w_ukv_p):
    tm = ROW_TILE
    est = 2 * (2 * tm * Q_LORA * 2 + tm * 4 * ROPE_DIM * 2 + 2 * tm * LANES * 4 + Q_LORA * UQ_W * 2
               + KV_LORA * UKV_W * 2 + B_HEADS * tm * (2 * QK_W + V_DIM) * 2) + 4 * tm * UQ_W * 4
    return pl.pallas_call(
        _upproj_kernel,
        grid=(ROW_TILES,),
        in_specs=[pl.BlockSpec((tm, Q_LORA), lambda t: (t, OFF_CQ // Q_LORA)),
                  pl.BlockSpec((tm, KV_LORA), lambda t: (t, OFF_CKV // KV_LORA)),
                  pl.BlockSpec((tm, 4 * ROPE_DIM), lambda t: (t, OFF_KR // (4 * ROPE_DIM))),
                  pl.BlockSpec((tm, LANES), lambda t: (_rope_tile(t), 0)),
                  pl.BlockSpec((tm, LANES), lambda t: (_rope_tile(t), 0)),
                  pl.BlockSpec((1, Q_LORA), lambda t: (0, 0)),
                  pl.BlockSpec((1, KV_LORA), lambda t: (0, 0)),
                  pl.BlockSpec((Q_LORA, UQ_W), lambda t: (0, 0)),
                  pl.BlockSpec((KV_LORA, UKV_W), lambda t: (0, 0))],
        out_specs=[pl.BlockSpec((B_HEADS, tm, QK_W), lambda t: (0, t, 0)),
                   pl.BlockSpec((B_HEADS, tm, QK_W), lambda t: (0, t, 0)),
                   pl.BlockSpec((B_HEADS, tm // KV_CHUNK, VT_ROWS, KV_CHUNK), lambda t: (0, t, 0, 0))],
        out_shape=[jax.ShapeDtypeStruct((B_HEADS, N_TOK, QK_W), BF16),
                   jax.ShapeDtypeStruct((B_HEADS, N_TOK, QK_W), BF16),
                   jax.ShapeDtypeStruct((B_HEADS, N_TOK // KV_CHUNK, VT_ROWS, KV_CHUNK), BF16)],
        compiler_params=pltpu.CompilerParams(dimension_semantics=("parallel",),
                                             vmem_limit_bytes=_vmem_limit(est)),
    )(z, z, z, cos2, sin2, qa_g.reshape(1, Q_LORA), kva_g.reshape(1, KV_LORA), w_uq_p, w_ukv_p)


MLA_TK = 512
MLA_TQ = 2048
MLA_TQG = MXU_DIM
assert SEQ % MLA_TK == 0 and MLA_TK % KV_CHUNK == 0 and SEQ % MLA_TQ == 0 and MLA_TQ % MLA_TQG == 0


def _attend_t(qs, chunks):
    def scores(k, q):
        return lax.dot_general(k, q, (((1,), (1,)), ((), ())), preferred_element_type=F32)

    state = [None] * len(qs)
    s_next = [scores(chunks[0][0], q) for q in qs]
    for j, (_, vts) in enumerate(chunks):
        for g, q in enumerate(qs):
            s = s_next[g]
            if j + 1 < len(chunks):
                s_next[g] = scores(chunks[j + 1][0], q)
            s_max = jnp.max(s, axis=0, keepdims=True)
            m_new = s_max if state[g] is None else jnp.maximum(state[g][0], s_max)
            pv = None
            for n, vt in enumerate(vts):
                p = jnp.exp2(s[n * KV_CHUNK:(n + 1) * KV_CHUNK] - m_new)
                d = jnp.dot(vt, p.astype(BF16), preferred_element_type=F32)
                pv = d if n == 0 else pv + d
            if state[g] is None:
                state[g] = (m_new, pv)
            else:
                m, acc = state[g]
                state[g] = (m_new, jnp.exp2(m - m_new) * acc + pv)
    return [acc[0:V_DIM] / acc[V_DIM:V_DIM + 1] for _, acc in state]


def _mla_lat_kernel(q_ref, kl_ref, vtl_ref, kc_ref, vtc_ref, gb_ref, o_ref):
    per = MLA_TK // KV_CHUNK
    chunks = [(kl_ref[0, j * MLA_TK:(j + 1) * MLA_TK, :], [vtl_ref[0, j * per + n] for n in range(per)])
              for j in range(SEQ // MLA_TK)]
    chunks.append((kc_ref[0], [vtc_ref[0, n] for n in range(CTX_LEN // KV_CHUNK)]))
    groups = [(g * MLA_TQG, (g + 1) * MLA_TQG) for g in range(MLA_TQ // MLA_TQG)]
    outs = _attend_t([q_ref[0, lo:hi, :] for lo, hi in groups], chunks)
    for (lo, hi), o_t in zip(groups, outs):
        o_ref[lo:hi, :] = (o_t.T * _silu(gb_ref[lo:hi, :].astype(F32))).astype(BF16)


def _mla_ctx_kernel(q_ref, kc_ref, vtc_ref, gb_ref, o_ref):
    (o_t,) = _attend_t([q_ref[0]], [(kc_ref[0], [vtc_ref[0, n] for n in range(CTX_LEN // KV_CHUNK)])])
    o = o_t.T
    o_ref[...] = (o * _silu(gb_ref[...].astype(F32))).astype(BF16)


def _mla(q, k, vt, z, with_ctx_queries):
    ctx_tile = N_LAT // CTX_LEN
    tq = MLA_TQ
    nq = SEQ // tq
    est = 2 * (tq * QK_W * 2 + SEQ * QK_W * 2 + SEQ * V_DIM * 2 + CTX_LEN * (QK_W + V_DIM) * 2
               + 2 * tq * HEAD_DIM * 2) + 8 * MLA_TK * tq * 4
    mb = pl.pallas_call(
        _mla_lat_kernel,
        grid=(BATCH, B_HEADS, nq),
        in_specs=[pl.BlockSpec((1, tq, QK_W), lambda b, h, i: (h, b * nq + i, 0)),
                  pl.BlockSpec((1, SEQ, QK_W), lambda b, h, i: (h, b, 0)),
                  pl.BlockSpec((1, SEQ // KV_CHUNK, VT_ROWS, KV_CHUNK), lambda b, h, i: (h, b, 0, 0)),
                  pl.BlockSpec((1, CTX_LEN, QK_W), lambda b, h, i: (h, ctx_tile + b, 0)),
                  pl.BlockSpec((1, CTX_LEN // KV_CHUNK, VT_ROWS, KV_CHUNK), lambda b, h, i: (h, ctx_tile + b, 0, 0)),
                  pl.BlockSpec((tq, HEAD_DIM), lambda b, h, i: (b * nq + i, OFF_GB // HEAD_DIM + h))],
        out_specs=pl.BlockSpec((tq, HEAD_DIM), lambda b, h, i: (b * nq + i, h)),
        out_shape=jax.ShapeDtypeStruct((N_LAT, W_B), BF16),
        compiler_params=pltpu.CompilerParams(dimension_semantics=("parallel", "parallel", "parallel"),
                                             vmem_limit_bytes=_vmem_limit(est)),
    )(q, k, vt, k, vt, z)
    if not with_ctx_queries:
        return mb, None
    mb_ctx = pl.pallas_call(
        _mla_ctx_kernel,
        grid=(BATCH, B_HEADS),
        in_specs=[pl.BlockSpec((1, CTX_LEN, QK_W), lambda b, h: (h, ctx_tile + b, 0)),
                  pl.BlockSpec((1, CTX_LEN, QK_W), lambda b, h: (h, ctx_tile + b, 0)),
                  pl.BlockSpec((1, CTX_LEN // KV_CHUNK, VT_ROWS, KV_CHUNK), lambda b, h: (h, ctx_tile + b, 0, 0)),
                  pl.BlockSpec((CTX_LEN, HEAD_DIM), lambda b, h: (ctx_tile + b, OFF_GB // HEAD_DIM + h))],
        out_specs=pl.BlockSpec((CTX_LEN, HEAD_DIM), lambda b, h: (b, h)),
        out_shape=jax.ShapeDtypeStruct((N_CTX, W_B), BF16),
        compiler_params=pltpu.CompilerParams(dimension_semantics=("parallel", "parallel")),
    )(q, k, vt, z)
    return mb, mb_ctx


NA_QROWS = Q_TILE // GRID_W
NA_KROWS = NA_QROWS + MAX_KH
NA_KEYS = NA_KROWS * GRID_W
assert ROWS >= NA_KROWS and NA_KEYS % MXU_DIM == 0 and NA_KROWS % 2 == 0 and 2 * GRID_W == LANES


def _na_strip_row(j):
    return int(np.clip(j * NA_QROWS - MAX_KH // 2, 0, ROWS - NA_KROWS))


def _na_tables():
    def one(j):
        s = _na_strip_row(j)
        valid = np.zeros((NA_QROWS, NA_KROWS), bool)
        d = np.zeros((NA_QROWS, NA_KROWS), np.int32)
        for a in range(NA_QROWS):
            r = j * NA_QROWS + a
            r0 = int(np.clip(r - MAX_KH // 2, 0, ROWS - MAX_KH))
            for i in range(NA_KROWS):
                valid[a, i] = 0 <= s + i - r0 < MAX_KH
                d[a, i] = s + i - r + (MAX_KH - 1) if valid[a, i] else 0
        return valid, d
    n_tiles = ROWS // NA_QROWS
    first, mid, last = one(0), one(1), one(n_tiles - 1)
    for j in range(1, n_tiles - 1):
        v, d = one(j)
        assert (v == mid[0]).all() and (d == mid[1]).all()
    return np.stack([first[0], mid[0], last[0]]), np.stack([first[1], mid[1], last[1]])


_NA_VALID, _NA_DROW = _na_tables()


def _na_bias_tables(rpb_l):
    col = np.arange(GRID_W)
    c0 = np.clip(col - KW // 2, 0, GRID_W - KW)
    col_ok = (col[None, :] >= c0[:, None]) & (col[None, :] < c0[:, None] + KW)
    dc = np.clip(col[None, :] - col[:, None], -(KW - 1), KW - 1) + (KW - 1)
    onehot = (dc.reshape(-1)[None, :] == np.arange(2 * KW - 1)[:, None]).astype(np.float32)
    t1 = jnp.dot(rpb_l.reshape(C_HEADS * (2 * MAX_KH - 1), 2 * KW - 1), jnp.asarray(onehot),
                 precision=lax.Precision.HIGHEST).reshape(C_HEADS, 2 * MAX_KH - 1, GRID_W, GRID_W)
    t1 = jnp.where(jnp.asarray(col_ok)[None, None], t1, NEG) / C_SCALE
    return jnp.concatenate([t1, t1], axis=-1)


_NA_EXP2_SCALE = C_SCALE * math.log2(math.e)


def _softmax_rows(blocks):
    m = functools.reduce(jnp.maximum, [jnp.max(s, axis=-1, keepdims=True) for s in blocks])
    ps = [jnp.exp2((s - m) * _NA_EXP2_SCALE) for s in blocks]
    return ps, functools.reduce(jnp.add, [jnp.sum(p, axis=-1, keepdims=True) for p in ps])


def _assemble_bias(t_ref, bias_ref):
    lane = lax.broadcasted_iota(jnp.int32, (GRID_W, 2 * GRID_W), 1)
    neg = jnp.full((GRID_W, 2 * GRID_W), NEG / C_SCALE, F32)
    for cls in range(3):
        for a in range(NA_QROWS):
            for pair in range(NA_KROWS // 2):
                halves = [t_ref[0, int(_NA_DROW[cls, a, i])] if _NA_VALID[cls, a, i] else neg
                          for i in (2 * pair, 2 * pair + 1)]
                tile = halves[0] if halves[0] is halves[1] else jnp.where(lane < GRID_W, halves[0], halves[1])
                bias_ref[cls, a * GRID_W:(a + 1) * GRID_W, pair * 2 * GRID_W:(pair + 1) * 2 * GRID_W] = tile


def _natten_kernel(q_ref, kl_ref, vl_ref, kc_ref, vc_ref, t_ref, g_ref, o_ref, bias_ref):
    nt = (((1,), (1,)), ((), ()))
    n_tiles = ROWS // NA_QROWS
    _assemble_bias(t_ref, bias_ref)

    def tile(j, carry):
        rows = pl.ds(pl.multiple_of(j * Q_TILE, Q_TILE), Q_TILE)
        strip_row = jnp.clip(j * NA_QROWS - MAX_KH // 2, 0, ROWS - NA_KROWS)
        strip = pl.ds(pl.multiple_of(strip_row * GRID_W, GRID_W), NA_KEYS)
        cls = jnp.where(j == 0, 0, jnp.where(j == n_tiles - 1, 2, 1))
        q = q_ref[rows, :]
        s_nb = lax.dot_general(q, kl_ref[strip, :], nt, preferred_element_type=F32) + bias_ref[cls]
        s_cx = lax.dot_general(q, kc_ref[...], nt, preferred_element_type=F32)
        (p_nb, p_cx), l = _softmax_rows([s_nb, s_cx])
        o = jnp.dot(p_nb.astype(BF16), vl_ref[strip, :], preferred_element_type=F32)
        o += jnp.dot(p_cx.astype(BF16), vc_ref[...], preferred_element_type=F32)
        o_ref[rows, :] = (o / l * _silu(g_ref[rows, :].astype(F32))).astype(BF16)
        return carry

    lax.fori_loop(0, n_tiles, tile, 0, unroll=4)


def _natten_ctx_kernel(q_ref, kc_ref, vc_ref, g_ref, o_ref):
    s =lax.dot_general(q_ref[...], kc_ref[...], (((1,), (1,)), ((), ())), preferred_element_type=F32)
    (p,), l = _softmax_rows([s])
    o = jnp.dot(p.astype(BF16), vc_ref[...], preferred_element_type=F32)
    o_ref[...] = (o / l * _silu(g_ref[...].astype(F32))).astype(BF16)


def _natten(z, bias_tab, with_ctx_queries):
    ctx_tile = N_LAT // CTX_LEN
    hd = HEAD_DIM
    est = (2 * (5 * SEQ * hd * 2 + 2 * CTX_LEN * hd * 2 + 3 * Q_TILE * NA_KEYS * 4)
           + 12 * Q_TILE * (NA_KEYS + CTX_LEN) * 4)
    mc = pl.pallas_call(
        _natten_kernel,
        grid=(BATCH, C_HEADS),
        in_specs=[pl.BlockSpec((SEQ, hd), lambda b, h: (b, OFF_QC // hd + h)),
                  pl.BlockSpec((SEQ, hd), lambda b, h: (b, OFF_KC // hd + h)),
                  pl.BlockSpec((SEQ, hd), lambda b, h: (b, OFF_VC // hd + h)),
                  pl.BlockSpec((CTX_LEN, hd), lambda b, h: (ctx_tile + b, OFF_KC // hd + h)),
                  pl.BlockSpec((CTX_LEN, hd), lambda b, h: (ctx_tile + b, OFF_VC // hd + h)),
                  pl.BlockSpec((1, 2 * MAX_KH - 1, GRID_W, 2 * GRID_W), lambda b, h: (h, 0, 0, 0)),
                  pl.BlockSpec((SEQ, hd), lambda b, h: (b, OFF_GC // hd + h))],
        out_specs=pl.BlockSpec((SEQ, hd), lambda b, h: (b, h)),
        out_shape=jax.ShapeDtypeStruct((N_LAT, W_C), BF16),
        scratch_shapes=[pltpu.VMEM((3, Q_TILE, NA_KEYS), F32)],
        compiler_params=pltpu.CompilerParams(dimension_semantics=("parallel", "parallel"),
                                             vmem_limit_bytes=_vmem_limit(est)),
    )(z, z, z, z, z, bias_tab, z)
    if not with_ctx_queries:
        return mc, None
    mc_ctx = pl.pallas_call(
        _natten_ctx_kernel,
        grid=(BATCH, C_HEADS),
        in_specs=[pl.BlockSpec((CTX_LEN, hd), lambda b, h: (ctx_tile + b, OFF_QC // hd + h)),
                  pl.BlockSpec((CTX_LEN, hd), lambda b, h: (ctx_tile + b, OFF_KC // hd + h)),
                  pl.BlockSpec((CTX_LEN, hd), lambda b, h: (ctx_tile + b, OFF_VC // hd + h)),
                  pl.BlockSpec((CTX_LEN, hd), lambda b, h: (ctx_tile + b, OFF_GC // hd + h))],
        out_specs=pl.BlockSpec((CTX_LEN, hd), lambda b, h: (b, h)),
        out_shape=jax.ShapeDtypeStruct((N_CTX, W_C), BF16),
        compiler_params=pltpu.CompilerParams(dimension_semantics=("parallel", "parallel")),
    )(z, z, z, z)
    return mc, mc_ctx


OUT_CHUNK = 512


def _outproj_body(ma, mb, mc, x, w_ref, mod_ref, o_ref):
    gate = mod_ref[0, :, 2 * D_MODEL:3 * D_MODEL]
    for c0 in range(0, D_MODEL, OUT_CHUNK):
        c1 = c0 + OUT_CHUNK
        y = jnp.dot(ma, w_ref[0:W_A, c0:c1], preferred_element_type=F32)
        y += jnp.dot(mb, w_ref[W_A:W_A + W_B, c0:c1], preferred_element_type=F32)
        y += jnp.dot(mc, w_ref[W_A + W_B:D_MODEL, c0:c1], preferred_element_type=F32)
        o_ref[:, c0:c1] = x[:, c0:c1] + gate[:, c0:c1] * y


def _cast_weight_once(w_ref, wb_ref):
    @pl.when(pl.program_id(0) == 0)
    def _():
        def rows(i, carry):
            r = pl.ds(pl.multiple_of(i * W_PREP_ROWS, W_PREP_ROWS), W_PREP_ROWS)
            wb_ref[r, :] = w_ref[r, :].astype(BF16)
            return carry
        lax.fori_loop(0, MIX_W // W_PREP_ROWS, rows, 0)


def _outproj_mid_kernel(ma_ref, mbl_ref, mbc_ref, mcl_ref, mcc_ref, xl_ref, xc_ref, w_ref, mod_ref, o_ref, wb_ref):
    _cast_weight_once(w_ref, wb_ref)
    _outproj_body(ma_ref[...], _pick_rows(mbl_ref, mbc_ref), _pick_rows(mcl_ref, mcc_ref),
                  _pick_rows(xl_ref, xc_ref), wb_ref, mod_ref, o_ref)


def _outproj_final_kernel(ma_ref, mb_ref, mc_ref, x_ref, w_ref, mod_ref, fg_ref, o_ref, wb_ref):
    _cast_weight_once(w_ref, wb_ref)
    _outproj_body(ma_ref[...], mb_ref[...], mc_ref[...], x_ref[...], wb_ref, mod_ref, o_ref)
    o_ref[...] = _rms(o_ref[...], fg_ref[...])


def _outproj_mid(ma, mb, mb_ctx, mc, mc_ctx, x_lat, x_ctx, ctx_block, w_out, l, mods_l):
    tm = ROW_TILE
    est = (3 * tm * D_MODEL * 2 + D_MODEL * D_MODEL * 6 + 5 * tm * D_MODEL * 4 + 3 * tm * OUT_CHUNK * 4)
    lat = lambda w: pl.BlockSpec((tm, w), lambda t: (_lat_tile(t), 0))
    ctx = lambda w, blk: pl.BlockSpec((tm, w), lambda t: (blk, 0), pipeline_mode=pl.Buffered(1))
    return pl.pallas_call(
        _outproj_mid_kernel,
        grid=(ROW_TILES,),
        in_specs=[pl.BlockSpec((tm, W_A), lambda t: (t, 0)),
                  lat(W_B), ctx(W_B, 0), lat(W_C), ctx(W_C, 0), lat(D_MODEL), ctx(D_MODEL, ctx_block),
                  pl.BlockSpec((MIX_W, D_MODEL), lambda t: (l, 0), pipeline_mode=pl.Buffered(1)),
                  pl.BlockSpec((1, 1, 3 * D_MODEL), lambda t: (_mod_row(t), 0, 0))],
        out_specs=pl.BlockSpec((tm, D_MODEL), lambda t: (t, 0)),
        out_shape=jax.ShapeDtypeStruct((N_TOK, D_MODEL), F32),
        scratch_shapes=[pltpu.VMEM((MIX_W, D_MODEL), BF16)],
        compiler_params=pltpu.CompilerParams(dimension_semantics=("arbitrary",),
                                             vmem_limit_bytes=_vmem_limit(est)),
    )(ma, mb, mb_ctx, mc, mc_ctx, x_lat, x_ctx, w_out.reshape(DEPTH * MIX_W, D_MODEL),
      mods_l.reshape(MOD_ROWS, 1, 3 * D_MODEL))


def _outproj_final(ma, mb, mc, x_lat, w_out, l, mods_l, final_g):
    tm = ROW_TILE
    est = (2 * tm * D_MODEL * 2 + D_MODEL * D_MODEL * 6 + 4 * tm * D_MODEL * 4 + 3 * tm * OUT_CHUNK * 4)
    return pl.pallas_call(
        _outproj_final_kernel,
        grid=(LAT_TILES,),
        in_specs=[pl.BlockSpec((tm, W_A), lambda t: (t, 0)),
                  pl.BlockSpec((tm, W_B), lambda t: (t, 0)),
                  pl.BlockSpec((tm, W_C), lambda t: (t, 0)),
                  pl.BlockSpec((tm, D_MODEL), lambda t: (t, 0)),
                  pl.BlockSpec((MIX_W, D_MODEL), lambda t: (l, 0), pipeline_mode=pl.Buffered(1)),
                  pl.BlockSpec((1, 1, 3 * D_MODEL), lambda t: (_mod_row(t), 0, 0)),
                  pl.BlockSpec((1, D_MODEL), lambda t: (0, 0))],
        out_specs=pl.BlockSpec((tm, D_MODEL), lambda t: (t, 0)),
        out_shape=jax.ShapeDtypeStruct((N_LAT, D_MODEL), F32),
        scratch_shapes=[pltpu.VMEM((MIX_W, D_MODEL), BF16)],
        compiler_params=pltpu.CompilerParams(dimension_semantics=("arbitrary",),
                                             vmem_limit_bytes=_vmem_limit(est)),
    )(ma, mb, mc, x_lat, w_out.reshape(DEPTH * MIX_W, D_MODEL), mods_l.reshape(MOD_ROWS, 1, 3 * D_MODEL),
      final_g.reshape(1, D_MODEL))


W_PREP_ROWS = MXU_DIM


def _prep_w_in_kernel(w_ref, kr_ref, o_ref):
    h0, h1 = _SRC["u"][0], _SRC["ckv"][1]
    t0, t1 = _SRC["gb"][0], _SRC["gc"][1]
    o_ref[0:h1 - h0, :] = w_ref[h0:h1, :].astype(BF16)
    o_ref[GM_W + OFF_GB:GM_W + OFF_KR, :] = w_ref[t0:t1, :].astype(BF16)
    o_ref[GM_W + OFF_KR:W_ROWS, :] = kr_ref[...]


def _prep_w_in(w_in, l):
    assert _SRC["ckv"][1] == GM_W + OFF_GB and OFF_GB + (_SRC["gc"][1] - _SRC["gb"][0]) == OFF_KR
    w_t = jnp.swapaxes(w_in, 1, 2)
    kr = w_t[l, _SRC["kr"][0]:_SRC["kr"][1], :]
    kr_sw = jnp.concatenate([kr[_Q4:2 * _Q4], kr[0:_Q4], kr[3 * _Q4:4 * _Q4], kr[2 * _Q4:3 * _Q4]], axis=0)
    kr4 = jnp.concatenate([kr, kr, kr_sw, kr_sw], axis=0).astype(BF16)
    cols = W_PREP_ROWS
    est = 2 * cols * (IN_W * 4 + W_ROWS * 2 + 4 * ROPE_DIM * 2) + 2 * cols * IN_W * 4
    return pl.pallas_call(
        _prep_w_in_kernel,
        grid=(D_MODEL // cols,),
        in_specs=[pl.BlockSpec((IN_W, cols), lambda i: (l, i)),
                  pl.BlockSpec((4 * ROPE_DIM, cols), lambda i: (0, i))],
        out_specs=pl.BlockSpec((W_ROWS, cols), lambda i: (0, i)),
        out_shape=jax.ShapeDtypeStruct((W_ROWS, D_MODEL), BF16),
        compiler_params=pltpu.CompilerParams(dimension_semantics=("parallel",),
                                             vmem_limit_bytes=_vmem_limit(est)),
    )(w_t.reshape(DEPTH * IN_W, D_MODEL), kr4)


def _prep_w_uq(w):
    w3 = w.reshape(Q_LORA, B_HEADS, NOPE_DIM + ROPE_DIM)
    nope = w3[:, :, :NOPE_DIM].reshape(Q_LORA, B_HEADS * NOPE_DIM)
    rope = w3[:, :, NOPE_DIM:]
    rope_sw = jnp.concatenate([rope[..., _Q4:2 * _Q4], rope[..., 0:_Q4],
                               rope[..., 3 * _Q4:4 * _Q4], rope[..., 2 * _Q4:3 * _Q4]], axis=-1)
    return jnp.concatenate([nope, rope.reshape(Q_LORA, -1), rope_sw.reshape(Q_LORA, -1)], axis=1).astype(BF16)


def _prep_w_ukv(w):
    w3 = w.reshape(KV_LORA, B_HEADS, NOPE_DIM + V_DIM)
    return jnp.concatenate([w3[:, :, :NOPE_DIM].reshape(KV_LORA, -1),
                            w3[:, :, NOPE_DIM:].reshape(KV_LORA, -1)], axis=1).astype(BF16)


def _rope_tables():
    t = np.arange(SEQ)
    row = (t // GRID_W).astype(np.float32)
    col = (t % GRID_W).astype(np.float32)
    inv = (np.float32(ROPE_THETA) ** (-np.arange(_Q4, dtype=np.float32) / np.float32(_Q4))).astype(np.float32)
    ar, ac = row[:, None] * inv, col[:, None] * inv
    cos = np.concatenate([np.cos(ar), np.cos(ar), np.cos(ac), np.cos(ac)], axis=1)
    sin = np.concatenate([-np.sin(ar), np.sin(ar), -np.sin(ac), np.sin(ac)], axis=1)
    cos = np.concatenate([cos, np.ones((ROW_TILE, ROPE_DIM))], axis=0).astype(np.float32)
    sin = np.concatenate([sin, np.zeros((ROW_TILE, ROPE_DIM))], axis=0).astype(np.float32)
    return np.concatenate([cos, cos], axis=1), np.concatenate([sin, sin], axis=1)


_COS2, _SIN2 = _rope_tables()


def _rope_tile(t):
    return jnp.where(t < LAT_TILES, t % TILES_PER_SAMPLE, TILES_PER_SAMPLE)


def kernel(x, c, ctx, c_ctx, w_ada, b_ada, norm_g, w_in, qa_g, kva_g, w_uq, w_ukv, sgu_g, w_s, b_s, rpb,
           w_out, final_g):
    assert x.shape == (BATCH, SEQ, D_MODEL) and ctx.shape == (BATCH, CTX_LEN, D_MODEL)
    assert w_in.shape == (DEPTH, D_MODEL, IN_W)
    cc = jnp.concatenate([c, c_ctx[None, :], jnp.zeros((MOD_ROWS - BATCH - 1, D_MODEL), F32)], axis=0)
    mods = _modulation(cc, w_ada, b_ada)
    cos2, sin2 = jnp.asarray(_COS2), jnp.asarray(_SIN2)

    x_lat, x_ctx, ctx_block = x.reshape(N_LAT, D_MODEL), ctx.reshape(N_CTX, D_MODEL), 0
    for l in range(DEPTH):
        last = l == DEPTH - 1
        b_s_exp = jnp.repeat(b_s[l].T, HEAD_DIM, axis=1)
        z, ma = _inproj(x_lat, x_ctx, ctx_block, mods[l], norm_g[l], _prep_w_in(w_in, l),
                        sgu_g[l], w_s[l].astype(BF16), b_s_exp)
        q, k, vt = _upproj(z, cos2, sin2, qa_g[l], kva_g[l], _prep_w_uq(w_uq[l]), _prep_w_ukv(w_ukv[l]))
        mb, mb_ctx = _mla(q, k, vt, z, with_ctx_queries=not last)
        mc, mc_ctx = _natten(z, _na_bias_tables(rpb[l]), with_ctx_queries=not last)
        if last:
            out = _outproj_final(ma, mb, mc, x_lat, w_out, l, mods[l], final_g)
            return out.reshape(BATCH, SEQ, D_MODEL)
        xf = _outproj_mid(ma, mb, mb_ctx, mc, mc_ctx, x_lat, x_ctx, ctx_block, w_out, l, mods[l])
        x_lat, x_ctx, ctx_block = xf, xf, LAT_TILES
```

```python
import functools
import math

import numpy as np
import jax
import jax.numpy as jnp
from jax import lax
from jax.experimental import pallas as pl
from jax.experimental.pallas import tpu as pltpu

D_MODEL = 2048
BATCH = 2
SEQ = 4096
DEPTH = 2
GRID_W = 64
CTX_LEN = 256
EPS = 1e-6
NEG = -1e30
HEAD_DIM = 128
W_A = D_MODEL // 4
W_B = D_MODEL // 2
W_C = D_MODEL // 4
MIX_W = W_A + W_B + W_C
CHUNK = 128
A_GROUPS = W_A // HEAD_DIM
B_HEADS = W_B // HEAD_DIM
Q_LORA = D_MODEL // 4
KV_LORA = 512
NOPE_DIM = 128
ROPE_DIM = 64
V_DIM = 128
MLA_SCALE = (NOPE_DIM + ROPE_DIM) ** -0.5
ROPE_THETA = 10000.0
C_HEADS = W_C // HEAD_DIM
MAX_KH = 8
KW = 16
C_SCALE = HEAD_DIM ** -0.5
ROWS = SEQ // GRID_W

LANES = 128
SUBLANES = 8
BF16_SUBLANES = 16
MXU_DIM = 256
VMEM_BYTES_V7X = 64 * 1024 * 1024
VMEM_LIMIT_CAP = VMEM_BYTES_V7X * 7 // 8

F32 = jnp.float32
BF16 = jnp.bfloat16

N_LAT = BATCH * SEQ
N_CTX = BATCH * CTX_LEN
N_TOK = N_LAT + N_CTX
ROW_TILE = N_CTX
LAT_TILES = N_LAT // ROW_TILE
ROW_TILES = N_TOK // ROW_TILE
TILES_PER_SAMPLE = SEQ // ROW_TILE
Q_TILE = CTX_LEN
KV_CHUNK = MXU_DIM
assert ROW_TILE % CHUNK == 0 and SEQ % ROW_TILE == 0 and SEQ % Q_TILE == 0 and CTX_LEN % KV_CHUNK == 0

GM_W = 3 * W_A
OFF_CQ = 0
OFF_CKV = OFF_CQ + Q_LORA
OFF_GB = OFF_CKV + KV_LORA
OFF_QC = OFF_GB + W_B
OFF_KC = OFF_QC + W_C
OFF_VC = OFF_KC + W_C
OFF_GC = OFF_VC + W_C
OFF_KR = OFF_GC + W_C
Z_W = OFF_KR + 4 * ROPE_DIM
W_ROWS = GM_W + Z_W
Z_CHUNK = 512
assert W_ROWS % MXU_DIM == 0 and OFF_KR % (4 * ROPE_DIM) == 0

_SRC = {}
_acc = 0
for _name, _w in (("u", W_A), ("va", W_A), ("ga", W_A), ("cq", Q_LORA), ("ckv", KV_LORA), ("kr", ROPE_DIM),
                  ("gb", W_B), ("qc", W_C), ("kc", W_C), ("vc", W_C), ("gc", W_C)):
    _SRC[_name] = (_acc, _acc + _w)
    _acc += _w
IN_W = _acc

_Q4 = ROPE_DIM // 4


def _vmem_limit(nbytes):
    return int(min(VMEM_LIMIT_CAP, max(16 * 1024 * 1024, nbytes * 5 // 4)))


def _silu(x):
    return x * jax.nn.sigmoid(x)


def _rms(x, g):
    return x * lax.rsqrt(jnp.mean(x * x, axis=-1, keepdims=True) + EPS) * g


MOD_ROWS = SUBLANES
MOD_TN = 768
assert BATCH + 1 <= MOD_ROWS and (3 * D_MODEL) % MOD_TN == 0 and MOD_TN % LANES == 0


def _mod_kernel(c_ref, w_ref, b_ref, o_ref):
    a = _silu(c_ref[...])
    a_hi = a.astype(BF16)
    a_lo = (a - a_hi.astype(F32)).astype(BF16)
    w = w_ref[0]
    w_hi = w.astype(BF16)
    w_lo = (w - w_hi.astype(F32)).astype(BF16)
    r = jnp.dot(jnp.concatenate([a_hi, a_lo], axis=0), w_hi, preferred_element_type=F32)
    r_lo = jnp.dot(a_hi, w_lo, preferred_element_type=F32)
    o_ref[0] = r[0:MOD_ROWS] + r[MOD_ROWS:2 * MOD_ROWS] + r_lo + b_ref[0]


def _modulation(cc, w_ada, b_ada):
    n = 3 * D_MODEL
    est = 2 * (MOD_ROWS * D_MODEL * 4 + D_MODEL * MOD_TN * 4 + 2 * MOD_ROWS * MOD_TN * 4)
    return pl.pallas_call(
        _mod_kernel,
        grid=(DEPTH, n // MOD_TN),
        in_specs=[pl.BlockSpec((MOD_ROWS, D_MODEL), lambda l, j: (0, 0)),
                  pl.BlockSpec((1, D_MODEL, MOD_TN), lambda l, j: (l, 0, j)),
                  pl.BlockSpec((1, 1, MOD_TN), lambda l, j: (l, 0, j))],
        out_specs=pl.BlockSpec((1, MOD_ROWS, MOD_TN), lambda l, j: (l, 0, j)),
        out_shape=jax.ShapeDtypeStruct((DEPTH, MOD_ROWS, n), F32),
        compiler_params=pltpu.CompilerParams(dimension_semantics=("parallel", "parallel"),
                                             vmem_limit_bytes=_vmem_limit(est)),
    )(cc, w_ada, b_ada.reshape(DEPTH, 1, n))


def _mod_row(t):
    return jnp.where(t < LAT_TILES, t // TILES_PER_SAMPLE, BATCH)


def _lat_tile(t):
    return jnp.minimum(t, LAT_TILES - 1)


def _pick_rows(lat_ref, ctx_ref):
    return jnp.where(pl.program_id(0) < LAT_TILES, lat_ref[...], ctx_ref[...])


def _inproj_kernel(xl_ref, xc_ref, mod_ref, g_ref, w_ref, sg_ref, ws_ref, bs_ref, z_ref, ma_ref, h_ref):
    nt = (((1,), (1,)), ((), ()))
    x = _pick_rows(xl_ref, xc_ref)
    shift = mod_ref[0, :, 0:D_MODEL]
    scale = mod_ref[0, :, D_MODEL:2 * D_MODEL]
    h_ref[...] = (_rms(x, g_ref[...]) * (1.0 + scale) + shift).astype(BF16)

    def proj(r0, r1):
        return lax.dot_general(h_ref[...], w_ref[r0:r1, :], nt, preferred_element_type=F32)

    vn = _rms(jax.nn.gelu(proj(W_A, 2 * W_A)), sg_ref[...]).astype(BF16)
    front = jax.nn.gelu(proj(0, W_A)) * _silu(proj(2 * W_A, 3 * W_A))
    for c in range(ROW_TILE // CHUNK):
        r0, r1 = c * CHUNK, (c + 1) * CHUNK
        for g in range(A_GROUPS):
            c0, c1 = g * HEAD_DIM, (g + 1) * HEAD_DIM
            s = jnp.dot(ws_ref[g], vn[r0:r1, c0:c1], preferred_element_type=F32) + bs_ref[:, c0:c1]
            ma_ref[r0:r1, c0:c1] = (front[r0:r1, c0:c1] * s).astype(BF16)

    for c0 in range(0, Z_W, Z_CHUNK):
        c1 = min(c0 + Z_CHUNK, Z_W)
        z_ref[:, c0:c1] = proj(GM_W + c0, GM_W + c1).astype(BF16)


def _inproj(x_lat, x_ctx, ctx_block, mods_l, norm_g, w_in_p, sgu_g, w_s_b, b_s_exp):
    tm = ROW_TILE
    est = (3 * tm * D_MODEL * 4 + D_MODEL * W_ROWS * 2 + 2 * tm * (Z_W + W_A) * 2 + tm * D_MODEL * 2
           + 6 * tm * Z_CHUNK * 4)
    return pl.pallas_call(
        _inproj_kernel,
        grid=(ROW_TILES,),
        in_specs=[pl.BlockSpec((tm, D_MODEL), lambda t: (_lat_tile(t), 0)),
                  pl.BlockSpec((tm, D_MODEL), lambda t: (ctx_block, 0), pipeline_mode=pl.Buffered(1)),
                  pl.BlockSpec((1, 1, 3 * D_MODEL), lambda t: (_mod_row(t), 0, 0)),
                  pl.BlockSpec((1, D_MODEL), lambda t: (0, 0)),
                  pl.BlockSpec((W_ROWS, D_MODEL), lambda t: (0, 0), pipeline_mode=pl.Buffered(1)),
                  pl.BlockSpec((1, W_A), lambda t: (0, 0)),
                  pl.BlockSpec((A_GROUPS, CHUNK, CHUNK), lambda t: (0, 0, 0)),
                  pl.BlockSpec((CHUNK, W_A), lambda t: (0, 0))],
        out_specs=[pl.BlockSpec((tm, Z_W), lambda t: (t, 0)),
                   pl.BlockSpec((tm, W_A), lambda t: (t, 0))],
        out_shape=[jax.ShapeDtypeStruct((N_TOK, Z_W), BF16),
                   jax.ShapeDtypeStruct((N_TOK, W_A), BF16)],
        scratch_shapes=[pltpu.VMEM((tm, D_MODEL), BF16)],
        compiler_params=pltpu.CompilerParams(dimension_semantics=("parallel",),
                                             vmem_limit_bytes=_vmem_limit(est)),
    )(x_lat, x_ctx, mods_l.reshape(MOD_ROWS, 1, 3 * D_MODEL), norm_g.reshape(1, D_MODEL), w_in_p,
      sgu_g.reshape(1, W_A), w_s_b, b_s_exp)


QK_W = 2 * HEAD_DIM
UQ_W = B_HEADS * (NOPE_DIM + 2 * ROPE_DIM)
UKV_W = B_HEADS * (NOPE_DIM + V_DIM)
VT_ROWS = V_DIM + BF16_SUBLANES
_Q_PRESCALE = MLA_SCALE * math.log2(math.e)
assert B_HEADS % 2 == 0 and 2 * ROPE_DIM == LANES


def _upproj_kernel(cq_ref, ckv_ref, kr_ref, cos_ref, sin_ref, qg_ref, kg_ref, wq_ref, wkv_ref,
                   q_ref, k_ref, vt_ref):
    cos2 = cos_ref[...]
    sin2 = sin_ref[...]
    n_rope = B_HEADS * ROPE_DIM
    cqn = _rms(cq_ref[...].astype(F32), qg_ref[...]).astype(BF16)
    q_all = jnp.dot(cqn, wq_ref[...], preferred_element_type=F32)
    rope0 = B_HEADS * NOPE_DIM
    for h in range(B_HEADS):
        q_ref[h, :, 0:NOPE_DIM] = (q_all[:, h * NOPE_DIM:(h + 1) * NOPE_DIM] * _Q_PRESCALE).astype(BF16)
    for j in range(B_HEADS // 2):
        a = q_all[:, rope0 + j * LANES: rope0 + (j + 1) * LANES]
        a_sw = q_all[:, rope0 + n_rope + j * LANES: rope0 + n_rope + (j + 1) * LANES]
        rot = ((a * cos2 + a_sw * sin2) * _Q_PRESCALE).astype(BF16)
        q_ref[2 * j, :, NOPE_DIM:QK_W] = rot
        q_ref[2 * j + 1, :, NOPE_DIM:QK_W] = rot

    ckvn = _rms(ckv_ref[...].astype(F32), kg_ref[...]).astype(BF16)
    kv_all = jnp.dot(ckvn, wkv_ref[...], preferred_element_type=F32)
    kr = kr_ref[...].astype(F32)
    krot = kr[:, 0:LANES] * cos2 + kr[:, LANES:2 * LANES] * sin2
    lane = lax.broadcasted_iota(jnp.int32, krot.shape, 1)
    k_lo = jnp.where(lane < ROPE_DIM, krot, 0.0).astype(BF16)
    k_hi = jnp.where(lane >= ROPE_DIM, krot, 0.0).astype(BF16)
    v0 = B_HEADS * NOPE_DIM
    sub = lax.broadcasted_iota(jnp.int32, (VT_ROWS - V_DIM, KV_CHUNK), 0)
    ones_rows = jnp.where(sub == 0, 1.0, 0.0).astype(BF16)
    for h in range(B_HEADS):
        k_ref[h, :, 0:NOPE_DIM] = kv_all[:, h * NOPE_DIM:(h + 1) * NOPE_DIM].astype(BF16)
        k_ref[h, :, NOPE_DIM:QK_W] = k_lo if h % 2 == 0 else k_hi
        v_h = kv_all[:, v0 + h * V_DIM: v0 + (h + 1) * V_DIM]
        for c in range(ROW_TILE // KV_CHUNK):
            vt_ref[h, c, 0:V_DIM, :] = v_h[c * KV_CHUNK:(c + 1) * KV_CHUNK, :].T.astype(BF16)
            vt_ref[h, c, V_DIM:VT_ROWS, :] = ones_rows


def _upproj(z, cos2, sin2, qa_g, kva_g, w_uq_p, w_ukv_p):
    tm = ROW_TILE
    est = 2 * (2 * tm * Q_LORA * 2 + tm * 4 * ROPE_DIM * 2 + 2 * tm * LANES * 4 + Q_LORA * UQ_W * 2
               + KV_LORA * UKV_W * 2 + B_HEADS * tm * (2 * QK_W + V_DIM) * 2) + 4 * tm * UQ_W * 4
    return pl.pallas_call(
        _upproj_kernel,
        grid=(ROW_TILES,),
        in_specs=[pl.BlockSpec((tm, Q_LORA), lambda t: (t, OFF_CQ // Q_LORA)),
                  pl.BlockSpec((tm, KV_LORA), lambda t: (t, OFF_CKV // KV_LORA)),
                  pl.BlockSpec((tm, 4 * ROPE_DIM), lambda t: (t, OFF_KR // (4 * ROPE_DIM))),
                  pl.BlockSpec((tm, LANES), lambda t: (_rope_tile(t), 0)),
                  pl.BlockSpec((tm, LANES), lambda t: (_rope_tile(t), 0)),
                  pl.BlockSpec((1, Q_LORA), lambda t: (0, 0)),
                  pl.BlockSpec((1, KV_LORA), lambda t: (0, 0)),
                  pl.BlockSpec((Q_LORA, UQ_W), lambda t: (0, 0)),
                  pl.BlockSpec((KV_LORA, UKV_W), lambda t: (0, 0))],
        out_specs=[pl.BlockSpec((B_HEADS, tm, QK_W), lambda t: (0, t, 0)),
                   pl.BlockSpec((B_HEADS, tm, QK_W), lambda t: (0, t, 0)),
                   pl.BlockSpec((B_HEADS, tm // KV_CHUNK, VT_ROWS, KV_CHUNK), lambda t: (0, t, 0, 0))],
        out_shape=[jax.ShapeDtypeStruct((B_HEADS, N_TOK, QK_W), BF16),
                   jax.ShapeDtypeStruct((B_HEADS, N_TOK, QK_W), BF16),
                   jax.ShapeDtypeStruct((B_HEADS, N_TOK // KV_CHUNK, VT_ROWS, KV_CHUNK), BF16)],
        compiler_params=pltpu.CompilerParams(dimension_semantics=("parallel",),
                                             vmem_limit_bytes=_vmem_limit(est)),
    )(z, z, z, cos2, sin2, qa_g.reshape(1, Q_LORA), kva_g.reshape(1, KV_LORA), w_uq_p, w_ukv_p)


MLA_TK = 512
MLA_TQ = 2048
MLA_TQG = MXU_DIM
assert SEQ % MLA_TK == 0 and MLA_TK % KV_CHUNK == 0 and SEQ % MLA_TQ == 0 and MLA_TQ % MLA_TQG == 0


def _attend_t(qs, chunks):
    def scores(k, q):
        return lax.dot_general(k, q, (((1,), (1,)), ((), ())), preferred_element_type=F32)

    state = [None] * len(qs)
    s_next = [scores(chunks[0][0], q) for q in qs]
    for j, (_, vts) in enumerate(chunks):
        for g, q in enumerate(qs):
            s = s_next[g]
            if j + 1 < len(chunks):
                s_next[g] = scores(chunks[j + 1][0], q)
            s_max = jnp.max(s, axis=0, keepdims=True)
            m_new = s_max if state[g] is None else jnp.maximum(state[g][0], s_max)
            pv = None
            for n, vt in enumerate(vts):
                p = jnp.exp2(s[n * KV_CHUNK:(n + 1) * KV_CHUNK] - m_new)
                d = jnp.dot(vt, p.astype(BF16), preferred_element_type=F32)
                pv = d if n == 0 else pv + d
            if state[g] is None:
                state[g] = (m_new, pv)
            else:
                m, acc = state[g]
                state[g] = (m_new, jnp.exp2(m - m_new) * acc + pv)
    return [acc[0:V_DIM] / acc[V_DIM:V_DIM + 1] for _, acc in state]


def _mla_lat_kernel(q_ref, kl_ref, vtl_ref, kc_ref, vtc_ref, gb_ref, o_ref):
    per = MLA_TK // KV_CHUNK
    chunks = [(kl_ref[0, j * MLA_TK:(j + 1) * MLA_TK, :], [vtl_ref[0, j * per + n] for n in range(per)])
              for j in range(SEQ // MLA_TK)]
    chunks.append((kc_ref[0], [vtc_ref[0, n] for n in range(CTX_LEN // KV_CHUNK)]))
    groups = [(g * MLA_TQG, (g + 1) * MLA_TQG) for g in range(MLA_TQ // MLA_TQG)]
    outs = _attend_t([q_ref[0, lo:hi, :] for lo, hi in groups], chunks)
    for (lo, hi), o_t in zip(groups, outs):
        o_ref[lo:hi, :] = (o_t.T * _silu(gb_ref[lo:hi, :].astype(F32))).astype(BF16)


def _mla_ctx_kernel(q_ref, kc_ref, vtc_ref, gb_ref, o_ref):
    (o_t,) = _attend_t([q_ref[0]], [(kc_ref[0], [vtc_ref[0, n] for n in range(CTX_LEN // KV_CHUNK)])])
    o = o_t.T
    o_ref[...] = (o * _silu(gb_ref[...].astype(F32))).astype(BF16)


def _mla(q, k, vt, z, with_ctx_queries):
    ctx_tile = N_LAT // CTX_LEN
    tq = MLA_TQ
    nq = SEQ // tq
    est = 2 * (tq * QK_W * 2 + SEQ * QK_W * 2 + SEQ * V_DIM * 2 + CTX_LEN * (QK_W + V_DIM) * 2
               + 2 * tq * HEAD_DIM * 2) + 8 * MLA_TK * tq * 4
    mb = pl.pallas_call(
        _mla_lat_kernel,
        grid=(BATCH, B_HEADS, nq),
        in_specs=[pl.BlockSpec((1, tq, QK_W), lambda b, h, i: (h, b * nq + i, 0)),
                  pl.BlockSpec((1, SEQ, QK_W), lambda b, h, i: (h, b, 0)),
                  pl.BlockSpec((1, SEQ // KV_CHUNK, VT_ROWS, KV_CHUNK), lambda b, h, i: (h, b, 0, 0)),
                  pl.BlockSpec((1, CTX_LEN, QK_W), lambda b, h, i: (h, ctx_tile + b, 0)),
                  pl.BlockSpec((1, CTX_LEN // KV_CHUNK, VT_ROWS, KV_CHUNK), lambda b, h, i: (h, ctx_tile + b, 0, 0)),
                  pl.BlockSpec((tq, HEAD_DIM), lambda b, h, i: (b * nq + i, OFF_GB // HEAD_DIM + h))],
        out_specs=pl.BlockSpec((tq, HEAD_DIM), lambda b, h, i: (b * nq + i, h)),
        out_shape=jax.ShapeDtypeStruct((N_LAT, W_B), BF16),
        compiler_params=pltpu.CompilerParams(dimension_semantics=("parallel", "parallel", "parallel"),
                                             vmem_limit_bytes=_vmem_limit(est)),
    )(q, k, vt, k, vt, z)
    if not with_ctx_queries:
        return mb, None
    mb_ctx = pl.pallas_call(
        _mla_ctx_kernel,
        grid=(BATCH, B_HEADS),
        in_specs=[pl.BlockSpec((1, CTX_LEN, QK_W), lambda b, h: (h, ctx_tile + b, 0)),
                  pl.BlockSpec((1, CTX_LEN, QK_W), lambda b, h: (h, ctx_tile + b, 0)),
                  pl.BlockSpec((1, CTX_LEN // KV_CHUNK, VT_ROWS, KV_CHUNK), lambda b, h: (h, ctx_tile + b, 0, 0)),
                  pl.BlockSpec((CTX_LEN, HEAD_DIM), lambda b, h: (ctx_tile + b, OFF_GB // HEAD_DIM + h))],
        out_specs=pl.BlockSpec((CTX_LEN, HEAD_DIM), lambda b, h: (b, h)),
        out_shape=jax.ShapeDtypeStruct((N_CTX, W_B), BF16),
        compiler_params=pltpu.CompilerParams(dimension_semantics=("parallel", "parallel")),
    )(q, k, vt, z)
    return mb, mb_ctx


NA_QROWS = Q_TILE // GRID_W
NA_KROWS = NA_QROWS + MAX_KH
NA_KEYS = NA_KROWS * GRID_W
assert ROWS >= NA_KROWS and NA_KEYS % MXU_DIM == 0 and NA_KROWS % 2 == 0 and 2 * GRID_W == LANES


def _na_strip_row(j):
    return int(np.clip(j * NA_QROWS - MAX_KH // 2, 0, ROWS - NA_KROWS))


def _na_tables():
    def one(j):
        s = _na_strip_row(j)
        valid = np.zeros((NA_QROWS, NA_KROWS), bool)
        d = np.zeros((NA_QROWS, NA_KROWS), np.int32)
        for a in range(NA_QROWS):
            r = j * NA_QROWS + a
            r0 = int(np.clip(r - MAX_KH // 2, 0, ROWS - MAX_KH))
            for i in range(NA_KROWS):
                valid[a, i] = 0 <= s + i - r0 < MAX_KH
                d[a, i] = s + i - r + (MAX_KH - 1) if valid[a, i] else 0
        return valid, d
    n_tiles = ROWS // NA_QROWS
    first, mid, last = one(0), one(1), one(n_tiles - 1)
    for j in range(1, n_tiles - 1):
        v, d = one(j)
        assert (v == mid[0]).all() and (d == mid[1]).all()
    return np.stack([first[0], mid[0], last[0]]), np.stack([first[1], mid[1], last[1]])


_NA_VALID, _NA_DROW = _na_tables()


def _na_bias_tables(rpb_l):
    col = np.arange(GRID_W)
    c0 = np.clip(col - KW // 2, 0, GRID_W - KW)
    col_ok = (col[None, :] >= c0[:, None]) & (col[None, :] < c0[:, None] + KW)
    dc = np.clip(col[None, :] - col[:, None], -(KW - 1), KW - 1) + (KW - 1)
    onehot = (dc.reshape(-1)[None, :] == np.arange(2 * KW - 1)[:, None]).astype(np.float32)
    t1 = jnp.dot(rpb_l.reshape(C_HEADS * (2 * MAX_KH - 1), 2 * KW - 1), jnp.asarray(onehot),
                 precision=lax.Precision.HIGHEST).reshape(C_HEADS, 2 * MAX_KH - 1, GRID_W, GRID_W)
    t1 = jnp.where(jnp.asarray(col_ok)[None, None], t1, NEG) / C_SCALE
    return jnp.concatenate([t1, t1], axis=-1)


_NA_EXP2_SCALE = C_SCALE * math.log2(math.e)


def _softmax_rows(blocks):
    m = functools.reduce(jnp.maximum, [jnp.max(s, axis=-1, keepdims=True) for s in blocks])
    ps = [jnp.exp2((s - m) * _NA_EXP2_SCALE) for s in blocks]
    return ps, functools.reduce(jnp.add, [jnp.sum(p, axis=-1, keepdims=True) for p in ps])


def _assemble_bias(t_ref, bias_ref):
    lane = lax.broadcasted_iota(jnp.int32, (GRID_W, 2 * GRID_W), 1)
    neg = jnp.full((GRID_W, 2 * GRID_W), NEG / C_SCALE, F32)
    for cls in range(3):
        for a in range(NA_QROWS):
            for pair in range(NA_KROWS // 2):
                halves = [t_ref[0, int(_NA_DROW[cls, a, i])] if _NA_VALID[cls, a, i] else neg
                          for i in (2 * pair, 2 * pair + 1)]
                tile = halves[0] if halves[0] is halves[1] else jnp.where(lane < GRID_W, halves[0], halves[1])
                bias_ref[cls, a * GRID_W:(a + 1) * GRID_W, pair * 2 * GRID_W:(pair + 1) * 2 * GRID_W] = tile


def _natten_kernel(q_ref, kl_ref, vl_ref, kc_ref, vc_ref, t_ref, g_ref, *rest, with_ctx_queries):
    nt = (((1,), (1,)), ((), ()))
    n_tiles = ROWS // NA_QROWS
    if with_ctx_queries:
        qc_ref, gc_ref, o_ref, oc_ref, bias_ref = rest
        _natten_ctx_kernel(qc_ref, kc_ref, vc_ref, gc_ref, oc_ref)
    else:
        o_ref, bias_ref = rest
    _assemble_bias(t_ref, bias_ref)

    def tile(j, carry):
        rows = pl.ds(pl.multiple_of(j * Q_TILE, Q_TILE), Q_TILE)
        strip_row = jnp.clip(j * NA_QROWS - MAX_KH // 2, 0, ROWS - NA_KROWS)
        strip = pl.ds(pl.multiple_of(strip_row * GRID_W, GRID_W), NA_KEYS)
        cls = jnp.where(j == 0, 0, jnp.where(j == n_tiles - 1, 2, 1))
        q = q_ref[rows, :]
        s_nb = lax.dot_general(q, kl_ref[strip, :], nt, preferred_element_type=F32) + bias_ref[cls]
        s_cx = lax.dot_general(q, kc_ref[...], nt, preferred_element_type=F32)
        (p_nb, p_cx), l = _softmax_rows([s_nb, s_cx])
        o = jnp.dot(p_nb.astype(BF16), vl_ref[strip, :], preferred_element_type=F32)
        o += jnp.dot(p_cx.astype(BF16), vc_ref[...], preferred_element_type=F32)
        o_ref[rows, :] = (o / l * _silu(g_ref[rows, :].astype(F32))).astype(BF16)
        return carry

    lax.fori_loop(0, n_tiles, tile, 0, unroll=4)


def _natten_ctx_kernel(q_ref, kc_ref, vc_ref, g_ref, o_ref):
    s =lax.dot_general(q_ref[...], kc_ref[...], (((1,), (1,)), ((), ())), preferred_element_type=F32)
    (p,), l = _softmax_rows([s])
    o = jnp.dot(p.astype(BF16), vc_ref[...], preferred_element_type=F32)
    o_ref[...] = (o / l * _silu(g_ref[...].astype(F32))).astype(BF16)


def _natten(z, bias_tab, with_ctx_queries):
    ctx_tile = N_LAT // CTX_LEN
    hd = HEAD_DIM
    est = (2 * (5 * SEQ * hd * 2 + 2 * CTX_LEN * hd * 2 + 3 * Q_TILE * NA_KEYS * 4)
           + 12 * Q_TILE * (NA_KEYS + CTX_LEN) * 4)
    in_specs = [pl.BlockSpec((SEQ, hd), lambda b, h: (b, OFF_QC // hd + h)),
                pl.BlockSpec((SEQ, hd), lambda b, h: (b, OFF_KC // hd + h)),
                pl.BlockSpec((SEQ, hd), lambda b, h: (b, OFF_VC // hd + h)),
                pl.BlockSpec((CTX_LEN, hd), lambda b, h: (ctx_tile + b, OFF_KC // hd + h)),
                pl.BlockSpec((CTX_LEN, hd), lambda b, h: (ctx_tile + b, OFF_VC // hd + h)),
                pl.BlockSpec((1, 2 * MAX_KH - 1, GRID_W, 2 * GRID_W), lambda b, h: (h, 0, 0, 0)),
                pl.BlockSpec((SEQ, hd), lambda b, h: (b, OFF_GC // hd + h))]
    out_specs = [pl.BlockSpec((SEQ, hd), lambda b, h: (b, h))]
    out_shape = [jax.ShapeDtypeStruct((N_LAT, W_C), BF16)]
    operands = [z, z, z, z, z, bias_tab, z]
    if with_ctx_queries:
        in_specs += [pl.BlockSpec((CTX_LEN, hd), lambda b, h: (ctx_tile + b, OFF_QC // hd + h)),
                     pl.BlockSpec((CTX_LEN, hd), lambda b, h: (ctx_tile + b, OFF_GC // hd + h))]
        out_specs.append(pl.BlockSpec((CTX_LEN, hd), lambda b, h: (b, h)))
        out_shape.append(jax.ShapeDtypeStruct((N_CTX, W_C), BF16))
        operands += [z, z]
    outs = pl.pallas_call(
        functools.partial(_natten_kernel, with_ctx_queries=with_ctx_queries),
        grid=(BATCH, C_HEADS),
        in_specs=in_specs,
        out_specs=out_specs,
        out_shape=out_shape,
        scratch_shapes=[pltpu.VMEM((3, Q_TILE, NA_KEYS), F32)],
        compiler_params=pltpu.CompilerParams(dimension_semantics=("parallel", "parallel"),
                                             vmem_limit_bytes=_vmem_limit(est)),
    )(*operands)
    return (outs[0], outs[1]) if with_ctx_queries else (outs[0], None)


OUT_CHUNK = 512


def _outproj_body(ma, mb, mc, x, w_ref, mod_ref, o_ref):
    gate = mod_ref[0, :, 2 * D_MODEL:3 * D_MODEL]
    for c0 in range(0, D_MODEL, OUT_CHUNK):
        c1 = c0 + OUT_CHUNK
        y = jnp.dot(ma, w_ref[0:W_A, c0:c1], preferred_element_type=F32)
        y += jnp.dot(mb, w_ref[W_A:W_A + W_B, c0:c1], preferred_element_type=F32)
        y += jnp.dot(mc, w_ref[W_A + W_B:D_MODEL, c0:c1], preferred_element_type=F32)
        o_ref[:, c0:c1] = x[:, c0:c1] + gate[:, c0:c1] * y


def _cast_weight_once(w_ref, wb_ref):
    @pl.when(pl.program_id(0) == 0)
    def _():
        def rows(i, carry):
            r = pl.ds(pl.multiple_of(i * W_PREP_ROWS, W_PREP_ROWS), W_PREP_ROWS)
            wb_ref[r, :] = w_ref[r, :].astype(BF16)
            return carry
        lax.fori_loop(0, MIX_W // W_PREP_ROWS, rows, 0)


def _outproj_mid_kernel(ma_ref, mbl_ref, mbc_ref, mcl_ref, mcc_ref, xl_ref, xc_ref, w_ref, mod_ref, o_ref, wb_ref):
    _cast_weight_once(w_ref, wb_ref)
    _outproj_body(ma_ref[...], _pick_rows(mbl_ref, mbc_ref), _pick_rows(mcl_ref, mcc_ref),
                  _pick_rows(xl_ref, xc_ref), wb_ref, mod_ref, o_ref)


def _outproj_final_kernel(ma_ref, mb_ref, mc_ref, x_ref, w_ref, mod_ref, fg_ref, o_ref, wb_ref):
    _cast_weight_once(w_ref, wb_ref)
    _outproj_body(ma_ref[...], mb_ref[...], mc_ref[...], x_ref[...], wb_ref, mod_ref, o_ref)
    o_ref[...] = _rms(o_ref[...], fg_ref[...])


def _outproj_mid(ma, mb, mb_ctx, mc, mc_ctx, x_lat, x_ctx, ctx_block, w_out, l, mods_l):
    tm = ROW_TILE
    est = (3 * tm * D_MODEL * 2 + D_MODEL * D_MODEL * 6 + 5 * tm * D_MODEL * 4 + 3 * tm * OUT_CHUNK * 4)
    lat = lambda w: pl.BlockSpec((tm, w), lambda t: (_lat_tile(t), 0))
    ctx = lambda w, blk: pl.BlockSpec((tm, w), lambda t: (blk, 0), pipeline_mode=pl.Buffered(1))
    return pl.pallas_call(
        _outproj_mid_kernel,
        grid=(ROW_TILES,),
        in_specs=[pl.BlockSpec((tm, W_A), lambda t: (t, 0)),
                  lat(W_B), ctx(W_B, 0), lat(W_C), ctx(W_C, 0), lat(D_MODEL), ctx(D_MODEL, ctx_block),
                  pl.BlockSpec((MIX_W, D_MODEL), lambda t: (l, 0), pipeline_mode=pl.Buffered(1)),
                  pl.BlockSpec((1, 1, 3 * D_MODEL), lambda t: (_mod_row(t), 0, 0))],
        out_specs=pl.BlockSpec((tm, D_MODEL), lambda t: (t, 0)),
        out_shape=jax.ShapeDtypeStruct((N_TOK, D_MODEL), F32),
        scratch_shapes=[pltpu.VMEM((MIX_W, D_MODEL), BF16)],
        compiler_params=pltpu.CompilerParams(dimension_semantics=("arbitrary",),
                                             vmem_limit_bytes=_vmem_limit(est)),
    )(ma, mb, mb_ctx, mc, mc_ctx, x_lat, x_ctx, w_out.reshape(DEPTH * MIX_W, D_MODEL),
      mods_l.reshape(MOD_ROWS, 1, 3 * D_MODEL))


def _outproj_final(ma, mb, mc, x_lat, w_out, l, mods_l, final_g):
    tm = ROW_TILE
    est = (2 * tm * D_MODEL * 2 + D_MODEL * D_MODEL * 6 + 4 * tm * D_MODEL * 4 + 3 * tm * OUT_CHUNK * 4)
    return pl.pallas_call(
        _outproj_final_kernel,
        grid=(LAT_TILES,),
        in_specs=[pl.BlockSpec((tm, W_A), lambda t: (t, 0)),
                  pl.BlockSpec((tm, W_B), lambda t: (t, 0)),
                  pl.BlockSpec((tm, W_C), lambda t: (t, 0)),
                  pl.BlockSpec((tm, D_MODEL), lambda t: (t, 0)),
                  pl.BlockSpec((MIX_W, D_MODEL), lambda t: (l, 0), pipeline_mode=pl.Buffered(1)),
                  pl.BlockSpec((1, 1, 3 * D_MODEL), lambda t: (_mod_row(t), 0, 0)),
                  pl.BlockSpec((1, D_MODEL), lambda t: (0, 0))],
        out_specs=pl.BlockSpec((tm, D_MODEL), lambda t: (t, 0)),
        out_shape=jax.ShapeDtypeStruct((N_LAT, D_MODEL), F32),
        scratch_shapes=[pltpu.VMEM((MIX_W, D_MODEL), BF16)],
        compiler_params=pltpu.CompilerParams(dimension_semantics=("arbitrary",),
                                             vmem_limit_bytes=_vmem_limit(est)),
    )(ma, mb, mc, x_lat, w_out.reshape(DEPTH * MIX_W, D_MODEL), mods_l.reshape(MOD_ROWS, 1, 3 * D_MODEL),
      final_g.reshape(1, D_MODEL))


W_PREP_ROWS = MXU_DIM


def _prep_w_in_kernel(w_ref, kr_ref, o_ref):
    h0, h1 = _SRC["u"][0], _SRC["ckv"][1]
    t0, t1 = _SRC["gb"][0], _SRC["gc"][1]
    o_ref[0:h1 - h0, :] = w_ref[h0:h1, :].astype(BF16)
    o_ref[GM_W + OFF_GB:GM_W + OFF_KR, :] = w_ref[t0:t1, :].astype(BF16)
    o_ref[GM_W + OFF_KR:W_ROWS, :] = kr_ref[...]


def _prep_w_in(w_in, l):
    assert _SRC["ckv"][1] == GM_W + OFF_GB and OFF_GB + (_SRC["gc"][1] - _SRC["gb"][0]) == OFF_KR
    w_t = jnp.swapaxes(w_in, 1, 2)
    kr = w_t[l, _SRC["kr"][0]:_SRC["kr"][1], :]
    kr_sw = jnp.concatenate([kr[_Q4:2 * _Q4], kr[0:_Q4], kr[3 * _Q4:4 * _Q4], kr[2 * _Q4:3 * _Q4]], axis=0)
    kr4 = jnp.concatenate([kr, kr, kr_sw, kr_sw], axis=0).astype(BF16)
    cols = W_PREP_ROWS
    est = 2 * cols * (IN_W * 4 + W_ROWS * 2 + 4 * ROPE_DIM * 2) + 2 * cols * IN_W * 4
    return pl.pallas_call(
        _prep_w_in_kernel,
        grid=(D_MODEL // cols,),
        in_specs=[pl.BlockSpec((IN_W, cols), lambda i: (l, i)),
                  pl.BlockSpec((4 * ROPE_DIM, cols), lambda i: (0, i))],
        out_specs=pl.BlockSpec((W_ROWS, cols), lambda i: (0, i)),
        out_shape=jax.ShapeDtypeStruct((W_ROWS, D_MODEL), BF16),
        compiler_params=pltpu.CompilerParams(dimension_semantics=("parallel",),
                                             vmem_limit_bytes=_vmem_limit(est)),
    )(w_t.reshape(DEPTH * IN_W, D_MODEL), kr4)


def _prep_w_uq(w):
    w3 = w.reshape(Q_LORA, B_HEADS, NOPE_DIM + ROPE_DIM)
    nope = w3[:, :, :NOPE_DIM].reshape(Q_LORA, B_HEADS * NOPE_DIM)
    rope = w3[:, :, NOPE_DIM:]
    rope_sw = jnp.concatenate([rope[..., _Q4:2 * _Q4], rope[..., 0:_Q4],
                               rope[..., 3 * _Q4:4 * _Q4], rope[..., 2 * _Q4:3 * _Q4]], axis=-1)
    return jnp.concatenate([nope, rope.reshape(Q_LORA, -1), rope_sw.reshape(Q_LORA, -1)], axis=1).astype(BF16)


def _prep_w_ukv(w):
    w3 = w.reshape(KV_LORA, B_HEADS, NOPE_DIM + V_DIM)
    return jnp.concatenate([w3[:, :, :NOPE_DIM].reshape(KV_LORA, -1),
                            w3[:, :, NOPE_DIM:].reshape(KV_LORA, -1)], axis=1).astype(BF16)


def _rope_tables():
    t = np.arange(SEQ)
    row = (t // GRID_W).astype(np.float32)
    col = (t % GRID_W).astype(np.float32)
    inv = (np.float32(ROPE_THETA) ** (-np.arange(_Q4, dtype=np.float32) / np.float32(_Q4))).astype(np.float32)
    ar, ac = row[:, None] * inv, col[:, None] * inv
    cos = np.concatenate([np.cos(ar), np.cos(ar), np.cos(ac), np.cos(ac)], axis=1)
    sin = np.concatenate([-np.sin(ar), np.sin(ar), -np.sin(ac), np.sin(ac)], axis=1)
    cos = np.concatenate([cos, np.ones((ROW_TILE, ROPE_DIM))], axis=0).astype(np.float32)
    sin = np.concatenate([sin, np.zeros((ROW_TILE, ROPE_DIM))], axis=0).astype(np.float32)
    return np.concatenate([cos, cos], axis=1), np.concatenate([sin, sin], axis=1)


_COS2, _SIN2 = _rope_tables()


def _rope_tile(t):
    return jnp.where(t < LAT_TILES, t % TILES_PER_SAMPLE, TILES_PER_SAMPLE)


def kernel(x, c, ctx, c_ctx, w_ada, b_ada, norm_g, w_in, qa_g, kva_g, w_uq, w_ukv, sgu_g, w_s, b_s, rpb,
           w_out, final_g):
    assert x.shape == (BATCH, SEQ, D_MODEL) and ctx.shape == (BATCH, CTX_LEN, D_MODEL)
    assert w_in.shape == (DEPTH, D_MODEL, IN_W)
    cc = jnp.concatenate([c, c_ctx[None, :], jnp.zeros((MOD_ROWS - BATCH - 1, D_MODEL), F32)], axis=0)
    mods = _modulation(cc, w_ada, b_ada)
    cos2, sin2 = jnp.asarray(_COS2), jnp.asarray(_SIN2)

    x_lat, x_ctx, ctx_block = x.reshape(N_LAT, D_MODEL), ctx.reshape(N_CTX, D_MODEL), 0
    for l in range(DEPTH):
        last = l == DEPTH - 1
        b_s_exp = jnp.repeat(b_s[l].T, HEAD_DIM, axis=1)
        z, ma = _inproj(x_lat, x_ctx, ctx_block, mods[l], norm_g[l], _prep_w_in(w_in, l),
                        sgu_g[l], w_s[l].astype(BF16), b_s_exp)
        q, k, vt = _upproj(z, cos2, sin2, qa_g[l], kva_g[l], _prep_w_uq(w_uq[l]), _prep_w_ukv(w_ukv[l]))
        mb, mb_ctx = _mla(q, k, vt, z, with_ctx_queries=not last)
        mc, mc_ctx = _natten(z, _na_bias_tables(rpb[l]), with_ctx_queries=not last)
        if last:
            out = _outproj_final(ma, mb, mc, x_lat, w_out, l, mods[l], final_g)
            return out.reshape(BATCH, SEQ, D_MODEL)
        xf = _outproj_mid(ma, mb, mb_ctx, mc, mc_ctx, x_lat, x_ctx, ctx_block, w_out, l, mods[l])
        x_lat, x_ctx, ctx_block = xf, xf, LAT_TILES
```
